```python
import math
import jax, jax.numpy as jnp
from jax import lax
import numpy as np

D_MODEL = 1024
BATCH = 2
SEQ = 8192
DEPTH = 2
DEC_BATCH = 16
DEC_SEQ = 64
PAST_LEN = 1024

CHUNK = 64
QBLK = 128
N_EVEN = (DEPTH + 1) // 2
N_ODD = DEPTH // 2
H_A = 4
DK = 128
DV = 128
QK_W = H_A * DK
V_W = H_A * DV
QKV_W = 2 * QK_W + V_W
SHORT_CONV = 4
C_B = D_MODEL // 2
CONF_K = 31
IN_E = QKV_W + V_W + 2 * H_A + 2 * C_B
MIX_E = V_W + C_B
H_C = 8
DH = D_MODEL // H_C
D_FF = 4 * D_MODEL
EPS = 1e-6

kernel_name = 'hybrid_stream_deltanet_conformer_stickbreak_step'


def rmsnorm(x, g):
    xf = x.astype(jnp.float32)
    y = xf * lax.rsqrt(jnp.mean(xf * xf, axis=-1, keepdims=True) + EPS)
    return (y * g.astype(jnp.float32)).astype(x.dtype)


def l2norm(x):
    return x * lax.rsqrt(jnp.sum(x * x, axis=-1, keepdims=True) + EPS)


def layernorm(x, g, b):
    mu = jnp.mean(x, axis=-1, keepdims=True)
    xc = x - mu
    var = jnp.mean(xc * xc, axis=-1, keepdims=True)
    return xc * lax.rsqrt(var + EPS) * g.astype(jnp.float32) + b.astype(jnp.float32)


def causal_dwconv(x, buf, w):
    xp = jnp.concatenate([buf.astype(x.dtype), x], axis=1)
    y = lax.conv_general_dilated(xp.astype(jnp.float32), w.astype(jnp.float32)[:, None, :],
                                 window_strides=(1,), padding='VALID',
                                 dimension_numbers=('NWC', 'WIO', 'NWC'),
                                 feature_group_count=x.shape[-1])
    return y, xp[:, -(w.shape[0] - 1):, :]


def gated_delta(q, k, v, g, beta, s0, chunk):
    B, L, H, _ = q.shape
    N = L // chunk
    q = q * (DK ** -0.5)

    def blocks(t):
        return t.reshape(B, N, chunk, H, -1).transpose(0, 3, 1, 2, 4)

    qc, kc, vc = blocks(q), blocks(k), blocks(v)
    gc = g.reshape(B, N, chunk, H).transpose(0, 3, 1, 2)
    bc = beta.reshape(B, N, chunk, H).transpose(0, 3, 1, 2)
    gcum = jnp.cumsum(gc, axis=-1)
    causal = jnp.tril(jnp.ones((chunk, chunk), dtype=bool))
    strict = jnp.tril(jnp.ones((chunk, chunk), dtype=bool), -1)
    diff = gcum[..., :, None] - gcum[..., None, :]
    decay = jnp.where(causal, jnp.exp(jnp.where(causal, diff, 0.0)), 0.0)
    kb = kc * bc[..., None]
    kk = jnp.where(strict, jnp.einsum('bhnid,bhnjd->bhnij', kb, kc) * decay, 0.0)
    eye = jnp.eye(chunk, dtype=jnp.float32)
    T = lax.linalg.triangular_solve(kk + eye, jnp.broadcast_to(eye, kk.shape),
                                    left_side=True, lower=True, unit_diagonal=True)
    u = T @ (vc * bc[..., None])
    w = T @ (kb * jnp.exp(gcum)[..., None])
    qk = jnp.where(causal, jnp.einsum('bhnid,bhnjd->bhnij', qc, kc) * decay, 0.0)
    qg = qc * jnp.exp(gcum)[..., None]
    kg = kc * jnp.exp(gcum[..., -1:] - gcum)[..., None]
    glast = jnp.exp(gcum[..., -1])
    xs = (jnp.moveaxis(qg, 2, 0), jnp.moveaxis(kg, 2, 0), jnp.moveaxis(u, 2, 0),
          jnp.moveaxis(w, 2, 0), jnp.moveaxis(qk, 2, 0), jnp.moveaxis(glast, 2, 0))

    def step(S, inp):
        qg_i, kg_i, u_i, w_i, qk_i, gl_i = inp
        v_new = u_i - w_i @ S
        o_i = qg_i @ S + qk_i @ v_new
        S = S * gl_i[..., None, None] + jnp.einsum('bhck,bhcv->bhkv', kg_i, v_new)
        return S, o_i

    S, o = lax.scan(step, s0.astype(jnp.float32), xs)
    o = o.transpose(1, 0, 3, 2, 4).reshape(B, L, H, DV)
    return o, S


def even_mixer(h, s0, qkv_buf, dw_buf, w_in, conv_qkv, a_log, dt_bias, onorm,
               dw_w, dw_b, ln_g, ln_b, w_out):
    B, L, _ = h.shape
    proj = h @ w_in
    qkv_pre, z, a, b, glu = jnp.split(
        proj, [QKV_W, QKV_W + V_W, QKV_W + V_W + H_A, QKV_W + V_W + 2 * H_A], axis=-1)
    qkv, qkv_buf_new = causal_dwconv(qkv_pre, qkv_buf, conv_qkv)
    qkv = jax.nn.silu(qkv)
    q = l2norm(qkv[..., :QK_W].reshape(B, L, H_A, DK))
    k = l2norm(qkv[..., QK_W:2 * QK_W].reshape(B, L, H_A, DK))
    v = qkv[..., 2 * QK_W:].reshape(B, L, H_A, DV)
    g = -jnp.exp(a_log.astype(jnp.float32)) * jax.nn.softplus(
        a.astype(jnp.float32) + dt_bias.astype(jnp.float32))
    beta = jax.nn.sigmoid(b.astype(jnp.float32))
    o, s_new = gated_delta(q, k, v, g, beta, s0, min(CHUNK, L))
    o = rmsnorm(o, onorm) * jax.nn.silu(z.astype(jnp.float32).reshape(B, L, H_A, DV))
    o = o.reshape(B, L, V_W)
    gluf = glu.astype(jnp.float32)
    ug = (gluf[..., :C_B] * jax.nn.sigmoid(gluf[..., C_B:])).astype(h.dtype)
    c, dw_buf_new = causal_dwconv(ug, dw_buf, dw_w)
    c = jax.nn.silu(layernorm(c + dw_b.astype(jnp.float32), ln_g, ln_b))
    mix = jnp.concatenate([o, c], axis=-1).astype(h.dtype) @ w_out
    return mix, s_new, qkv_buf_new, dw_buf_new


def stick_breaking(q, k, v, q_offset):
    B, H, Lq, dh = q.shape
    Lk = k.shape[2]
    blk = min(QBLK, Lq)
    nb = Lq // blk
    qb = q.reshape(B, H, nb, blk, dh).transpose(2, 0, 1, 3, 4)
    kpos = jnp.arange(Lk)
    scale = dh ** -0.5

    def one_block(args):
        qi, bi = args
        z = jnp.einsum('bhqd,bhkd->bhqk', qi, k) * scale
        qpos = q_offset + bi * blk + jnp.arange(blk)
        mask = kpos[None, :] < qpos[:, None]
        log_rest = jnp.where(mask, jax.nn.log_sigmoid(-z), 0.0)
        tail = lax.cumsum(log_rest, axis=3, reverse=True) - log_rest
        wts = jnp.where(mask, jnp.exp(jax.nn.log_sigmoid(z) + tail), 0.0)
        return jnp.einsum('bhqk,bhkd->bhqd', wts, v)

    o = lax.map(one_block, (qb, jnp.arange(nb)))
    return o.transpose(1, 2, 0, 3, 4).reshape(B, H, Lq, dh)


def odd_mixer(h, k_past, v_past, w_qkv, qn, kn, w_out):
    B, L, _ = h.shape
    qkv = (h @ w_qkv).reshape(B, L, 3, H_C, DH)
    q = rmsnorm(qkv[:, :, 0], qn).transpose(0, 2, 1, 3)
    k = rmsnorm(qkv[:, :, 1], kn).transpose(0, 2, 1, 3)
    v = qkv[:, :, 2].transpose(0, 2, 1, 3)
    if k_past is None:
        k_all, v_all, offset = k, v, 0
    else:
        k_all = jnp.concatenate([k_past.astype(k.dtype), k], axis=2)
        v_all = jnp.concatenate([v_past.astype(v.dtype), v], axis=2)
        offset = k_past.shape[2]
    o = stick_breaking(q.astype(jnp.float32), k_all.astype(jnp.float32),
                       v_all.astype(jnp.float32), offset)
    o = o.transpose(0, 2, 1, 3).reshape(B, L, H_C * DH).astype(h.dtype)
    return o @ w_out, k, v


def sqrelu_mlp(h, w_up, w_down):
    r = jax.nn.relu(h @ w_up)
    return (r * r) @ w_down


def setup_inputs(seed: int = 0) -> dict:
    key = jax.random.key(seed)
    ks = jax.random.split(key, 32)

    def nrm(k, shape, scale):
        return jax.random.normal(k, shape, jnp.float32) * scale

    dt = jnp.exp(jax.random.uniform(ks[12], (N_EVEN, H_A), jnp.float32,
                                    minval=math.log(1e-3), maxval=math.log(1e-1)))
    return {
        'x_prompt': nrm(ks[0], (BATCH, SEQ, D_MODEL), 1.0),
        'x_sample': nrm(ks[1], (DEC_BATCH, DEC_SEQ, D_MODEL), 1.0),
        'state_delta': nrm(ks[2], (N_EVEN, DEC_BATCH, H_A, DK, DV), 0.05),
        'state_qkv_conv': nrm(ks[3], (N_EVEN, DEC_BATCH, SHORT_CONV - 1, QKV_W), 1.0),
        'state_dw_conv': nrm(ks[4], (N_EVEN, DEC_BATCH, CONF_K - 1, C_B), 0.5),
        'cache_k': nrm(ks[5], (N_ODD, DEC_BATCH, H_C, PAST_LEN, DH), 1.0),
        'cache_v': nrm(ks[6], (N_ODD, DEC_BATCH, H_C, PAST_LEN, DH), 1.0),
        'norm_mix': 1.0 + nrm(ks[7], (DEPTH, D_MODEL), 0.02),
        'norm_mlp': 1.0 + nrm(ks[8], (DEPTH, D_MODEL), 0.02),
        'w_in_e': nrm(ks[9], (N_EVEN, D_MODEL, IN_E), D_MODEL ** -0.5),
        'conv_qkv_e': nrm(ks[10], (N_EVEN, SHORT_CONV, QKV_W), SHORT_CONV ** -0.5),
        'a_log_e': jnp.log(jax.random.uniform(ks[11], (N_EVEN, H_A), jnp.float32, minval=1.0, maxval=16.0)),
        'dt_bias_e': dt + jnp.log(-jnp.expm1(-dt)),
        'onorm_e': 1.0 + nrm(ks[13], (N_EVEN, DV), 0.02),
        'dw_w_e': nrm(ks[14], (N_EVEN, CONF_K, C_B), CONF_K ** -0.5),
        'dw_b_e': nrm(ks[15], (N_EVEN, C_B), 0.02),
        'ln_g_e': 1.0 + nrm(ks[16], (N_EVEN, C_B), 0.02),
        'ln_b_e': nrm(ks[17], (N_EVEN, C_B), 0.02),
        'w_out_e': nrm(ks[18], (N_EVEN, MIX_E, D_MODEL), MIX_E ** -0.5),
        'w_qkv_o': nrm(ks[19], (N_ODD, D_MODEL, 3 * H_C * DH), D_MODEL ** -0.5),
        'qn_o': 1.0 + nrm(ks[20], (N_ODD, DH), 0.02),
        'kn_o': 1.0 + nrm(ks[21], (N_ODD, DH), 0.02),
        'w_out_o': nrm(ks[22], (N_ODD, H_C * DH, D_MODEL), (H_C * DH) ** -0.5),
        'w_up': nrm(ks[23], (DEPTH, D_MODEL, D_FF), D_MODEL ** -0.5),
        'w_down': nrm(ks[24], (DEPTH, D_FF, D_MODEL), D_FF ** -0.5),
    }


def reference(x_prompt, x_sample, state_delta, state_qkv_conv, state_dw_conv, cache_k, cache_v,
              norm_mix, norm_mlp, w_in_e, conv_qkv_e, a_log_e, dt_bias_e, onorm_e, dw_w_e, dw_b_e,
              ln_g_e, ln_b_e, w_out_e, w_qkv_o, qn_o, kn_o, w_out_o, w_up, w_down):
    xp, xs = x_prompt, x_sample
    Bp = xp.shape[0]
    pd, pq, pw, pk, pv = [], [], [], [], []
    sd, sq, sw, sk, sv = [], [], [], [], []
    for i in range(DEPTH):
        hp = rmsnorm(xp, norm_mix[i])
        hs = rmsnorm(xs, norm_mix[i])
        if i % 2 == 0:
            e = i // 2
            wts = (w_in_e[e], conv_qkv_e[e], a_log_e[e], dt_bias_e[e], onorm_e[e],
                   dw_w_e[e], dw_b_e[e], ln_g_e[e], ln_b_e[e], w_out_e[e])
            s0 = jnp.zeros((Bp, H_A, DK, DV), jnp.float32)
            qb0 = jnp.zeros((Bp, SHORT_CONV - 1, QKV_W), xp.dtype)
            db0 = jnp.zeros((Bp, CONF_K - 1, C_B), xp.dtype)
            mp, d1, q1, c1 = even_mixer(hp, s0, qb0, db0, *wts)
            ms, d2, q2, c2 = even_mixer(hs, state_delta[e], state_qkv_conv[e], state_dw_conv[e], *wts)
            pd.append(d1); pq.append(q1); pw.append(c1)
            sd.append(d2); sq.append(q2); sw.append(c2)
        else:
            o = i // 2
            mp, k1, v1 = odd_mixer(hp, None, None, w_qkv_o[o], qn_o[o], kn_o[o], w_out_o[o])
            ms, k2, v2 = odd_mixer(hs, cache_k[o], cache_v[o], w_qkv_o[o], qn_o[o], kn_o[o], w_out_o[o])
            pk.append(k1); pv.append(v1)
            sk.append(k2); sv.append(v2)
        xp = xp + mp
        xs = xs + ms
        xp = xp + sqrelu_mlp(rmsnorm(xp, norm_mlp[i]), w_up[i], w_down[i])
        xs = xs + sqrelu_mlp(rmsnorm(xs, norm_mlp[i]), w_up[i], w_down[i])
    return (xp, xs,
            jnp.stack(pd), jnp.stack(pq), jnp.stack(pw), jnp.stack(pk), jnp.stack(pv),
            jnp.stack(sd), jnp.stack(sq), jnp.stack(sw), jnp.stack(sk), jnp.stack(sv))
```

```python
import functools

import jax
import jax.numpy as jnp
from jax import lax
from jax.experimental import pallas as pl
from jax.experimental.pallas import tpu as pltpu

F32 = jnp.float32
BF16 = jnp.bfloat16
EPS = 1e-6
CHUNK = 64
LANES = 128
PAIR = 2 * CHUNK
QBUF_OFF = 8
DBUF_OFF = 32
DW_ROWS = 32
VMEM_LIMIT = 56 * 1024 * 1024
LOG_W_FLOOR = -104.0


def _mm(a, b):
    return jnp.dot(a.astype(BF16), b.astype(BF16), preferred_element_type=F32)


def _mm_nt(a, b):
    return lax.dot_general(a.astype(BF16), b.astype(BF16), (((1,), (1,)), ((), ())),
                           preferred_element_type=F32)


def _sigmoid(x):
    return 1.0 / (1.0 + jnp.exp(-x))


def _silu(x):
    return x * _sigmoid(x)


def _softplus(x):
    return jnp.maximum(x, 0.0) + jnp.log1p(jnp.exp(-jnp.abs(x)))


def _rms_scale(x):
    return lax.rsqrt(jnp.mean(x * x, axis=-1, keepdims=True) + EPS)


def _const_spec(shape):
    nd = len(shape)
    return pl.BlockSpec(shape, lambda *_: (0,) * nd, pipeline_mode=pl.Buffered(1))


def _even_pre_kernel(x_ref, g_ref, w_ref, cw_ref, alog_ref, dtb_ref, dww_ref, dwb_ref, lng_ref, lnb_ref,
                     qbuf_ref, dbuf_ref,
                     qkv_ref, gcol_ref, z_ref, c_ref, qlast_ref, dlast_ref,
                     qext, uext, *, tl, qkv_w, v_w, c_b, n_heads):
    kq = cw_ref.shape[0]
    kd = dww_ref.shape[0]

    @pl.when(pl.program_id(1) == 0)
    def _():
        qext[QBUF_OFF - (kq - 1):QBUF_OFF, :] = qbuf_ref[0]
        uext[DBUF_OFF - (kd - 1):DBUF_OFF, :] = dbuf_ref[0]

    x = x_ref[0]
    h = (x * _rms_scale(x) * g_ref[...]).astype(BF16)

    glu0 = qkv_w + v_w
    qext[QBUF_OFF:QBUF_OFF + tl, :] = jnp.dot(h, w_ref[:, 0:qkv_w], preferred_element_type=F32)
    z_ref[0] = jnp.dot(h, w_ref[:, qkv_w:glu0], preferred_element_type=F32)
    ga = jnp.dot(h, w_ref[:, glu0:glu0 + c_b], preferred_element_type=F32)
    gb = jnp.dot(h, w_ref[:, glu0 + c_b:glu0 + 2 * c_b], preferred_element_type=F32)
    uext[DBUF_OFF:DBUF_OFF + tl, :] = ga * _sigmoid(gb)
    ab = jnp.dot(h, w_ref[:, glu0 + 2 * c_b:glu0 + 2 * c_b + LANES], preferred_element_type=F32)

    for s in range(qkv_w // LANES):
        cols = slice(s * LANES, (s + 1) * LANES)
        y = None
        for j in range(kq):
            r0 = QBUF_OFF - (kq - 1) + j
            t = cw_ref[j:j + 1, cols] * qext[r0:r0 + tl, cols]
            y = t if y is None else y + t
        y = _silu(y)
        if s < 2 * n_heads:
            y = y * lax.rsqrt(jnp.sum(y * y, axis=-1, keepdims=True) + EPS)
        qkv_ref[0, :, cols] = y
    qlast = qext[QBUF_OFF + tl - (kq - 1):QBUF_OFF + tl, :]
    qlast_ref[0] = qlast
    qext[QBUF_OFF - (kq - 1):QBUF_OFF, :] = qlast

    for r in range(tl // DW_ROWS):
        acc = None
        for j in range(kd):
            r0 = DBUF_OFF - (kd - 1) + j + r * DW_ROWS
            t = dww_ref[j:j + 1, :] * uext[r0:r0 + DW_ROWS, :]
            acc = t if acc is None else acc + t
        cpre = acc + dwb_ref[...]
        mu = jnp.mean(cpre, axis=-1, keepdims=True)
        xc = cpre - mu
        var = jnp.mean(xc * xc, axis=-1, keepdims=True)
        y = xc * lax.rsqrt(var + EPS) * lng_ref[...] + lnb_ref[...]
        c_ref[0, r * DW_ROWS:(r + 1) * DW_ROWS, :] = _silu(y).astype(c_ref.dtype)
    dlast = uext[DBUF_OFF + tl - (kd - 1):DBUF_OFF + tl, :]
    dlast_ref[0] = dlast
    uext[DBUF_OFF - (kd - 1):DBUF_OFF, :] = dlast

    lane = lax.broadcasted_iota(jnp.int32, (tl, LANES), 1)
    g = -jnp.exp(alog_ref[...]) * _softplus(ab + dtb_ref[...])
    g = jnp.where(lane < n_heads, g, 0.0)
    beta = _sigmoid(ab)
    row = lax.broadcasted_iota(jnp.int32, (tl, tl), 0)
    col = lax.broadcasted_iota(jnp.int32, (tl, tl), 1)
    tri = (((row ^ col) < CHUNK) & (col <= row)).astype(BF16)
    g_hi = g.astype(BF16)
    g_r1 = g - g_hi.astype(F32)
    g_mid = g_r1.astype(BF16)
    g_lo = (g_r1 - g_mid.astype(F32)).astype(BF16)
    gcum = (jnp.dot(tri, g_hi, preferred_element_type=F32)
            + jnp.dot(tri, g_mid, preferred_element_type=F32)
            + jnp.dot(tri, g_lo, preferred_element_type=F32))
    gcol_ref[0] = jnp.where(lane < n_heads, gcum, jnp.where(lane < 2 * n_heads, beta, 0.0))


def _even_pre(x, gain, w_all, conv_w, alog_pad, dtb_pad, dw_w, dw_b, ln_g, ln_b, qbuf, dbuf, *, n_heads, qkv_w, v_w, c_b):
    n_seq, L, D = x.shape
    tl = min(256, L)
    assert L % tl == 0 and tl % DW_ROWS == 0 and tl % CHUNK == 0
    kq, kd = conv_w.shape[0], dw_w.shape[0]
    assert kq - 1 <= QBUF_OFF and kd - 1 <= DBUF_OFF and tl >= kd - 1
    nw = w_all.shape[1]
    kern = functools.partial(_even_pre_kernel, tl=tl, qkv_w=qkv_w, v_w=v_w, c_b=c_b, n_heads=n_heads)
    seq_blk = lambda w: pl.BlockSpec((1, tl, w), lambda s, j: (s, j, 0))
    per_seq = lambda r, w: pl.BlockSpec((1, r, w), lambda s, j: (s, 0, 0))
    return pl.pallas_call(
        kern,
        grid=(n_seq, L // tl),
        in_specs=[seq_blk(D), _const_spec((1, D)), _const_spec((D, nw)), _const_spec((kq, qkv_w)),
                  _const_spec((1, LANES)), _const_spec((1, LANES)), _const_spec((kd, c_b)),
                  _const_spec((1, c_b)), _const_spec((1, c_b)), _const_spec((1, c_b)),
                  per_seq(kq - 1, qkv_w), per_seq(kd - 1, c_b)],
        out_specs=[seq_blk(qkv_w), seq_blk(LANES), seq_blk(v_w), seq_blk(c_b),
                   per_seq(kq - 1, qkv_w), per_seq(kd - 1, c_b)],
        out_shape=[jax.ShapeDtypeStruct((n_seq, L, qkv_w), F32),
                   jax.ShapeDtypeStruct((n_seq, L, LANES), F32),
                   jax.ShapeDtypeStruct((n_seq, L, v_w), F32),
                   jax.ShapeDtypeStruct((n_seq, L, c_b), BF16),
                   jax.ShapeDtypeStruct((n_seq, kq - 1, qkv_w), F32),
                   jax.ShapeDtypeStruct((n_seq, kd - 1, c_b), F32)],
        scratch_shapes=[pltpu.VMEM((QBUF_OFF + tl, qkv_w), F32), pltpu.VMEM((DBUF_OFF + tl, c_b), F32)],
        compiler_params=pltpu.CompilerParams(dimension_semantics=("arbitrary", "arbitrary"),
                                             vmem_limit_bytes=VMEM_LIMIT),
        name="even_pre",
    )(x, gain, w_all, conv_w, alog_pad, dtb_pad, dw_w, dw_b, ln_g, ln_b, qbuf, dbuf)


def _unit_lower_inverse(a, row, col):
    x = row ^ col
    eye = (row == col).astype(F32)
    a8 = jnp.where(x < 8, a, 0.0)
    a8_2 = _mm(a8, a8)
    a8_4 = _mm(a8_2, a8_2)
    p = eye - a8
    p = p + _mm(p, a8_2)
    d = p + _mm(p, a8_4)
    s = 8
    while s < CHUNK:
        off = jnp.where((x >= s) & (x < 2 * s), a, 0.0)
        d = d - _mm(d, _mm(off, d))
        s *= 2
    return d


def _delta_kernel(qkv_ref, gcol_ref, z_ref, onorm_ref, sin_ref, o_ref, sout_ref, *, tb, n_heads, chained):
    dk = dv = LANES
    qk_w = n_heads * dk
    if chained:
        @pl.when(pl.program_id(1) == 0)
        def _():
            sout_ref[...] = sin_ref[...]

    row = lax.broadcasted_iota(jnp.int32, (PAIR, PAIR), 0)
    col = lax.broadcasted_iota(jnp.int32, (PAIR, PAIR), 1)
    same = (row ^ col) < CHUNK
    causal = same & (col <= row)
    strict = same & (col < row)

    for p in range(tb // PAIR):
        rows = slice(p * PAIR, (p + 1) * PAIR)
        gslab = gcol_ref[0, rows, :]
        for h in range(n_heads):
            q = qkv_ref[0, rows, h * dk:(h + 1) * dk] * (dk ** -0.5)
            k = qkv_ref[0, rows, qk_w + h * dk:qk_w + (h + 1) * dk]
            v = qkv_ref[0, rows, 2 * qk_w + h * dv:2 * qk_w + (h + 1) * dv]
            gc = jnp.broadcast_to(gslab[:, h:h + 1], (PAIR, PAIR))
            bt = jnp.broadcast_to(gslab[:, n_heads + h:n_heads + h + 1], (PAIR, PAIR))
            diff = gc - gc.T
            decay = jnp.where(causal, jnp.exp(jnp.where(causal, diff, 0.0)), 0.0)
            kb = k * bt
            a = jnp.where(strict, _mm_nt(kb, k) * decay, 0.0)
            qk = jnp.where(causal, _mm_nt(q, k) * decay, 0.0)
            t = _unit_lower_inverse(a, row, col)
            eg = jnp.exp(gc)
            u = _mm(t, v * bt)
            w = _mm(t, kb * eg)
            qg = q * eg
            gl_rows = [jnp.broadcast_to(gc[(c + 1) * CHUNK - 1:(c + 1) * CHUNK, :], (PAIR, PAIR)) for c in range(2)]
            gcl = jnp.where(row < CHUNK, gl_rows[0], gl_rows[1])
            kg_t = (k * jnp.exp(gcl - gc)).T
            o_parts = []
            for c in range(2):
                r = slice(c * CHUNK, (c + 1) * CHUNK)
                si = 0 if chained else c
                s_old = sout_ref[si, h] if chained else sin_ref[si, h]
                ws_qs = _mm(jnp.concatenate([w[r], qg[r]], axis=0), s_old)
                v_new = u[r] - ws_qs[:CHUNK]
                pad = jnp.zeros_like(v_new)
                v_pad = jnp.concatenate([v_new, pad] if c == 0 else [pad, v_new], axis=0)
                o_parts.append(ws_qs[CHUNK:] + _mm(qk[r], v_pad))
                in_chunk = (col >= c * CHUNK) & (col < (c + 1) * CHUNK)
                sout_ref[si, h] = s_old * jnp.exp(gl_rows[c]) + _mm(jnp.where(in_chunk, kg_t, 0.0), v_pad)
            o = jnp.concatenate(o_parts, axis=0)
            zz = z_ref[0, rows, h * dv:(h + 1) * dv].astype(F32)
            y = (o * _rms_scale(o) * onorm_ref[...]) * _silu(zz)
            o_ref[0, rows, h * dv:(h + 1) * dv] = y.astype(o_ref.dtype)


def _delta(qkv, gcol, z, onorm, s0, *, n_heads):
    n_seq, L, qkv_w = qkv.shape
    v_w = z.shape[-1]
    assert qkv_w == 3 * n_heads * LANES and v_w == n_heads * LANES
    if L % PAIR == 0:
        chained, n_grp, Lg, per_grp = True, n_seq, L, 1
    else:
        assert L == CHUNK and n_seq % 2 == 0
        chained, n_grp, Lg, per_grp = False, n_seq // 2, PAIR, 2
        qkv, gcol, z = (t.reshape(n_grp, PAIR, t.shape[-1]) for t in (qkv, gcol, z))
    tb = min(256, Lg)
    assert Lg % tb == 0
    kern = functools.partial(_delta_kernel, tb=tb, n_heads=n_heads, chained=chained)
    blk = lambda w: pl.BlockSpec((1, tb, w), lambda s, j: (s, j, 0))
    st = pl.BlockSpec((per_grp, n_heads, LANES, LANES), lambda s, j: (s, 0, 0, 0))
    o, s_new = pl.pallas_call(
        kern,
        grid=(n_grp, Lg // tb),
        in_specs=[blk(qkv_w), blk(LANES), blk(v_w), _const_spec((1, LANES)), st],
        out_specs=[blk(v_w), st],
        out_shape=[jax.ShapeDtypeStruct((n_grp, Lg, v_w), BF16),
                   jax.ShapeDtypeStruct(s0.shape, F32)],
        compiler_params=pltpu.CompilerParams(dimension_semantics=("arbitrary", "arbitrary"),
                                             vmem_limit_bytes=VMEM_LIMIT),
        name="delta_rule",
    )(qkv, gcol, z, onorm, s0)
    return o.reshape(n_seq, L, v_w), s_new


def _out_mlp_kernel(*refs, n_a, tf):
    x_ref = refs[0]
    a_refs = refs[1:1 + n_a]
    w_ref, g_ref, wup_ref, wdn_ref, o_ref = refs[1 + n_a:]
    a = a_refs[0][...] if n_a == 1 else jnp.concatenate([a_ref[...] for a_ref in a_refs], axis=-1)
    x1 = x_ref[...] + jnp.dot(a, w_ref[...], preferred_element_type=F32)
    h = (x1 * _rms_scale(x1) * g_ref[...]).astype(BF16)
    acc = x1
    for f in range(wup_ref.shape[1] // tf):
        r = jnp.maximum(jnp.dot(h, wup_ref[:, f * tf:(f + 1) * tf], preferred_element_type=F32), 0.0)
        acc = acc + jnp.dot((r * r).astype(BF16), wdn_ref[f * tf:(f + 1) * tf, :], preferred_element_type=F32)
    o_ref[...] = acc


def _out_mlp(x, a_list, w_out, gain, w_up, w_down):
    shape = x.shape
    D = shape[-1]
    x2 = x.reshape(-1, D)
    n = x2.shape[0]
    a_list = [a.reshape(n, a.shape[-1]) for a in a_list]
    tm = min(512, n)
    assert n % tm == 0
    F = w_up.shape[1]
    tf = min(1024, F)
    assert F % tf == 0
    kern = functools.partial(_out_mlp_kernel, n_a=len(a_list), tf=tf)
    rows = lambda w: pl.BlockSpec((tm, w), lambda i: (i, 0))
    out = pl.pallas_call(
        kern,
        grid=(n // tm,),
        in_specs=([rows(D)] + [rows(a.shape[1]) for a in a_list] + [_const_spec(w_out.shape)]
                  + [_const_spec((1, D)), _const_spec(w_up.shape), _const_spec(w_down.shape)]),
        out_specs=rows(D),
        out_shape=jax.ShapeDtypeStruct((n, D), F32),
        compiler_params=pltpu.CompilerParams(dimension_semantics=("arbitrary",), vmem_limit_bytes=VMEM_LIMIT),
        name="out_mlp",
    )(x2, *a_list, w_out, gain, w_up, w_down)
    return out.reshape(shape)


def _odd_pre_kernel(x_ref, g_ref, w_ref, qn_ref, kn_ref, q_ref, k_ref, v_ref, *, n_heads):
    dh = LANES
    hd = n_heads * dh
    x = x_ref[0]
    h = (x * _rms_scale(x) * g_ref[...]).astype(BF16)
    q = jnp.dot(h, w_ref[:, 0:hd], preferred_element_type=F32)
    k = jnp.dot(h, w_ref[:, hd:2 * hd], preferred_element_type=F32)
    v = jnp.dot(h, w_ref[:, 2 * hd:3 * hd], preferred_element_type=F32)
    for hh in range(n_heads):
        cols = slice(hh * dh, (hh + 1) * dh)
        qh = q[:, cols]
        kh = k[:, cols]
        q_ref[0, hh] = (qh * _rms_scale(qh) * qn_ref[...]).astype(q_ref.dtype)
        k_ref[0, hh] = kh * _rms_scale(kh) * kn_ref[...]
        v_ref[0, hh] = v[:, cols]


def _odd_pre(x, gain, w_qkv, qn, kn, *, n_heads):
    B, L, D = x.shape
    tl = min(256, L)
    assert L % tl == 0 and w_qkv.shape[1] == 3 * n_heads * LANES
    kern = functools.partial(_odd_pre_kernel, n_heads=n_heads)
    head_major = pl.BlockSpec((1, n_heads, tl, LANES), lambda b, j: (b, 0, j, 0))
    hm_shape = (B, n_heads, L, LANES)
    return pl.pallas_call(
        kern,
        grid=(B, L // tl),
        in_specs=[pl.BlockSpec((1, tl, D), lambda b, j: (b, j, 0)), _const_spec((1, D)),
                  _const_spec(w_qkv.shape), _const_spec((1, LANES)), _const_spec((1, LANES))],
        out_specs=[head_major, head_major, head_major],
        out_shape=[jax.ShapeDtypeStruct(hm_shape, BF16), jax.ShapeDtypeStruct(hm_shape, F32),
                   jax.ShapeDtypeStruct(hm_shape, F32)],
        compiler_params=pltpu.CompilerParams(dimension_semantics=("arbitrary", "arbitrary"),
                                             vmem_limit_bytes=VMEM_LIMIT),
        name="odd_pre",
    )(x, gain, w_qkv, qn, kn)


def _suffix_ones(tk):
    j = lax.broadcasted_iota(jnp.int32, (tk, tk + LANES), 0)
    s = lax.broadcasted_iota(jnp.int32, (tk, tk + LANES), 1)
    return ((j > s) | (s >= tk)).astype(BF16)


def _sb_kernel(*refs, tq, tkp, n_past, scale):
    if n_past:
        q_ref, kn_ref, vn_ref, kp_ref, vp_ref, o_ref = refs
    else:
        q_ref, kn_ref, vn_ref, o_ref = refs
    i = pl.program_id(2)
    q = q_ref[0, 0]
    u_self = _suffix_ones(tq)
    row = lax.broadcasted_iota(jnp.int32, (tq, tq), 0)
    col = lax.broadcasted_iota(jnp.int32, (tq, tq), 1)
    below = col < row

    def block(k_blk, v_blk, u, carry, acc, mask):
        tk = k_blk.shape[0]
        z = _mm_nt(q, k_blk) * scale
        lr = -_softplus(z)
        if mask is not None:
            lr = jnp.where(mask, lr, 0.0)
        lr_hi = lr.astype(BF16)
        lr_lo = (lr - lr_hi.astype(F32)).astype(BF16)
        sums = jnp.dot(lr_hi, u, preferred_element_type=F32) + jnp.dot(lr_lo, u, preferred_element_type=F32)
        tail = sums[:, :tk] + carry
        wts = jnp.exp(z + lr + tail)
        if mask is not None:
            wts = jnp.where(mask, wts, 0.0)
        return carry + sums[:, tk:tk + 1], acc + _mm(wts, v_blk)

    def live(carry):
        return jnp.max(carry) > LOG_W_FLOOR

    start = pl.multiple_of(i * tq, tq)
    carry = jnp.zeros((tq, 1), F32)
    acc = jnp.zeros((tq, LANES), F32)
    carry, acc = block(kn_ref[0, 0, pl.ds(start, tq), :], vn_ref[0, 0, pl.ds(start, tq), :], u_self,
                       carry, acc, below)

    def walk(k_ref, v_ref, tk, u, first, carry, acc):
        def cond(st):
            return (st[0] >= 0) & live(st[1])

        def body(st):
            j, carry, acc = st
            s0 = pl.multiple_of(j * tk, tk)
            carry, acc = block(k_ref[0, 0, pl.ds(s0, tk), :], v_ref[0, 0, pl.ds(s0, tk), :], u, carry, acc, None)
            return j - 1, carry, acc

        _, carry, acc = lax.while_loop(cond, body, (first, carry, acc))
        return carry, acc

    carry, acc = walk(kn_ref, vn_ref, tq, u_self, i - 1, carry, acc)
    if n_past:
        carry, acc = walk(kp_ref, vp_ref, tkp, _suffix_ones(tkp), jnp.int32(n_past - 1), carry, acc)
    o_ref[0] = acc.astype(o_ref.dtype)


def _stick_breaking(q, k_new, v_new, k_past, v_past):
    B, H, L, dh = q.shape
    assert dh == LANES
    tq = min(256, L)
    assert L % tq == 0
    args = [q, k_new, v_new]
    full = lambda n: pl.BlockSpec((1, 1, n, LANES), lambda b, h, i: (b, h, 0, 0))
    in_specs = [pl.BlockSpec((1, 1, tq, LANES), lambda b, h, i: (b, h, i, 0)), full(L), full(L)]
    n_past, tkp = 0, 0
    if k_past is not None:
        P = k_past.shape[2]
        tkp = min(256, P)
        assert P % tkp == 0
        n_past = P // tkp
        args += [k_past, v_past]
        in_specs += [full(P), full(P)]
    kern = functools.partial(_sb_kernel, tq=tq, tkp=tkp, n_past=n_past, scale=dh ** -0.5)
    return pl.pallas_call(
        kern,
        grid=(B, H, L // tq),
        in_specs=in_specs,
        out_specs=pl.BlockSpec((1, tq, LANES), lambda b, h, i: (b, i, h)),
        out_shape=jax.ShapeDtypeStruct((B, L, H * LANES), BF16),
        compiler_params=pltpu.CompilerParams(dimension_semantics=("arbitrary", "arbitrary", "arbitrary"),
                                             vmem_limit_bytes=VMEM_LIMIT),
        name="stick_breaking",
    )(*args)


def _pad_lanes(v):
    return jnp.zeros((1, LANES), F32).at[0, :v.shape[0]].set(v.astype(F32))


def _even_layer(x, s0, qbuf, dbuf, wts, gain_mix, gain_mlp, w_up, w_down):
    (w_all, conv_w, alog_pad, dtb_pad, onorm, dw_w, dw_b, ln_g, ln_b, w_out, n_heads, qkv_w, v_w, c_b) = wts
    qkv, gcol, z, c, qlast, dlast = _even_pre(x, gain_mix, w_all, conv_w, alog_pad, dtb_pad, dw_w, dw_b, ln_g, ln_b,
                                              qbuf, dbuf, n_heads=n_heads, qkv_w=qkv_w, v_w=v_w, c_b=c_b)
    o, s_new = _delta(qkv, gcol, z, onorm, s0, n_heads=n_heads)
    y = _out_mlp(x, [o, c], w_out, gain_mlp, w_up, w_down)
    return y, s_new, qlast, dlast


def _odd_layer(x, k_past, v_past, wts, gain_mix, gain_mlp, w_up, w_down):
    w_qkv, qn, kn, w_out, n_heads = wts
    q, k, v = _odd_pre(x, gain_mix, w_qkv, qn, kn, n_heads=n_heads)
    o = _stick_breaking(q, k, v, k_past, v_past)
    y = _out_mlp(x, [o], w_out, gain_mlp, w_up, w_down)
    return y, k, v


def kernel(x_prompt, x_sample, state_delta, state_qkv_conv, state_dw_conv, cache_k, cache_v, norm_mix, norm_mlp, w_in_e, conv_qkv_e, a_log_e, dt_bias_e, onorm_e, dw_w_e, dw_b_e, ln_g_e, ln_b_e, w_out_e, w_qkv_o, qn_o, kn_o, w_out_o, w_up, w_down):
    depth = norm_mix.shape[0]
    bp = x_prompt.shape[0]
    xp, xs = x_prompt, x_sample
    pd, pq, pw, pk, pv = [], [], [], [], []
    sd, sq, sw, sk, sv = [], [], [], [], []
    row = lambda v: v.astype(F32).reshape(1, -1)
    for i in range(depth):
        gm, gl = row(norm_mix[i]), row(norm_mlp[i])
        wu, wd = w_up[i].astype(BF16), w_down[i].astype(BF16)
        if i % 2 == 0:
            e = i // 2
            n_heads = a_log_e.shape[1]
            qkv_w = conv_qkv_e.shape[2]
            c_b = dw_w_e.shape[2]
            v_w = n_heads * onorm_e.shape[1]
            w_in = w_in_e[e]
            ab0 = qkv_w + v_w
            ab = jnp.zeros((w_in.shape[0], LANES), w_in.dtype).at[:, :2 * n_heads].set(w_in[:, ab0:ab0 + 2 * n_heads])
            w_all = jnp.concatenate([w_in[:, :ab0], w_in[:, ab0 + 2 * n_heads:], ab], axis=1).astype(BF16)
            w_out = w_out_e[e].astype(BF16)
            wts = (w_all, conv_qkv_e[e].astype(F32), _pad_lanes(a_log_e[e]), _pad_lanes(dt_bias_e[e]),
                   row(onorm_e[e]), dw_w_e[e].astype(F32), row(dw_b_e[e]), row(ln_g_e[e]), row(ln_b_e[e]),
                   w_out, n_heads, qkv_w, v_w, c_b)
            kq, kd = conv_qkv_e.shape[1], dw_w_e.shape[1]
            s0 = jnp.zeros((bp,) + state_delta.shape[2:], F32)
            qb0 = jnp.zeros((bp, kq - 1, qkv_w), F32)
            db0 = jnp.zeros((bp, kd - 1, c_b), F32)
            xp, d1, q1, c1 = _even_layer(xp, s0, qb0, db0, wts, gm, gl, wu, wd)
            xs, d2, q2, c2 = _even_layer(xs, state_delta[e].astype(F32), state_qkv_conv[e].astype(F32),
                                         state_dw_conv[e].astype(F32), wts, gm, gl, wu, wd)
            pd.append(d1); pq.append(q1); pw.append(c1)
            sd.append(d2); sq.append(q2); sw.append(c2)
        else:
            o = i // 2
            n_heads = cache_k.shape[2]
            wts = (w_qkv_o[o].astype(BF16), row(qn_o[o]), row(kn_o[o]), w_out_o[o].astype(BF16), n_heads)
            xp, k1, v1 = _odd_layer(xp, None, None, wts, gm, gl, wu, wd)
            xs, k2, v2 = _odd_layer(xs, cache_k[o].astype(F32), cache_v[o].astype(F32), wts, gm, gl, wu, wd)
            pk.append(k1); pv.append(v1)
            sk.append(k2); sv.append(v2)
    return (xp, xs,
            jnp.stack(pd), jnp.stack(pq), jnp.stack(pw), jnp.stack(pk), jnp.stack(pv),
            jnp.stack(sd), jnp.stack(sq), jnp.stack(sw), jnp.stack(sk), jnp.stack(sv))
```

```python
import functools

import jax
import jax.numpy as jnp
from jax import lax
from jax.experimental import pallas as pl
from jax.experimental.pallas import tpu as pltpu

F32 = jnp.float32
BF16 = jnp.bfloat16
EPS = 1e-6
CHUNK = 64
LANES = 128
PAIR = 2 * CHUNK
QBUF_OFF = 8
DBUF_OFF = 32
DW_ROWS = 32
VMEM_LIMIT = 56 * 1024 * 1024
LOG2E = 1.4426950408889634
LOG2_W_FLOOR = -151.0
MASKED = 1e30


def _mm(a, b):
    return jnp.dot(a.astype(BF16), b.astype(BF16), preferred_element_type=F32)


def _mm_nt(a, b):
    return lax.dot_general(a.astype(BF16), b.astype(BF16), (((1,), (1,)), ((), ())),
                           preferred_element_type=F32)


def _sigmoid(x):
    return 1.0 / (1.0 + jnp.exp(-x))


def _silu(x):
    return x * _sigmoid(x)


def _softplus(x):
    return jnp.maximum(x, 0.0) + jnp.log1p(jnp.exp(-jnp.abs(x)))


def _rms_scale(x):
    return lax.rsqrt(jnp.mean(x * x, axis=-1, keepdims=True) + EPS)


def _const_spec(shape):
    nd = len(shape)
    return pl.BlockSpec(shape, lambda *_: (0,) * nd, pipeline_mode=pl.Buffered(1))


def _even_pre_kernel(x_ref, g_ref, w_ref, cw_ref, alog_ref, dtb_ref, dww_ref, dwb_ref, lng_ref, lnb_ref,
                     qbuf_ref, dbuf_ref,
                     qkv_ref, gcol_ref, z_ref, c_ref, qlast_ref, dlast_ref,
                     qext, uext, *, tl, qkv_w, v_w, c_b, n_heads):
    kq = cw_ref.shape[0]
    kd = dww_ref.shape[0]

    @pl.when(pl.program_id(1) == 0)
    def _():
        qext[QBUF_OFF - (kq - 1):QBUF_OFF, :] = qbuf_ref[0]
        uext[DBUF_OFF - (kd - 1):DBUF_OFF, :] = dbuf_ref[0]

    x = x_ref[0]
    h = (x * _rms_scale(x) * g_ref[...]).astype(BF16)

    glu0 = qkv_w + v_w
    qext[QBUF_OFF:QBUF_OFF + tl, :] = jnp.dot(h, w_ref[:, 0:qkv_w], preferred_element_type=F32)
    z_ref[0] = jnp.dot(h, w_ref[:, qkv_w:glu0], preferred_element_type=F32)
    ga = jnp.dot(h, w_ref[:, glu0:glu0 + c_b], preferred_element_type=F32)
    gb = jnp.dot(h, w_ref[:, glu0 + c_b:glu0 + 2 * c_b], preferred_element_type=F32)
    uext[DBUF_OFF:DBUF_OFF + tl, :] = ga * _sigmoid(gb)
    ab = jnp.dot(h, w_ref[:, glu0 + 2 * c_b:glu0 + 2 * c_b + LANES], preferred_element_type=F32)

    for s in range(qkv_w // LANES):
        cols = slice(s * LANES, (s + 1) * LANES)
        y = None
        for j in range(kq):
            r0 = QBUF_OFF - (kq - 1) + j
            t = cw_ref[j:j + 1, cols] * qext[r0:r0 + tl, cols]
            y = t if y is None else y + t
        y = _silu(y)
        if s < 2 * n_heads:
            y = y * lax.rsqrt(jnp.sum(y * y, axis=-1, keepdims=True) + EPS)
        qkv_ref[0, :, cols] = y
    qlast = qext[QBUF_OFF + tl - (kq - 1):QBUF_OFF + tl, :]
    qlast_ref[0] = qlast
    qext[QBUF_OFF - (kq - 1):QBUF_OFF, :] = qlast

    for r in range(tl // DW_ROWS):
        acc = None
        for j in range(kd):
            r0 = DBUF_OFF - (kd - 1) + j + r * DW_ROWS
            t = dww_ref[j:j + 1, :] * uext[r0:r0 + DW_ROWS, :]
            acc = t if acc is None else acc + t
        cpre = acc + dwb_ref[...]
        mu = jnp.mean(cpre, axis=-1, keepdims=True)
        xc = cpre - mu
        var = jnp.mean(xc * xc, axis=-1, keepdims=True)
        y = xc * lax.rsqrt(var + EPS) * lng_ref[...] + lnb_ref[...]
        c_ref[0, r * DW_ROWS:(r + 1) * DW_ROWS, :] = _silu(y).astype(c_ref.dtype)
    dlast = uext[DBUF_OFF + tl - (kd - 1):DBUF_OFF + tl, :]
    dlast_ref[0] = dlast
    uext[DBUF_OFF - (kd - 1):DBUF_OFF, :] = dlast

    lane = lax.broadcasted_iota(jnp.int32, (tl, LANES), 1)
    g = -jnp.exp(alog_ref[...]) * _softplus(ab + dtb_ref[...])
    g = jnp.where(lane < n_heads, g, 0.0)
    beta = _sigmoid(ab)
    row = lax.broadcasted_iota(jnp.int32, (tl, tl), 0)
    col = lax.broadcasted_iota(jnp.int32, (tl, tl), 1)
    tri = (((row ^ col) < CHUNK) & (col <= row)).astype(BF16)
    g_hi = g.astype(BF16)
    g_r1 = g - g_hi.astype(F32)
    g_mid = g_r1.astype(BF16)
    g_lo = (g_r1 - g_mid.astype(F32)).astype(BF16)
    gcum = (jnp.dot(tri, g_hi, preferred_element_type=F32)
            + jnp.dot(tri, g_mid, preferred_element_type=F32)
            + jnp.dot(tri, g_lo, preferred_element_type=F32))
    gcol_ref[0] = jnp.where(lane < n_heads, gcum, jnp.where(lane < 2 * n_heads, beta, 0.0))


def _even_pre(x, gain, w_all, conv_w, alog_pad, dtb_pad, dw_w, dw_b, ln_g, ln_b, qbuf, dbuf, *, n_heads, qkv_w, v_w, c_b):
    n_seq, L, D = x.shape
    tl = min(256, L)
    assert L % tl == 0 and tl % DW_ROWS == 0 and tl % CHUNK == 0
    kq, kd = conv_w.shape[0], dw_w.shape[0]
    assert kq - 1 <= QBUF_OFF and kd - 1 <= DBUF_OFF and tl >= kd - 1
    nw = w_all.shape[1]
    kern = functools.partial(_even_pre_kernel, tl=tl, qkv_w=qkv_w, v_w=v_w, c_b=c_b, n_heads=n_heads)
    seq_blk = lambda w: pl.BlockSpec((1, tl, w), lambda s, j: (s, j, 0))
    per_seq = lambda r, w: pl.BlockSpec((1, r, w), lambda s, j: (s, 0, 0))
    return pl.pallas_call(
        kern,
        grid=(n_seq, L // tl),
        in_specs=[seq_blk(D), _const_spec((1, D)), _const_spec((D, nw)), _const_spec((kq, qkv_w)),
                  _const_spec((1, LANES)), _const_spec((1, LANES)), _const_spec((kd, c_b)),
                  _const_spec((1, c_b)), _const_spec((1, c_b)), _const_spec((1, c_b)),
                  per_seq(kq - 1, qkv_w), per_seq(kd - 1, c_b)],
        out_specs=[seq_blk(qkv_w), seq_blk(LANES), seq_blk(v_w), seq_blk(c_b),
                   per_seq(kq - 1, qkv_w), per_seq(kd - 1, c_b)],
        out_shape=[jax.ShapeDtypeStruct((n_seq, L, qkv_w), F32),
                   jax.ShapeDtypeStruct((n_seq, L, LANES), F32),
                   jax.ShapeDtypeStruct((n_seq, L, v_w), F32),
                   jax.ShapeDtypeStruct((n_seq, L, c_b), BF16),
                   jax.ShapeDtypeStruct((n_seq, kq - 1, qkv_w), F32),
                   jax.ShapeDtypeStruct((n_seq, kd - 1, c_b), F32)],
        scratch_shapes=[pltpu.VMEM((QBUF_OFF + tl, qkv_w), F32), pltpu.VMEM((DBUF_OFF + tl, c_b), F32)],
        compiler_params=pltpu.CompilerParams(dimension_semantics=("arbitrary", "arbitrary"),
                                             vmem_limit_bytes=VMEM_LIMIT),
        name="even_pre",
    )(x, gain, w_all, conv_w, alog_pad, dtb_pad, dw_w, dw_b, ln_g, ln_b, qbuf, dbuf)


def _unit_lower_inverses(a_list, row, col):
    x = row ^ col
    eye = (row == col).astype(F32)
    a8 = [jnp.where(x < 8, a, 0.0) for a in a_list]
    a8_2 = [_mm(t, t) for t in a8]
    p = [eye - t for t in a8]
    a8_4 = [_mm(t, t) for t in a8_2]
    p = [t + _mm(t, sq) for t, sq in zip(p, a8_2)]
    d = [t + _mm(t, sq) for t, sq in zip(p, a8_4)]
    s = 8
    while s < CHUNK:
        off = [jnp.where((x >= s) & (x < 2 * s), a, 0.0) for a in a_list]
        od = [_mm(o, t) for o, t in zip(off, d)]
        d = [t - _mm(t, u) for t, u in zip(d, od)]
        s *= 2
    return d


def _delta_kernel(qkv_ref, gcol_ref, z_ref, onorm_ref, sin_ref, o_ref, sout_ref, *, tb, n_heads, chained):
    dk = dv = LANES
    qk_w = n_heads * dk
    n_pairs = tb // PAIR
    if chained:
        @pl.when(pl.program_id(1) == 0)
        def _():
            sout_ref[...] = sin_ref[...]

    row = lax.broadcasted_iota(jnp.int32, (PAIR, PAIR), 0)
    col = lax.broadcasted_iota(jnp.int32, (PAIR, PAIR), 1)
    same = (row ^ col) < CHUNK
    causal = same & (col <= row)
    strict = same & (col < row)
    in_chunk = [(col >= c * CHUNK) & (col < (c + 1) * CHUNK) for c in range(2)]

    tiles = [(p, h) for p in range(n_pairs) for h in range(n_heads)]
    q, k, kb, vb, gc, a, qk = {}, {}, {}, {}, {}, {}, {}
    for p, h in tiles:
        rows = slice(p * PAIR, (p + 1) * PAIR)
        q[p, h] = qkv_ref[0, rows, h * dk:(h + 1) * dk] * (dk ** -0.5)
        k[p, h] = qkv_ref[0, rows, qk_w + h * dk:qk_w + (h + 1) * dk]
        gc[p, h] = jnp.broadcast_to(gcol_ref[0, rows, h:h + 1], (PAIR, PAIR))
        bt = jnp.broadcast_to(gcol_ref[0, rows, n_heads + h:n_heads + h + 1], (PAIR, PAIR))
        kb[p, h] = k[p, h] * bt
        vb[p, h] = qkv_ref[0, rows, 2 * qk_w + h * dv:2 * qk_w + (h + 1) * dv] * bt
    for t in tiles:
        diff = gc[t] - gc[t].T
        decay = jnp.where(causal, jnp.exp(jnp.where(causal, diff, 0.0)), 0.0)
        a[t] = jnp.where(strict, _mm_nt(kb[t], k[t]) * decay, 0.0)
        qk[t] = jnp.where(causal, _mm_nt(q[t], k[t]) * decay, 0.0)
    inv = dict(zip(tiles, _unit_lower_inverses([a[t] for t in tiles], row, col)))
    u, w, qg, kg_t, gl = {}, {}, {}, {}, {}
    for t in tiles:
        eg = jnp.exp(gc[t])
        u[t] = _mm(inv[t], vb[t])
        w[t] = _mm(inv[t], kb[t] * eg)
        qg[t] = q[t] * eg
        gl_rows = [jnp.broadcast_to(gc[t][(c + 1) * CHUNK - 1:(c + 1) * CHUNK, :], (PAIR, PAIR)) for c in range(2)]
        gl[t] = [jnp.exp(g) for g in gl_rows]
        gcl = jnp.where(row < CHUNK, gl_rows[0], gl_rows[1])
        kg_t[t] = (k[t] * jnp.exp(gcl - gc[t])).T

    state = [sout_ref[0, h] for h in range(n_heads)] if chained else None
    for p in range(n_pairs):
        rows = slice(p * PAIR, (p + 1) * PAIR)
        o_parts = {h: [] for h in range(n_heads)}
        for c in range(2):
            r = slice(c * CHUNK, (c + 1) * CHUNK)
            for h in range(n_heads):
                t = (p, h)
                s_old = state[h] if chained else sin_ref[c, h]
                ws_qs = _mm(jnp.concatenate([w[t][r], qg[t][r]], axis=0), s_old)
                v_new = u[t][r] - ws_qs[:CHUNK]
                pad = jnp.zeros_like(v_new)
                v_pad = jnp.concatenate([v_new, pad] if c == 0 else [pad, v_new], axis=0)
                o_parts[h].append(ws_qs[CHUNK:] + _mm(qk[t][r], v_pad))
                s_new = s_old * gl[t][c] + _mm(jnp.where(in_chunk[c], kg_t[t], 0.0), v_pad)
                if chained:
                    state[h] = s_new
                else:
                    sout_ref[c, h] = s_new
        for h in range(n_heads):
            o = jnp.concatenate(o_parts[h], axis=0)
            zz = z_ref[0, rows, h * dv:(h + 1) * dv].astype(F32)
            y = (o * _rms_scale(o) * onorm_ref[...]) * _silu(zz)
            o_ref[0, rows, h * dv:(h + 1) * dv] = y.astype(o_ref.dtype)
    if chained:
        for h in range(n_heads):
            sout_ref[0, h] = state[h]


def _delta(qkv, gcol, z, onorm, s0, *, n_heads):
    n_seq, L, qkv_w = qkv.shape
    v_w = z.shape[-1]
    assert qkv_w == 3 * n_heads * LANES and v_w == n_heads * LANES
    if L % PAIR == 0:
        chained, n_grp, Lg, per_grp = True, n_seq, L, 1
    else:
        assert L == CHUNK and n_seq % 2 == 0
        chained, n_grp, Lg, per_grp = False, n_seq // 2, PAIR, 2
        qkv, gcol, z = (t.reshape(n_grp, PAIR, t.shape[-1]) for t in (qkv, gcol, z))
    tb = min(256, Lg)
    assert Lg % tb == 0
    kern = functools.partial(_delta_kernel, tb=tb, n_heads=n_heads, chained=chained)
    blk = lambda w: pl.BlockSpec((1, tb, w), lambda s, j: (s, j, 0))
    st = pl.BlockSpec((per_grp, n_heads, LANES, LANES), lambda s, j: (s, 0, 0, 0))
    o, s_new = pl.pallas_call(
        kern,
        grid=(n_grp, Lg // tb),
        in_specs=[blk(qkv_w), blk(LANES), blk(v_w), _const_spec((1, LANES)), st],
        out_specs=[blk(v_w), st],
        out_shape=[jax.ShapeDtypeStruct((n_grp, Lg, v_w), BF16),
                   jax.ShapeDtypeStruct(s0.shape, F32)],
        compiler_params=pltpu.CompilerParams(dimension_semantics=("arbitrary", "arbitrary"),
                                             vmem_limit_bytes=VMEM_LIMIT),
        name="delta_rule",
    )(qkv, gcol, z, onorm, s0)
    return o.reshape(n_seq, L, v_w), s_new


def _out_mlp_kernel(*refs, n_a, tf):
    x_ref = refs[0]
    a_refs = refs[1:1 + n_a]
    w_ref, g_ref, wup_ref, wdn_ref, o_ref = refs[1 + n_a:]
    a = a_refs[0][...] if n_a == 1 else jnp.concatenate([a_ref[...] for a_ref in a_refs], axis=-1)
    x1 = x_ref[...] + jnp.dot(a, w_ref[...], preferred_element_type=F32)
    h = (x1 * _rms_scale(x1) * g_ref[...]).astype(BF16)
    acc = x1
    for f in range(wup_ref.shape[1] // tf):
        r = jnp.maximum(jnp.dot(h, wup_ref[:, f * tf:(f + 1) * tf], preferred_element_type=F32), 0.0)
        acc = acc + jnp.dot((r * r).astype(BF16), wdn_ref[f * tf:(f + 1) * tf, :], preferred_element_type=F32)
    o_ref[...] = acc


def _out_mlp(x, a_list, w_out, gain, w_up, w_down):
    shape = x.shape
    D = shape[-1]
    x2 = x.reshape(-1, D)
    n = x2.shape[0]
    a_list = [a.reshape(n, a.shape[-1]) for a in a_list]
    tm = min(512, n)
    assert n % tm == 0
    F = w_up.shape[1]
    tf = min(1024, F)
    assert F % tf == 0
    kern = functools.partial(_out_mlp_kernel, n_a=len(a_list), tf=tf)
    rows = lambda w: pl.BlockSpec((tm, w), lambda i: (i, 0))
    out = pl.pallas_call(
        kern,
        grid=(n // tm,),
        in_specs=([rows(D)] + [rows(a.shape[1]) for a in a_list] + [_const_spec(w_out.shape)]
                  + [_const_spec((1, D)), _const_spec(w_up.shape), _const_spec(w_down.shape)]),
        out_specs=rows(D),
        out_shape=jax.ShapeDtypeStruct((n, D), F32),
        compiler_params=pltpu.CompilerParams(dimension_semantics=("arbitrary",), vmem_limit_bytes=VMEM_LIMIT),
        name="out_mlp",
    )(x2, *a_list, w_out, gain, w_up, w_down)
    return out.reshape(shape)


def _odd_pre_kernel(x_ref, g_ref, w_ref, qn_ref, kn_ref, q_ref, k_ref, v_ref, *, n_heads):
    dh = LANES
    hd = n_heads * dh
    x = x_ref[0]
    h = (x * _rms_scale(x) * g_ref[...]).astype(BF16)
    q = jnp.dot(h, w_ref[:, 0:hd], preferred_element_type=F32)
    k = jnp.dot(h, w_ref[:, hd:2 * hd], preferred_element_type=F32)
    v = jnp.dot(h, w_ref[:, 2 * hd:3 * hd], preferred_element_type=F32)
    for hh in range(n_heads):
        cols = slice(hh * dh, (hh + 1) * dh)
        qh = q[:, cols]
        kh = k[:, cols]
        q_ref[0, hh] = (qh * _rms_scale(qh) * qn_ref[...] * (dh ** -0.5 * LOG2E)).astype(q_ref.dtype)
        k_ref[0, hh] = kh * _rms_scale(kh) * kn_ref[...]
        v_ref[0, hh] = v[:, cols]


def _odd_pre(x, gain, w_qkv, qn, kn, *, n_heads):
    B, L, D = x.shape
    tl = min(256, L)
    assert L % tl == 0 and w_qkv.shape[1] == 3 * n_heads * LANES
    kern = functools.partial(_odd_pre_kernel, n_heads=n_heads)
    head_major = pl.BlockSpec((1, n_heads, tl, LANES), lambda b, j: (b, 0, j, 0))
    hm_shape = (B, n_heads, L, LANES)
    return pl.pallas_call(
        kern,
        grid=(B, L // tl),
        in_specs=[pl.BlockSpec((1, tl, D), lambda b, j: (b, j, 0)), _const_spec((1, D)),
                  _const_spec(w_qkv.shape), _const_spec((1, LANES)), _const_spec((1, LANES))],
        out_specs=[head_major, head_major, head_major],
        out_shape=[jax.ShapeDtypeStruct(hm_shape, BF16), jax.ShapeDtypeStruct(hm_shape, F32),
                   jax.ShapeDtypeStruct(hm_shape, F32)],
        compiler_params=pltpu.CompilerParams(dimension_semantics=("arbitrary", "arbitrary"),
                                             vmem_limit_bytes=VMEM_LIMIT),
        name="odd_pre",
    )(x, gain, w_qkv, qn, kn)


def _suffix_ones(tk):
    j = lax.broadcasted_iota(jnp.int32, (2 * tk, tk), 0) & (tk - 1)
    s = lax.broadcasted_iota(jnp.int32, (2 * tk, tk), 1)
    return (j > s).astype(BF16)


def _sb_block(q, k_blk, v_blk, u, off, acc, below=None):
    z = _mm_nt(q, k_blk)
    t = jnp.maximum(z, 0.0) + jnp.log(1.0 + jnp.exp2(-jnp.abs(z))) * LOG2E
    if below is not None:
        t = jnp.where(below, t, 0.0)
    t_hi = t.astype(BF16)
    t_lo = (t - t_hi.astype(F32)).astype(BF16)
    later = jnp.dot(jnp.concatenate([t_hi, t_lo], axis=1), u, preferred_element_type=F32)
    wts = jnp.exp2(z - t - (later + off))
    return later[:, 0:1] + t[:, 0:1], acc + _mm(wts, v_blk)


def _sb_kernel(*refs, tq, tkp, n_past, n_q, hb):
    if n_past:
        q_ref, kn_ref, vn_ref, kp_ref, vp_ref, o_ref = refs
    else:
        q_ref, kn_ref, vn_ref, o_ref = refs
    i = pl.program_id(2)
    u_self = _suffix_ones(tq)
    u_past = _suffix_ones(tkp) if n_past else None
    row = lax.broadcasted_iota(jnp.int32, (tq, tq), 0)
    col = lax.broadcasted_iota(jnp.int32, (tq, tq), 1)
    below = col < row
    not_below = jnp.where(below, 0.0, MASKED)
    start = pl.multiple_of(i * tq, tq)

    walks = []
    for hh in range(hb):
        q = q_ref[0, hh]
        acc = jnp.zeros((tq, LANES), F32)
        spent, acc = _sb_block(q, kn_ref[0, hh, pl.ds(start, tq), :], vn_ref[0, hh, pl.ds(start, tq), :],
                               u_self, not_below, acc, below)
        first_self, first_past = i - 1, n_past - 1
        if n_q > 1:
            prev = pl.multiple_of(jnp.maximum(i - 1, 0) * tq, tq)
            has_prev = jnp.full((tq, 1), i, jnp.int32) >= 1
            own, acc = _sb_block(q, kn_ref[0, hh, pl.ds(prev, tq), :], vn_ref[0, hh, pl.ds(prev, tq), :],
                                 u_self, jnp.where(has_prev, spent, MASKED), acc)
            spent = spent + jnp.where(has_prev, own, 0.0)
            first_self = i - 2
        elif n_past:
            lastp = (n_past - 1) * tkp
            own, acc = _sb_block(q, kp_ref[0, hh, lastp:lastp + tkp, :], vp_ref[0, hh, lastp:lastp + tkp, :],
                                 u_past, spent, acc)
            spent = spent + own
            first_past = n_past - 2
        walks.append((q, spent, acc, first_self, first_past))

    def walk(q, k_ref, v_ref, hh, tk, u, first, spent, acc):
        def cond(st):
            return (st[0] >= 0) & (jnp.min(st[1]) < -LOG2_W_FLOOR)

        def body(st):
            j, spent, acc = st
            s0 = pl.multiple_of(j * tk, tk)
            own, acc = _sb_block(q, k_ref[0, hh, pl.ds(s0, tk), :], v_ref[0, hh, pl.ds(s0, tk), :], u, spent, acc)
            return j - 1, spent + own, acc

        _, spent, acc = lax.while_loop(cond, body, (first, spent, acc))
        return spent, acc

    for hh, (q, spent, acc, first_self, first_past) in enumerate(walks):
        if n_q > 1:
            spent, acc = walk(q, kn_ref, vn_ref, hh, tq, u_self, first_self, spent, acc)
        if n_past and (n_q > 1 or n_past > 1):
            spent, acc = walk(q, kp_ref, vp_ref, hh, tkp, u_past, jnp.int32(first_past), spent, acc)
        o_ref[0, :, hh * LANES:(hh + 1) * LANES] = acc.astype(o_ref.dtype)


def _stick_breaking(q, k_new, v_new, k_past, v_past):
    B, H, L, dh = q.shape
    assert dh == LANES
    tq = min(256, L)
    assert L % tq == 0
    n_q = L // tq
    assert tq & (tq - 1) == 0
    hb = H if n_q == 1 else 2
    assert H % hb == 0
    args = [q, k_new, v_new]
    full = lambda n: pl.BlockSpec((1, hb, n, LANES), lambda b, h, i: (b, h, 0, 0))
    in_specs = [pl.BlockSpec((1, hb, tq, LANES), lambda b, h, i: (b, h, i, 0)), full(L), full(L)]
    n_past, tkp = 0, 0
    if k_past is not None:
        P = k_past.shape[2]
        tkp = min(256, P)
        assert P % tkp == 0 and tkp & (tkp - 1) == 0
        n_past = P // tkp
        args += [k_past, v_past]
        in_specs += [full(P), full(P)]
    kern = functools.partial(_sb_kernel, tq=tq, tkp=tkp, n_past=n_past, n_q=n_q, hb=hb)
    return pl.pallas_call(
        kern,
        grid=(B, H // hb, n_q),
        in_specs=in_specs,
        out_specs=pl.BlockSpec((1, tq, hb * LANES), lambda b, h, i: (b, i, h)),
        out_shape=jax.ShapeDtypeStruct((B, L, H * LANES), BF16),
        compiler_params=pltpu.CompilerParams(dimension_semantics=("arbitrary", "arbitrary", "arbitrary"),
                                             vmem_limit_bytes=VMEM_LIMIT),
        name="stick_breaking",
    )(*args)


def _pad_lanes(v):
    return jnp.zeros((1, LANES), F32).at[0, :v.shape[0]].set(v.astype(F32))


def _even_layer(x, s0, qbuf, dbuf, wts, gain_mix, gain_mlp, w_up, w_down):
    (w_all, conv_w, alog_pad, dtb_pad, onorm, dw_w, dw_b, ln_g, ln_b, w_out, n_heads, qkv_w, v_w, c_b) = wts
    qkv, gcol, z, c, qlast, dlast = _even_pre(x, gain_mix, w_all, conv_w, alog_pad, dtb_pad, dw_w, dw_b, ln_g, ln_b,
                                              qbuf, dbuf, n_heads=n_heads, qkv_w=qkv_w, v_w=v_w, c_b=c_b)
    o, s_new = _delta(qkv, gcol, z, onorm, s0, n_heads=n_heads)
    y = _out_mlp(x, [o, c], w_out, gain_mlp, w_up, w_down)
    return y, s_new, qlast, dlast


def _odd_layer(x, k_past, v_past, wts, gain_mix, gain_mlp, w_up, w_down):
    w_qkv, qn, kn, w_out, n_heads = wts
    q, k, v = _odd_pre(x, gain_mix, w_qkv, qn, kn, n_heads=n_heads)
    o = _stick_breaking(q, k, v, k_past, v_past)
    y = _out_mlp(x, [o], w_out, gain_mlp, w_up, w_down)
    return y, k, v


def kernel(x_prompt, x_sample, state_delta, state_qkv_conv, state_dw_conv, cache_k, cache_v, norm_mix, norm_mlp, w_in_e, conv_qkv_e, a_log_e, dt_bias_e, onorm_e, dw_w_e, dw_b_e, ln_g_e, ln_b_e, w_out_e, w_qkv_o, qn_o, kn_o, w_out_o, w_up, w_down):
    depth = norm_mix.shape[0]
    bp = x_prompt.shape[0]
    xp, xs = x_prompt, x_sample
    pd, pq, pw, pk, pv = [], [], [], [], []
    sd, sq, sw, sk, sv = [], [], [], [], []
    row = lambda v: v.astype(F32).reshape(1, -1)
    for i in range(depth):
        gm, gl = row(norm_mix[i]), row(norm_mlp[i])
        wu, wd = w_up[i].astype(BF16), w_down[i].astype(BF16)
        if i % 2 == 0:
            e = i // 2
            n_heads = a_log_e.shape[1]
            qkv_w = conv_qkv_e.shape[2]
            c_b = dw_w_e.shape[2]
            v_w = n_heads * onorm_e.shape[1]
            w_in = w_in_e[e]
            ab0 = qkv_w + v_w
            ab = jnp.zeros((w_in.shape[0], LANES), w_in.dtype).at[:, :2 * n_heads].set(w_in[:, ab0:ab0 + 2 * n_heads])
            w_all = jnp.concatenate([w_in[:, :ab0], w_in[:, ab0 + 2 * n_heads:], ab], axis=1).astype(BF16)
            w_out = w_out_e[e].astype(BF16)
            wts = (w_all, conv_qkv_e[e].astype(F32), _pad_lanes(a_log_e[e]), _pad_lanes(dt_bias_e[e]),
                   row(onorm_e[e]), dw_w_e[e].astype(F32), row(dw_b_e[e]), row(ln_g_e[e]), row(ln_b_e[e]),
                   w_out, n_heads, qkv_w, v_w, c_b)
            kq, kd = conv_qkv_e.shape[1], dw_w_e.shape[1]
            s0 = jnp.zeros((bp,) + state_delta.shape[2:], F32)
            qb0 = jnp.zeros((bp, kq - 1, qkv_w), F32)
            db0 = jnp.zeros((bp, kd - 1, c_b), F32)
            xp, d1, q1, c1 = _even_layer(xp, s0, qb0, db0, wts, gm, gl, wu, wd)
            xs, d2, q2, c2 = _even_layer(xs, state_delta[e].astype(F32), state_qkv_conv[e].astype(F32),
                                         state_dw_conv[e].astype(F32), wts, gm, gl, wu, wd)
            pd.append(d1); pq.append(q1); pw.append(c1)
            sd.append(d2); sq.append(q2); sw.append(c2)
        else:
            o = i // 2
            n_heads = cache_k.shape[2]
            wts = (w_qkv_o[o].astype(BF16), row(qn_o[o]), row(kn_o[o]), w_out_o[o].astype(BF16), n_heads)
            xp, k1, v1 = _odd_layer(xp, None, None, wts, gm, gl, wu, wd)
            xs, k2, v2 = _odd_layer(xs, cache_k[o].astype(F32), cache_v[o].astype(F32), wts, gm, gl, wu, wd)
            pk.append(k1); pv.append(v1)
            sk.append(k2); sv.append(v2)
    return (xp, xs,
            jnp.stack(pd), jnp.stack(pq), jnp.stack(pw), jnp.stack(pk), jnp.stack(pv),
            jnp.stack(sd), jnp.stack(sq), jnp.stack(sw), jnp.stack(sk), jnp.stack(sv))
```

```python
import functools

import jax
import jax.numpy as jnp
from jax import lax
from jax.experimental import pallas as pl
from jax.experimental.pallas import tpu as pltpu

F32 = jnp.float32
BF16 = jnp.bfloat16
EPS = 1e-6
CHUNK = 64
LANES = 128
PAIR = 2 * CHUNK
QBUF_OFF = 8
DBUF_OFF = 32
DW_ROWS = 32
SUBLANES = 8
VMEM_LIMIT = 56 * 1024 * 1024
LOG2E = 1.4426950408889634
LOG2_W_FLOOR = -151.0
MASKED = 1e30


def _mm(a, b):
    return jnp.dot(a.astype(BF16), b.astype(BF16), preferred_element_type=F32)


def _mm_nt(a, b):
    return lax.dot_general(a.astype(BF16), b.astype(BF16), (((1,), (1,)), ((), ())),
                           preferred_element_type=F32)


def _sigmoid(x):
    return 0.5 * jnp.tanh(0.5 * x) + 0.5


def _silu(x):
    h = 0.5 * x
    return h + h * jnp.tanh(h)


def _softplus(x):
    return jnp.maximum(x, 0.0) + jnp.log1p(jnp.exp(-jnp.abs(x)))


def _rms_scale(x):
    return lax.rsqrt(jnp.mean(x * x, axis=-1, keepdims=True) + EPS)


def _const_spec(shape):
    nd = len(shape)
    return pl.BlockSpec(shape, lambda *_: (0,) * nd, pipeline_mode=pl.Buffered(1))


def _even_pre_kernel(x_ref, g_ref, w_ref, cw_ref, alog_ref, dtb_ref, dww_ref, dwb_ref, lng_ref, lnb_ref,
                     qbuf_ref, dbuf_ref,
                     qkv_ref, gcol_ref, z_ref, c_ref, qlast_ref, dlast_ref,
                     qext, uext, ushift, *, tl, qkv_w, v_w, c_b, n_heads):
    kq = cw_ref.shape[0]
    kd = dww_ref.shape[0]

    @pl.when(pl.program_id(1) == 0)
    def _():
        qext[QBUF_OFF - (kq - 1):QBUF_OFF, :] = qbuf_ref[0]
        uext[0:DBUF_OFF - (kd - 1), :] = jnp.zeros((DBUF_OFF - (kd - 1), uext.shape[1]), F32)
        uext[DBUF_OFF - (kd - 1):DBUF_OFF, :] = dbuf_ref[0]

    x = x_ref[0]
    h = (x * _rms_scale(x) * g_ref[...]).astype(BF16)

    glu0 = qkv_w + v_w
    qext[QBUF_OFF:QBUF_OFF + tl, :] = jnp.dot(h, w_ref[:, 0:qkv_w], preferred_element_type=F32)
    z_ref[0] = jnp.dot(h, w_ref[:, qkv_w:glu0], preferred_element_type=F32)
    ga = jnp.dot(h, w_ref[:, glu0:glu0 + c_b], preferred_element_type=F32)
    gb = jnp.dot(h, w_ref[:, glu0 + c_b:glu0 + 2 * c_b], preferred_element_type=F32)
    uext[DBUF_OFF:DBUF_OFF + tl, :] = ga * _sigmoid(gb)
    ab = jnp.dot(h, w_ref[:, glu0 + 2 * c_b:glu0 + 2 * c_b + LANES], preferred_element_type=F32)

    for s in range(qkv_w // LANES):
        cols = slice(s * LANES, (s + 1) * LANES)
        y = None
        for j in range(kq):
            r0 = QBUF_OFF - (kq - 1) + j
            t = cw_ref[j:j + 1, cols] * qext[r0:r0 + tl, cols]
            y = t if y is None else y + t
        y = _silu(y)
        if s < 2 * n_heads:
            y = y * lax.rsqrt(jnp.sum(y * y, axis=-1, keepdims=True) + EPS)
        qkv_ref[0, :, cols] = y
    qlast = qext[QBUF_OFF + tl - (kq - 1):QBUF_OFF + tl, :]
    qlast_ref[0] = qlast
    qext[QBUF_OFF - (kq - 1):QBUF_OFF, :] = qlast

    span = tl + DBUF_OFF - SUBLANES
    for b in range(1, SUBLANES):
        ushift[b - 1, 0:span, :] = uext[b:b + span, :]
    for r in range(tl // DW_ROWS):
        acc = None
        for j in range(kd):
            a, b = divmod(DBUF_OFF - (kd - 1) + j, SUBLANES)
            r0 = a * SUBLANES + r * DW_ROWS
            win = uext[r0:r0 + DW_ROWS, :] if b == 0 else ushift[b - 1, r0:r0 + DW_ROWS, :]
            t = dww_ref[j:j + 1, :] * win
            acc = t if acc is None else acc + t
        cpre = acc + dwb_ref[...]
        mu = jnp.mean(cpre, axis=-1, keepdims=True)
        xc = cpre - mu
        var = jnp.mean(xc * xc, axis=-1, keepdims=True)
        y = xc * lax.rsqrt(var + EPS) * lng_ref[...] + lnb_ref[...]
        c_ref[0, r * DW_ROWS:(r + 1) * DW_ROWS, :] = _silu(y).astype(c_ref.dtype)
    dlast = uext[DBUF_OFF + tl - (kd - 1):DBUF_OFF + tl, :]
    dlast_ref[0] = dlast
    uext[DBUF_OFF - (kd - 1):DBUF_OFF, :] = dlast

    lane = lax.broadcasted_iota(jnp.int32, (tl, LANES), 1)
    g = -jnp.exp(alog_ref[...]) * _softplus(ab + dtb_ref[...])
    g = jnp.where(lane < n_heads, g, 0.0)
    beta = _sigmoid(ab)
    row = lax.broadcasted_iota(jnp.int32, (tl, tl), 0)
    col = lax.broadcasted_iota(jnp.int32, (tl, tl), 1)
    tri = (((row ^ col) < CHUNK) & (col <= row)).astype(BF16)
    g_hi = g.astype(BF16)
    g_r1 = g - g_hi.astype(F32)
    g_mid = g_r1.astype(BF16)
    g_lo = (g_r1 - g_mid.astype(F32)).astype(BF16)
    gcum = (jnp.dot(tri, g_hi, preferred_element_type=F32)
            + jnp.dot(tri, g_mid, preferred_element_type=F32)
            + jnp.dot(tri, g_lo, preferred_element_type=F32))
    gcol_ref[0] = jnp.where(lane < n_heads, gcum, jnp.where(lane < 2 * n_heads, beta, 0.0))


def _even_pre(x, gain, w_all, conv_w, alog_pad, dtb_pad, dw_w, dw_b, ln_g, ln_b, qbuf, dbuf, *, n_heads, qkv_w, v_w, c_b):
    n_seq, L, D = x.shape
    tl = min(256, L)
    assert L % tl == 0 and tl % DW_ROWS == 0 and tl % CHUNK == 0
    kq, kd = conv_w.shape[0], dw_w.shape[0]
    assert kq - 1 <= QBUF_OFF and kd - 1 <= DBUF_OFF and tl >= kd - 1
    nw = w_all.shape[1]
    kern = functools.partial(_even_pre_kernel, tl=tl, qkv_w=qkv_w, v_w=v_w, c_b=c_b, n_heads=n_heads)
    seq_blk = lambda w: pl.BlockSpec((1, tl, w), lambda s, j: (s, j, 0))
    per_seq = lambda r, w: pl.BlockSpec((1, r, w), lambda s, j: (s, 0, 0))
    return pl.pallas_call(
        kern,
        grid=(n_seq, L // tl),
        in_specs=[seq_blk(D), _const_spec((1, D)), _const_spec((D, nw)), _const_spec((kq, qkv_w)),
                  _const_spec((1, LANES)), _const_spec((1, LANES)), _const_spec((kd, c_b)),
                  _const_spec((1, c_b)), _const_spec((1, c_b)), _const_spec((1, c_b)),
                  per_seq(kq - 1, qkv_w), per_seq(kd - 1, c_b)],
        out_specs=[seq_blk(qkv_w), seq_blk(LANES), seq_blk(v_w), seq_blk(c_b),
                   per_seq(kq - 1, qkv_w), per_seq(kd - 1, c_b)],
        out_shape=[jax.ShapeDtypeStruct((n_seq, L, qkv_w), F32),
                   jax.ShapeDtypeStruct((n_seq, L, LANES), F32),
                   jax.ShapeDtypeStruct((n_seq, L, v_w), F32),
                   jax.ShapeDtypeStruct((n_seq, L, c_b), BF16),
                   jax.ShapeDtypeStruct((n_seq, kq - 1, qkv_w), F32),
                   jax.ShapeDtypeStruct((n_seq, kd - 1, c_b), F32)],
        scratch_shapes=[pltpu.VMEM((QBUF_OFF + tl, qkv_w), F32), pltpu.VMEM((DBUF_OFF + tl, c_b), F32),
                        pltpu.VMEM((SUBLANES - 1, DBUF_OFF + tl, c_b), F32)],
        compiler_params=pltpu.CompilerParams(dimension_semantics=("arbitrary", "arbitrary"),
                                             vmem_limit_bytes=VMEM_LIMIT),
        name="even_pre",
    )(x, gain, w_all, conv_w, alog_pad, dtb_pad, dw_w, dw_b, ln_g, ln_b, qbuf, dbuf)


def _unit_lower_inverses(a_list, row, col):
    x = row ^ col
    eye = (row == col).astype(F32)
    a8 = [jnp.where(x < 8, a, 0.0) for a in a_list]
    a8_2 = [_mm(t, t) for t in a8]
    p = [eye - t for t in a8]
    a8_4 = [_mm(t, t) for t in a8_2]
    p = [t + _mm(t, sq) for t, sq in zip(p, a8_2)]
    d = [t + _mm(t, sq) for t, sq in zip(p, a8_4)]
    s = 8
    while s < CHUNK:
        off = [jnp.where((x >= s) & (x < 2 * s), a, 0.0) for a in a_list]
        od = [_mm(o, t) for o, t in zip(off, d)]
        d = [t - _mm(t, u) for t, u in zip(d, od)]
        s *= 2
    return d


def _delta_kernel(qkv_ref, gcol_ref, z_ref, onorm_ref, sin_ref, o_ref, sout_ref, *, tb, n_heads, chained):
    dk = dv = LANES
    qk_w = n_heads * dk
    n_pairs = tb // PAIR
    if chained:
        @pl.when(pl.program_id(1) == 0)
        def _():
            sout_ref[...] = sin_ref[...]

    row = lax.broadcasted_iota(jnp.int32, (PAIR, PAIR), 0)
    col = lax.broadcasted_iota(jnp.int32, (PAIR, PAIR), 1)
    same = (row ^ col) < CHUNK
    causal = same & (col <= row)
    strict = same & (col < row)
    in_chunk = [(col >= c * CHUNK) & (col < (c + 1) * CHUNK) for c in range(2)]

    tiles = [(p, h) for p in range(n_pairs) for h in range(n_heads)]
    q, k, kb, vb, gc, a, qk = {}, {}, {}, {}, {}, {}, {}
    for p, h in tiles:
        rows = slice(p * PAIR, (p + 1) * PAIR)
        q[p, h] = qkv_ref[0, rows, h * dk:(h + 1) * dk] * (dk ** -0.5)
        k[p, h] = qkv_ref[0, rows, qk_w + h * dk:qk_w + (h + 1) * dk]
        gc[p, h] = jnp.broadcast_to(gcol_ref[0, rows, h:h + 1], (PAIR, PAIR))
        bt = jnp.broadcast_to(gcol_ref[0, rows, n_heads + h:n_heads + h + 1], (PAIR, PAIR))
        kb[p, h] = k[p, h] * bt
        vb[p, h] = qkv_ref[0, rows, 2 * qk_w + h * dv:2 * qk_w + (h + 1) * dv] * bt
    for t in tiles:
        diff = gc[t] - gc[t].T
        decay = jnp.where(causal, jnp.exp(jnp.where(causal, diff, 0.0)), 0.0)
        a[t] = jnp.where(strict, _mm_nt(kb[t], k[t]) * decay, 0.0)
        qk[t] = jnp.where(causal, _mm_nt(q[t], k[t]) * decay, 0.0)
    inv = dict(zip(tiles, _unit_lower_inverses([a[t] for t in tiles], row, col)))
    u, w, qg, kg_t, gl = {}, {}, {}, {}, {}
    for t in tiles:
        eg = jnp.exp(gc[t])
        u[t] = _mm(inv[t], vb[t])
        w[t] = _mm(inv[t], kb[t] * eg)
        qg[t] = q[t] * eg
        gl_rows = [jnp.broadcast_to(gc[t][(c + 1) * CHUNK - 1:(c + 1) * CHUNK, :], (PAIR, PAIR)) for c in range(2)]
        gl[t] = [jnp.exp(g) for g in gl_rows]
        gcl = jnp.where(row < CHUNK, gl_rows[0], gl_rows[1])
        kg_t[t] = (k[t] * jnp.exp(gcl - gc[t])).T

    state = [sout_ref[0, h] for h in range(n_heads)] if chained else None
    for p in range(n_pairs):
        rows = slice(p * PAIR, (p + 1) * PAIR)
        o_parts = {h: [] for h in range(n_heads)}
        for c in range(2):
            r = slice(c * CHUNK, (c + 1) * CHUNK)
            for h in range(n_heads):
                t = (p, h)
                s_old = state[h] if chained else sin_ref[c, h]
                ws_qs = _mm(jnp.concatenate([w[t][r], qg[t][r]], axis=0), s_old)
                v_new = u[t][r] - ws_qs[:CHUNK]
                pad = jnp.zeros_like(v_new)
                v_pad = jnp.concatenate([v_new, pad] if c == 0 else [pad, v_new], axis=0)
                o_parts[h].append(ws_qs[CHUNK:] + _mm(qk[t][r], v_pad))
                s_new = s_old * gl[t][c] + _mm(jnp.where(in_chunk[c], kg_t[t], 0.0), v_pad)
                if chained:
                    state[h] = s_new
                else:
                    sout_ref[c, h] = s_new
        for h in range(n_heads):
            o = jnp.concatenate(o_parts[h], axis=0)
            zz = z_ref[0, rows, h * dv:(h + 1) * dv].astype(F32)
            y = (o * _rms_scale(o) * onorm_ref[...]) * _silu(zz)
            o_ref[0, rows, h * dv:(h + 1) * dv] = y.astype(o_ref.dtype)
    if chained:
        for h in range(n_heads):
            sout_ref[0, h] = state[h]


def _delta(qkv, gcol, z, onorm, s0, *, n_heads):
    n_seq, L, qkv_w = qkv.shape
    v_w = z.shape[-1]
    assert qkv_w == 3 * n_heads * LANES and v_w == n_heads * LANES
    if L % PAIR == 0:
        chained, n_grp, Lg, per_grp = True, n_seq, L, 1
    else:
        assert L == CHUNK and n_seq % 2 == 0
        chained, n_grp, Lg, per_grp = False, n_seq // 2, PAIR, 2
        qkv, gcol, z = (t.reshape(n_grp, PAIR, t.shape[-1]) for t in (qkv, gcol, z))
    tb = min(512, Lg)
    assert Lg % tb == 0
    kern = functools.partial(_delta_kernel, tb=tb, n_heads=n_heads, chained=chained)
    blk = lambda w: pl.BlockSpec((1, tb, w), lambda s, j: (s, j, 0))
    st = pl.BlockSpec((per_grp, n_heads, LANES, LANES), lambda s, j: (s, 0, 0, 0))
    o, s_new = pl.pallas_call(
        kern,
        grid=(n_grp, Lg // tb),
        in_specs=[blk(qkv_w), blk(LANES), blk(v_w), _const_spec((1, LANES)), st],
        out_specs=[blk(v_w), st],
        out_shape=[jax.ShapeDtypeStruct((n_grp, Lg, v_w), BF16),
                   jax.ShapeDtypeStruct(s0.shape, F32)],
        compiler_params=pltpu.CompilerParams(dimension_semantics=("arbitrary", "arbitrary"),
                                             vmem_limit_bytes=VMEM_LIMIT),
        name="delta_rule",
    )(qkv, gcol, z, onorm, s0)
    return o.reshape(n_seq, L, v_w), s_new


def _out_mlp_kernel(*refs, n_a, tf):
    x_ref = refs[0]
    a_refs = refs[1:1 + n_a]
    w_ref, g_ref, wup_ref, wdn_ref, o_ref = refs[1 + n_a:]
    a = a_refs[0][...] if n_a == 1 else jnp.concatenate([a_ref[...] for a_ref in a_refs], axis=-1)
    x1 = x_ref[...] + jnp.dot(a, w_ref[...], preferred_element_type=F32)
    h = (x1 * _rms_scale(x1) * g_ref[...]).astype(BF16)
    acc = x1
    for f in range(wup_ref.shape[1] // tf):
        r = jnp.maximum(jnp.dot(h, wup_ref[:, f * tf:(f + 1) * tf], preferred_element_type=F32), 0.0)
        acc = acc + jnp.dot((r * r).astype(BF16), wdn_ref[f * tf:(f + 1) * tf, :], preferred_element_type=F32)
    o_ref[...] = acc


def _out_mlp(x, a_list, w_out, gain, w_up, w_down):
    shape = x.shape
    D = shape[-1]
    x2 = x.reshape(-1, D)
    n = x2.shape[0]
    a_list = [a.reshape(n, a.shape[-1]) for a in a_list]
    tm = min(512, n)
    assert n % tm == 0
    F = w_up.shape[1]
    tf = min(1024, F)
    assert F % tf == 0
    kern = functools.partial(_out_mlp_kernel, n_a=len(a_list), tf=tf)
    rows = lambda w: pl.BlockSpec((tm, w), lambda i: (i, 0))
    out = pl.pallas_call(
        kern,
        grid=(n // tm,),
        in_specs=([rows(D)] + [rows(a.shape[1]) for a in a_list] + [_const_spec(w_out.shape)]
                  + [_const_spec((1, D)), _const_spec(w_up.shape), _const_spec(w_down.shape)]),
        out_specs=rows(D),
        out_shape=jax.ShapeDtypeStruct((n, D), F32),
        compiler_params=pltpu.CompilerParams(dimension_semantics=("arbitrary",), vmem_limit_bytes=VMEM_LIMIT),
        name="out_mlp",
    )(x2, *a_list, w_out, gain, w_up, w_down)
    return out.reshape(shape)


def _odd_pre_kernel(x_ref, g_ref, w_ref, qn_ref, kn_ref, q_ref, k_ref, v_ref, *, n_heads):
    dh = LANES
    hd = n_heads * dh
    x = x_ref[0]
    h = (x * _rms_scale(x) * g_ref[...]).astype(BF16)
    q = jnp.dot(h, w_ref[:, 0:hd], preferred_element_type=F32)
    k = jnp.dot(h, w_ref[:, hd:2 * hd], preferred_element_type=F32)
    v = jnp.dot(h, w_ref[:, 2 * hd:3 * hd], preferred_element_type=F32)
    for hh in range(n_heads):
        cols = slice(hh * dh, (hh + 1) * dh)
        qh = q[:, cols]
        kh = k[:, cols]
        q_ref[0, hh] = (qh * _rms_scale(qh) * qn_ref[...] * (dh ** -0.5 * LOG2E)).astype(q_ref.dtype)
        k_ref[0, hh] = kh * _rms_scale(kh) * kn_ref[...]
        v_ref[0, hh] = v[:, cols]


def _odd_pre(x, gain, w_qkv, qn, kn, *, n_heads):
    B, L, D = x.shape
    tl = min(256, L)
    assert L % tl == 0 and w_qkv.shape[1] == 3 * n_heads * LANES
    kern = functools.partial(_odd_pre_kernel, n_heads=n_heads)
    head_major = pl.BlockSpec((1, n_heads, tl, LANES), lambda b, j: (b, 0, j, 0))
    hm_shape = (B, n_heads, L, LANES)
    return pl.pallas_call(
        kern,
        grid=(B, L // tl),
        in_specs=[pl.BlockSpec((1, tl, D), lambda b, j: (b, j, 0)), _const_spec((1, D)),
                  _const_spec(w_qkv.shape), _const_spec((1, LANES)), _const_spec((1, LANES))],
        out_specs=[head_major, head_major, head_major],
        out_shape=[jax.ShapeDtypeStruct(hm_shape, BF16), jax.ShapeDtypeStruct(hm_shape, F32),
                   jax.ShapeDtypeStruct(hm_shape, F32)],
        compiler_params=pltpu.CompilerParams(dimension_semantics=("arbitrary", "arbitrary"),
                                             vmem_limit_bytes=VMEM_LIMIT),
        name="odd_pre",
    )(x, gain, w_qkv, qn, kn)


def _suffix_ones(tk):
    j = lax.broadcasted_iota(jnp.int32, (2 * tk, tk), 0) & (tk - 1)
    s = lax.broadcasted_iota(jnp.int32, (2 * tk, tk), 1)
    return (j > s).astype(BF16)


def _sb_block(q, k_blk, v_blk, u, off, acc, below=None):
    z = _mm_nt(q, k_blk)
    t = jnp.maximum(z, 0.0) + jnp.log(1.0 + jnp.exp2(-jnp.abs(z))) * LOG2E
    if below is not None:
        t = jnp.where(below, t, 0.0)
    t_hi = t.astype(BF16)
    t_lo = (t - t_hi.astype(F32)).astype(BF16)
    later = jnp.dot(jnp.concatenate([t_hi, t_lo], axis=1), u, preferred_element_type=F32)
    wts = jnp.exp2(z - t - (later + off))
    return later[:, 0:1] + t[:, 0:1], acc + _mm(wts, v_blk)


def _sb_kernel(*refs, tq, tkp, n_past, n_q, hb):
    if n_past:
        q_ref, kn_ref, vn_ref, kp_ref, vp_ref, o_ref = refs
    else:
        q_ref, kn_ref, vn_ref, o_ref = refs
    i = pl.program_id(2)
    u_self = _suffix_ones(tq)
    u_past = _suffix_ones(tkp) if n_past else None
    row = lax.broadcasted_iota(jnp.int32, (tq, tq), 0)
    col = lax.broadcasted_iota(jnp.int32, (tq, tq), 1)
    below = col < row
    not_below = jnp.where(below, 0.0, MASKED)
    start = pl.multiple_of(i * tq, tq)

    walks = []
    for hh in range(hb):
        q = q_ref[0, hh]
        acc = jnp.zeros((tq, LANES), F32)
        spent, acc = _sb_block(q, kn_ref[0, hh, pl.ds(start, tq), :], vn_ref[0, hh, pl.ds(start, tq), :],
                               u_self, not_below, acc, below)
        first_self, first_past = i - 1, n_past - 1
        if n_q > 1:
            prev = pl.multiple_of(jnp.maximum(i - 1, 0) * tq, tq)
            has_prev = jnp.full((tq, 1), i, jnp.int32) >= 1
            own, acc = _sb_block(q, kn_ref[0, hh, pl.ds(prev, tq), :], vn_ref[0, hh, pl.ds(prev, tq), :],
                                 u_self, jnp.where(has_prev, spent, MASKED), acc)
            spent = spent + jnp.where(has_prev, own, 0.0)
            first_self = i - 2
        elif n_past:
            lastp = (n_past - 1) * tkp
            own, acc = _sb_block(q, kp_ref[0, hh, lastp:lastp + tkp, :], vp_ref[0, hh, lastp:lastp + tkp, :],
                                 u_past, spent, acc)
            spent = spent + own
            first_past = n_past - 2
        walks.append((q, spent, acc, first_self, first_past))

    def walk(q, k_ref, v_ref, hh, tk, u, first, spent, acc):
        def cond(st):
            return (st[0] >= 0) & (jnp.min(st[1]) < -LOG2_W_FLOOR)

        def body(st):
            j, spent, acc = st
            s0 = pl.multiple_of(j * tk, tk)
            own, acc = _sb_block(q, k_ref[0, hh, pl.ds(s0, tk), :], v_ref[0, hh, pl.ds(s0, tk), :], u, spent, acc)
            return j - 1, spent + own, acc

        _, spent, acc = lax.while_loop(cond, body, (first, spent, acc))
        return spent, acc

    for hh, (q, spent, acc, first_self, first_past) in enumerate(walks):
        if n_q > 1:
            spent, acc = walk(q, kn_ref, vn_ref, hh, tq, u_self, first_self, spent, acc)
        if n_past and (n_q > 1 or n_past > 1):
            spent, acc = walk(q, kp_ref, vp_ref, hh, tkp, u_past, jnp.int32(first_past), spent, acc)
        o_ref[0, :, hh * LANES:(hh + 1) * LANES] = acc.astype(o_ref.dtype)


def _stick_breaking(q, k_new, v_new, k_past, v_past):
    B, H, L, dh = q.shape
    assert dh == LANES
    tq = min(256, L)
    assert L % tq == 0
    n_q = L // tq
    assert tq & (tq - 1) == 0
    hb = H if n_q == 1 else 2
    assert H % hb == 0
    args = [q, k_new, v_new]
    full = lambda n: pl.BlockSpec((1, hb, n, LANES), lambda b, h, i: (b, h, 0, 0))
    in_specs = [pl.BlockSpec((1, hb, tq, LANES), lambda b, h, i: (b, h, i, 0)), full(L), full(L)]
    n_past, tkp = 0, 0
    if k_past is not None:
        P = k_past.shape[2]
        tkp = min(256, P)
        assert P % tkp == 0 and tkp & (tkp - 1) == 0
        n_past = P // tkp
        args += [k_past, v_past]
        in_specs += [full(P), full(P)]
    kern = functools.partial(_sb_kernel, tq=tq, tkp=tkp, n_past=n_past, n_q=n_q, hb=hb)
    return pl.pallas_call(
        kern,
        grid=(B, H // hb, n_q),
        in_specs=in_specs,
        out_specs=pl.BlockSpec((1, tq, hb * LANES), lambda b, h, i: (b, i, h)),
        out_shape=jax.ShapeDtypeStruct((B, L, H * LANES), BF16),
        compiler_params=pltpu.CompilerParams(dimension_semantics=("arbitrary", "arbitrary", "arbitrary"),
                                             vmem_limit_bytes=VMEM_LIMIT),
        name="stick_breaking",
    )(*args)


def _pad_lanes(v):
    return jnp.zeros((1, LANES), F32).at[0, :v.shape[0]].set(v.astype(F32))


def _even_layer(x, s0, qbuf, dbuf, wts, gain_mix, gain_mlp, w_up, w_down):
    (w_all, conv_w, alog_pad, dtb_pad, onorm, dw_w, dw_b, ln_g, ln_b, w_out, n_heads, qkv_w, v_w, c_b) = wts
    qkv, gcol, z, c, qlast, dlast = _even_pre(x, gain_mix, w_all, conv_w, alog_pad, dtb_pad, dw_w, dw_b, ln_g, ln_b,
                                              qbuf, dbuf, n_heads=n_heads, qkv_w=qkv_w, v_w=v_w, c_b=c_b)
    o, s_new = _delta(qkv, gcol, z, onorm, s0, n_heads=n_heads)
    y = _out_mlp(x, [o, c], w_out, gain_mlp, w_up, w_down)
    return y, s_new, qlast, dlast


def _odd_layer(x, k_past, v_past, wts, gain_mix, gain_mlp, w_up, w_down):
    w_qkv, qn, kn, w_out, n_heads = wts
    q, k, v = _odd_pre(x, gain_mix, w_qkv, qn, kn, n_heads=n_heads)
    o = _stick_breaking(q, k, v, k_past, v_past)
    y = _out_mlp(x, [o], w_out, gain_mlp, w_up, w_down)
    return y, k, v


def kernel(x_prompt, x_sample, state_delta, state_qkv_conv, state_dw_conv, cache_k, cache_v, norm_mix, norm_mlp, w_in_e, conv_qkv_e, a_log_e, dt_bias_e, onorm_e, dw_w_e, dw_b_e, ln_g_e, ln_b_e, w_out_e, w_qkv_o, qn_o, kn_o, w_out_o, w_up, w_down):
    depth = norm_mix.shape[0]
    bp = x_prompt.shape[0]
    xp, xs = x_prompt, x_sample
    pd, pq, pw, pk, pv = [], [], [], [], []
    sd, sq, sw, sk, sv = [], [], [], [], []
    row = lambda v: v.astype(F32).reshape(1, -1)
    for i in range(depth):
        gm, gl = row(norm_mix[i]), row(norm_mlp[i])
        wu, wd = w_up[i].astype(BF16), w_down[i].astype(BF16)
        if i % 2 == 0:
            e = i // 2
            n_heads = a_log_e.shape[1]
            qkv_w = conv_qkv_e.shape[2]
            c_b = dw_w_e.shape[2]
            v_w = n_heads * onorm_e.shape[1]
            w_in = w_in_e[e]
            ab0 = qkv_w + v_w
            ab = jnp.zeros((w_in.shape[0], LANES), w_in.dtype).at[:, :2 * n_heads].set(w_in[:, ab0:ab0 + 2 * n_heads])
            w_all = jnp.concatenate([w_in[:, :ab0], w_in[:, ab0 + 2 * n_heads:], ab], axis=1).astype(BF16)
            w_out = w_out_e[e].astype(BF16)
            wts = (w_all, conv_qkv_e[e].astype(F32), _pad_lanes(a_log_e[e]), _pad_lanes(dt_bias_e[e]),
                   row(onorm_e[e]), dw_w_e[e].astype(F32), row(dw_b_e[e]), row(ln_g_e[e]), row(ln_b_e[e]),
                   w_out, n_heads, qkv_w, v_w, c_b)
            kq, kd = conv_qkv_e.shape[1], dw_w_e.shape[1]
            s0 = jnp.zeros((bp,) + state_delta.shape[2:], F32)
            qb0 = jnp.zeros((bp, kq - 1, qkv_w), F32)
            db0 = jnp.zeros((bp, kd - 1, c_b), F32)
            xp, d1, q1, c1 = _even_layer(xp, s0, qb0, db0, wts, gm, gl, wu, wd)
            xs, d2, q2, c2 = _even_layer(xs, state_delta[e].astype(F32), state_qkv_conv[e].astype(F32),
                                         state_dw_conv[e].astype(F32), wts, gm, gl, wu, wd)
            pd.append(d1); pq.append(q1); pw.append(c1)
            sd.append(d2); sq.append(q2); sw.append(c2)
        else:
            o = i // 2
            n_heads = cache_k.shape[2]
            wts = (w_qkv_o[o].astype(BF16), row(qn_o[o]), row(kn_o[o]), w_out_o[o].astype(BF16), n_heads)
            xp, k1, v1 = _odd_layer(xp, None, None, wts, gm, gl, wu, wd)
            xs, k2, v2 = _odd_layer(xs, cache_k[o].astype(F32), cache_v[o].astype(F32), wts, gm, gl, wu, wd)
            pk.append(k1); pv.append(v1)
            sk.append(k2); sv.append(v2)
    return (xp, xs,
            jnp.stack(pd), jnp.stack(pq), jnp.stack(pw), jnp.stack(pk), jnp.stack(pv),
            jnp.stack(sd), jnp.stack(sq), jnp.stack(sw), jnp.stack(sk), jnp.stack(sv))
```

```python
import functools

import jax
import jax.numpy as jnp
from jax import lax
from jax.experimental import pallas as pl
from jax.experimental.pallas import tpu as pltpu

F32 = jnp.float32
BF16 = jnp.bfloat16
EPS = 1e-6
CHUNK = 64
LANES = 128
PAIR = 2 * CHUNK
QBUF_OFF = 8
DBUF_OFF = 32
DW_ROWS = 32
SUBLANES = 8
VMEM_LIMIT = 56 * 1024 * 1024
LOG2E = 1.4426950408889634
LOG2_W_FLOOR = -151.0
MASKED = 1e30


def _mm(a, b):
    return jnp.dot(a.astype(BF16), b.astype(BF16), preferred_element_type=F32)


def _mm_nt(a, b):
    return lax.dot_general(a.astype(BF16), b.astype(BF16), (((1,), (1,)), ((), ())),
                           preferred_element_type=F32)


def _sigmoid(x):
    return 0.5 * jnp.tanh(0.5 * x) + 0.5


def _silu(x):
    h = 0.5 * x
    return h + h * jnp.tanh(h)


def _softplus(x):
    return jnp.maximum(x, 0.0) + jnp.log1p(jnp.exp(-jnp.abs(x)))


def _rms_scale(x):
    return lax.rsqrt(jnp.mean(x * x, axis=-1, keepdims=True) + EPS)


def _const_spec(shape):
    nd = len(shape)
    return pl.BlockSpec(shape, lambda *_: (0,) * nd, pipeline_mode=pl.Buffered(1))


def _even_pre_kernel(x_ref, g_ref, w_ref, cw_ref, alog_ref, dtb_ref, dww_ref, dwb_ref, lng_ref, lnb_ref,
                     qbuf_ref, dbuf_ref,
                     qkv_ref, gcol_ref, z_ref, c_ref, qlast_ref, dlast_ref,
                     qext, uext, ushift, *, tl, qkv_w, v_w, c_b, n_heads):
    kq = cw_ref.shape[0]
    kd = dww_ref.shape[0]

    @pl.when(pl.program_id(1) == 0)
    def _():
        qext[QBUF_OFF - (kq - 1):QBUF_OFF, :] = qbuf_ref[0]
        uext[0:DBUF_OFF - (kd - 1), :] = jnp.zeros((DBUF_OFF - (kd - 1), uext.shape[1]), F32)
        uext[DBUF_OFF - (kd - 1):DBUF_OFF, :] = dbuf_ref[0]

    x = x_ref[0]
    h = (x * _rms_scale(x) * g_ref[...]).astype(BF16)

    glu0 = qkv_w + v_w
    qext[QBUF_OFF:QBUF_OFF + tl, :] = jnp.dot(h, w_ref[:, 0:qkv_w], preferred_element_type=F32)
    z_ref[0] = jnp.dot(h, w_ref[:, qkv_w:glu0], preferred_element_type=F32)
    ga = jnp.dot(h, w_ref[:, glu0:glu0 + c_b], preferred_element_type=F32)
    gb = jnp.dot(h, w_ref[:, glu0 + c_b:glu0 + 2 * c_b], preferred_element_type=F32)
    uext[DBUF_OFF:DBUF_OFF + tl, :] = ga * _sigmoid(gb)
    ab = jnp.dot(h, w_ref[:, glu0 + 2 * c_b:glu0 + 2 * c_b + LANES], preferred_element_type=F32)

    for s in range(qkv_w // LANES):
        cols = slice(s * LANES, (s + 1) * LANES)
        y = None
        for j in range(kq):
            r0 = QBUF_OFF - (kq - 1) + j
            t = cw_ref[j:j + 1, cols] * qext[r0:r0 + tl, cols]
            y = t if y is None else y + t
        y = _silu(y)
        if s < 2 * n_heads:
            y = y * lax.rsqrt(jnp.sum(y * y, axis=-1, keepdims=True) + EPS)
        qkv_ref[0, :, cols] = y
    qlast = qext[QBUF_OFF + tl - (kq - 1):QBUF_OFF + tl, :]
    qlast_ref[0] = qlast
    qext[QBUF_OFF - (kq - 1):QBUF_OFF, :] = qlast

    span = tl + DBUF_OFF - SUBLANES
    for b in range(1, SUBLANES):
        ushift[b - 1, 0:span, :] = uext[b:b + span, :]
    for r in range(tl // DW_ROWS):
        acc = None
        for j in range(kd):
            a, b = divmod(DBUF_OFF - (kd - 1) + j, SUBLANES)
            r0 = a * SUBLANES + r * DW_ROWS
            win = uext[r0:r0 + DW_ROWS, :] if b == 0 else ushift[b - 1, r0:r0 + DW_ROWS, :]
            t = dww_ref[j:j + 1, :] * win
            acc = t if acc is None else acc + t
        cpre = acc + dwb_ref[...]
        mu = jnp.mean(cpre, axis=-1, keepdims=True)
        xc = cpre - mu
        var = jnp.mean(xc * xc, axis=-1, keepdims=True)
        y = xc * lax.rsqrt(var + EPS) * lng_ref[...] + lnb_ref[...]
        c_ref[0, r * DW_ROWS:(r + 1) * DW_ROWS, :] = _silu(y).astype(c_ref.dtype)
    dlast = uext[DBUF_OFF + tl - (kd - 1):DBUF_OFF + tl, :]
    dlast_ref[0] = dlast
    uext[DBUF_OFF - (kd - 1):DBUF_OFF, :] = dlast

    lane = lax.broadcasted_iota(jnp.int32, (tl, LANES), 1)
    g = -jnp.exp(alog_ref[...]) * _softplus(ab + dtb_ref[...])
    g = jnp.where(lane < n_heads, g, 0.0)
    beta = _sigmoid(ab)
    row = lax.broadcasted_iota(jnp.int32, (tl, tl), 0)
    col = lax.broadcasted_iota(jnp.int32, (tl, tl), 1)
    tri = (((row ^ col) < CHUNK) & (col <= row)).astype(BF16)
    g_hi = g.astype(BF16)
    g_r1 = g - g_hi.astype(F32)
    g_mid = g_r1.astype(BF16)
    g_lo = (g_r1 - g_mid.astype(F32)).astype(BF16)
    gcum = (jnp.dot(tri, g_hi, preferred_element_type=F32)
            + jnp.dot(tri, g_mid, preferred_element_type=F32)
            + jnp.dot(tri, g_lo, preferred_element_type=F32))
    gcol_ref[0] = jnp.where(lane < n_heads, gcum, jnp.where(lane < 2 * n_heads, beta, 0.0))


def _even_pre(x, gain, w_all, conv_w, alog_pad, dtb_pad, dw_w, dw_b, ln_g, ln_b, qbuf, dbuf, *, n_heads, qkv_w, v_w, c_b):
    n_seq, L, D = x.shape
    tl = min(256, L)
    assert L % tl == 0 and tl % DW_ROWS == 0 and tl % CHUNK == 0
    kq, kd = conv_w.shape[0], dw_w.shape[0]
    assert kq - 1 <= QBUF_OFF and kd - 1 <= DBUF_OFF and tl >= kd - 1
    nw = w_all.shape[1]
    kern = functools.partial(_even_pre_kernel, tl=tl, qkv_w=qkv_w, v_w=v_w, c_b=c_b, n_heads=n_heads)
    seq_blk = lambda w: pl.BlockSpec((1, tl, w), lambda s, j: (s, j, 0))
    per_seq = lambda r, w: pl.BlockSpec((1, r, w), lambda s, j: (s, 0, 0))
    return pl.pallas_call(
        kern,
        grid=(n_seq, L // tl),
        in_specs=[seq_blk(D), _const_spec((1, D)), _const_spec((D, nw)), _const_spec((kq, qkv_w)),
                  _const_spec((1, LANES)), _const_spec((1, LANES)), _const_spec((kd, c_b)),
                  _const_spec((1, c_b)), _const_spec((1, c_b)), _const_spec((1, c_b)),
                  per_seq(kq - 1, qkv_w), per_seq(kd - 1, c_b)],
        out_specs=[seq_blk(qkv_w), seq_blk(LANES), seq_blk(v_w), seq_blk(c_b),
                   per_seq(kq - 1, qkv_w), per_seq(kd - 1, c_b)],
        out_shape=[jax.ShapeDtypeStruct((n_seq, L, qkv_w), F32),
                   jax.ShapeDtypeStruct((n_seq, L, LANES), F32),
                   jax.ShapeDtypeStruct((n_seq, L, v_w), F32),
                   jax.ShapeDtypeStruct((n_seq, L, c_b), BF16),
                   jax.ShapeDtypeStruct((n_seq, kq - 1, qkv_w), F32),
                   jax.ShapeDtypeStruct((n_seq, kd - 1, c_b), F32)],
        scratch_shapes=[pltpu.VMEM((QBUF_OFF + tl, qkv_w), F32), pltpu.VMEM((DBUF_OFF + tl, c_b), F32),
                        pltpu.VMEM((SUBLANES - 1, DBUF_OFF + tl, c_b), F32)],
        compiler_params=pltpu.CompilerParams(dimension_semantics=("arbitrary", "arbitrary"),
                                             vmem_limit_bytes=VMEM_LIMIT),
        name="even_pre",
    )(x, gain, w_all, conv_w, alog_pad, dtb_pad, dw_w, dw_b, ln_g, ln_b, qbuf, dbuf)


def _unit_lower_inverses(a_list, row, col):
    x = row ^ col
    eye = (row == col).astype(F32)
    a8 = [jnp.where(x < 8, a, 0.0) for a in a_list]
    a8_2 = [_mm(t, t) for t in a8]
    p = [eye - t for t in a8]
    a8_4 = [_mm(t, t) for t in a8_2]
    p = [t + _mm(t, sq) for t, sq in zip(p, a8_2)]
    d = [t + _mm(t, sq) for t, sq in zip(p, a8_4)]
    s = 8
    while s < CHUNK:
        off = [jnp.where((x >= s) & (x < 2 * s), a, 0.0) for a in a_list]
        od = [_mm(o, t) for o, t in zip(off, d)]
        d = [t - _mm(t, u) for t, u in zip(d, od)]
        s *= 2
    return d


def _delta_kernel(qkv_ref, gcol_ref, z_ref, onorm_ref, sin_ref, o_ref, sout_ref, *, tb, n_heads, chained):
    dk = dv = LANES
    qk_w = n_heads * dk
    n_pairs = tb // PAIR
    if chained:
        @pl.when(pl.program_id(1) == 0)
        def _():
            sout_ref[...] = sin_ref[...]

    row = lax.broadcasted_iota(jnp.int32, (PAIR, PAIR), 0)
    col = lax.broadcasted_iota(jnp.int32, (PAIR, PAIR), 1)
    same = (row ^ col) < CHUNK
    causal = same & (col <= row)
    strict = same & (col < row)
    in_chunk = [(col >= c * CHUNK) & (col < (c + 1) * CHUNK) for c in range(2)]

    tiles = [(p, h) for p in range(n_pairs) for h in range(n_heads)]
    q, k, kb, vb, gc, a, qk = {}, {}, {}, {}, {}, {}, {}
    for p, h in tiles:
        rows = slice(p * PAIR, (p + 1) * PAIR)
        q[p, h] = qkv_ref[0, rows, h * dk:(h + 1) * dk] * (dk ** -0.5)
        k[p, h] = qkv_ref[0, rows, qk_w + h * dk:qk_w + (h + 1) * dk]
        gc[p, h] = jnp.broadcast_to(gcol_ref[0, rows, h:h + 1], (PAIR, PAIR))
        bt = jnp.broadcast_to(gcol_ref[0, rows, n_heads + h:n_heads + h + 1], (PAIR, PAIR))
        kb[p, h] = k[p, h] * bt
        vb[p, h] = qkv_ref[0, rows, 2 * qk_w + h * dv:2 * qk_w + (h + 1) * dv] * bt
    for t in tiles:
        diff = gc[t] - gc[t].T
        decay = jnp.where(causal, jnp.exp(jnp.where(causal, diff, 0.0)), 0.0)
        a[t] = jnp.where(strict, _mm_nt(kb[t], k[t]) * decay, 0.0)
        qk[t] = jnp.where(causal, _mm_nt(q[t], k[t]) * decay, 0.0)
    inv = dict(zip(tiles, _unit_lower_inverses([a[t] for t in tiles], row, col)))
    uw, kg_t, gl = {}, {}, {}
    for t in tiles:
        eg = jnp.exp(gc[t])
        uw[t] = _mm(inv[t], jnp.concatenate([vb[t], kb[t] * eg], axis=1))
        gl_rows = [jnp.broadcast_to(gc[t][(c + 1) * CHUNK - 1:(c + 1) * CHUNK, :], (PAIR, PAIR)) for c in range(2)]
        gl[t] = [jnp.exp(g) for g in gl_rows]
        gcl = jnp.where(row < CHUNK, gl_rows[0], gl_rows[1])
        kg_t[t] = (k[t] * jnp.exp(gcl - gc[t])).T
        q[t] = q[t] * eg
    qk_uw = {t: _mm(qk[t], uw[t]) for t in tiles}
    kg_uw = {(t, c): _mm(jnp.where(in_chunk[c], kg_t[t], 0.0), uw[t])
             for t in tiles for c in range(2)}

    state = [sout_ref[0, h] for h in range(n_heads)] if chained else None
    for p in range(n_pairs):
        rows = slice(p * PAIR, (p + 1) * PAIR)
        o_parts = {h: [] for h in range(n_heads)}
        for c in range(2):
            r = slice(c * CHUNK, (c + 1) * CHUNK)
            for h in range(n_heads):
                t = (p, h)
                s_old = state[h] if chained else sin_ref[c, h]
                q_eff = q[t][r] - qk_uw[t][r, dv:]
                xs = _mm(jnp.concatenate([kg_uw[t, c][:, dv:], q_eff], axis=0), s_old)
                o_parts[h].append(qk_uw[t][r, :dv] + xs[dk:])
                s_new = s_old * gl[t][c] + (kg_uw[t, c][:, :dv] - xs[:dk])
                if chained:
                    state[h] = s_new
                else:
                    sout_ref[c, h] = s_new
        for h in range(n_heads):
            o = jnp.concatenate(o_parts[h], axis=0)
            zz = z_ref[0, rows, h * dv:(h + 1) * dv].astype(F32)
            y = (o * _rms_scale(o) * onorm_ref[...]) * _silu(zz)
            o_ref[0, rows, h * dv:(h + 1) * dv] = y.astype(o_ref.dtype)
    if chained:
        for h in range(n_heads):
            sout_ref[0, h] = state[h]


def _delta(qkv, gcol, z, onorm, s0, *, n_heads):
    n_seq, L, qkv_w = qkv.shape
    v_w = z.shape[-1]
    assert qkv_w == 3 * n_heads * LANES and v_w == n_heads * LANES
    if L % PAIR == 0:
        chained, n_grp, Lg, per_grp = True, n_seq, L, 1
    else:
        assert L == CHUNK and n_seq % 2 == 0
        chained, n_grp, Lg, per_grp = False, n_seq // 2, PAIR, 2
        qkv, gcol, z = (t.reshape(n_grp, PAIR, t.shape[-1]) for t in (qkv, gcol, z))
    tb = min(512, Lg)
    assert Lg % tb == 0
    kern = functools.partial(_delta_kernel, tb=tb, n_heads=n_heads, chained=chained)
    blk = lambda w: pl.BlockSpec((1, tb, w), lambda s, j: (s, j, 0))
    st = pl.BlockSpec((per_grp, n_heads, LANES, LANES), lambda s, j: (s, 0, 0, 0))
    o, s_new = pl.pallas_call(
        kern,
        grid=(n_grp, Lg // tb),
        in_specs=[blk(qkv_w), blk(LANES), blk(v_w), _const_spec((1, LANES)), st],
        out_specs=[blk(v_w), st],
        out_shape=[jax.ShapeDtypeStruct((n_grp, Lg, v_w), BF16),
                   jax.ShapeDtypeStruct(s0.shape, F32)],
        compiler_params=pltpu.CompilerParams(dimension_semantics=("arbitrary", "arbitrary"),
                                             vmem_limit_bytes=VMEM_LIMIT),
        name="delta_rule",
    )(qkv, gcol, z, onorm, s0)
    return o.reshape(n_seq, L, v_w), s_new


def _out_mlp_kernel(*refs, n_a, tf):
    x_ref = refs[0]
    a_refs = refs[1:1 + n_a]
    w_ref, g_ref, wup_ref, wdn_ref, o_ref = refs[1 + n_a:]
    a = a_refs[0][...] if n_a == 1 else jnp.concatenate([a_ref[...] for a_ref in a_refs], axis=-1)
    x1 = x_ref[...] + jnp.dot(a, w_ref[...], preferred_element_type=F32)
    h = (x1 * _rms_scale(x1) * g_ref[...]).astype(BF16)
    acc = x1
    for f in range(wup_ref.shape[1] // tf):
        r = jnp.maximum(jnp.dot(h, wup_ref[:, f * tf:(f + 1) * tf], preferred_element_type=F32), 0.0)
        acc = acc + jnp.dot((r * r).astype(BF16), wdn_ref[f * tf:(f + 1) * tf, :], preferred_element_type=F32)
    o_ref[...] = acc


def _out_mlp(x, a_list, w_out, gain, w_up, w_down):
    shape = x.shape
    D = shape[-1]
    x2 = x.reshape(-1, D)
    n = x2.shape[0]
    a_list = [a.reshape(n, a.shape[-1]) for a in a_list]
    tm = min(512, n)
    assert n % tm == 0
    F = w_up.shape[1]
    tf = min(1024, F)
    assert F % tf == 0
    kern = functools.partial(_out_mlp_kernel, n_a=len(a_list), tf=tf)
    rows = lambda w: pl.BlockSpec((tm, w), lambda i: (i, 0))
    out = pl.pallas_call(
        kern,
        grid=(n // tm,),
        in_specs=([rows(D)] + [rows(a.shape[1]) for a in a_list] + [_const_spec(w_out.shape)]
                  + [_const_spec((1, D)), _const_spec(w_up.shape), _const_spec(w_down.shape)]),
        out_specs=rows(D),
        out_shape=jax.ShapeDtypeStruct((n, D), F32),
        compiler_params=pltpu.CompilerParams(dimension_semantics=("arbitrary",), vmem_limit_bytes=VMEM_LIMIT),
        name="out_mlp",
    )(x2, *a_list, w_out, gain, w_up, w_down)
    return out.reshape(shape)


def _odd_pre_kernel(x_ref, g_ref, w_ref, qn_ref, kn_ref, q_ref, k_ref, v_ref, *, n_heads):
    dh = LANES
    hd = n_heads * dh
    x = x_ref[0]
    h = (x * _rms_scale(x) * g_ref[...]).astype(BF16)
    q = jnp.dot(h, w_ref[:, 0:hd], preferred_element_type=F32)
    k = jnp.dot(h, w_ref[:, hd:2 * hd], preferred_element_type=F32)
    v = jnp.dot(h, w_ref[:, 2 * hd:3 * hd], preferred_element_type=F32)
    for hh in range(n_heads):
        cols = slice(hh * dh, (hh + 1) * dh)
        qh = q[:, cols]
        kh = k[:, cols]
        q_ref[0, hh] = (qh * _rms_scale(qh) * qn_ref[...] * (dh ** -0.5 * LOG2E)).astype(q_ref.dtype)
        k_ref[0, hh] = kh * _rms_scale(kh) * kn_ref[...]
        v_ref[0, hh] = v[:, cols]


def _odd_pre(x, gain, w_qkv, qn, kn, *, n_heads):
    B, L, D = x.shape
    tl = min(256, L)
    assert L % tl == 0 and w_qkv.shape[1] == 3 * n_heads * LANES
    kern = functools.partial(_odd_pre_kernel, n_heads=n_heads)
    head_major = pl.BlockSpec((1, n_heads, tl, LANES), lambda b, j: (b, 0, j, 0))
    hm_shape = (B, n_heads, L, LANES)
    return pl.pallas_call(
        kern,
        grid=(B, L // tl),
        in_specs=[pl.BlockSpec((1, tl, D), lambda b, j: (b, j, 0)), _const_spec((1, D)),
                  _const_spec(w_qkv.shape), _const_spec((1, LANES)), _const_spec((1, LANES))],
        out_specs=[head_major, head_major, head_major],
        out_shape=[jax.ShapeDtypeStruct(hm_shape, BF16), jax.ShapeDtypeStruct(hm_shape, F32),
                   jax.ShapeDtypeStruct(hm_shape, F32)],
        compiler_params=pltpu.CompilerParams(dimension_semantics=("arbitrary", "arbitrary"),
                                             vmem_limit_bytes=VMEM_LIMIT),
        name="odd_pre",
    )(x, gain, w_qkv, qn, kn)


def _suffix_ones(tk):
    j = lax.broadcasted_iota(jnp.int32, (2 * tk, tk), 0) & (tk - 1)
    s = lax.broadcasted_iota(jnp.int32, (2 * tk, tk), 1)
    return (j > s).astype(BF16)


def _sb_block(q, k_blk, v_blk, u, off, acc, below=None):
    z = _mm_nt(q, k_blk)
    t = jnp.maximum(z, 0.0) + jnp.log(1.0 + jnp.exp2(-jnp.abs(z))) * LOG2E
    if below is not None:
        t = jnp.where(below, t, 0.0)
    t_hi = t.astype(BF16)
    t_lo = (t - t_hi.astype(F32)).astype(BF16)
    later = jnp.dot(jnp.concatenate([t_hi, t_lo], axis=1), u, preferred_element_type=F32)
    wts = jnp.exp2(z - t - (later + off))
    return later[:, 0:1] + t[:, 0:1], acc + _mm(wts, v_blk)


def _sb_kernel(*refs, tq, tkp, n_past, n_q, hb):
    if n_past:
        q_ref, kn_ref, vn_ref, kp_ref, vp_ref, o_ref = refs
    else:
        q_ref, kn_ref, vn_ref, o_ref = refs
    i = pl.program_id(2)
    u_self = _suffix_ones(tq)
    u_past = _suffix_ones(tkp) if n_past else None
    row = lax.broadcasted_iota(jnp.int32, (tq, tq), 0)
    col = lax.broadcasted_iota(jnp.int32, (tq, tq), 1)
    below = col < row
    not_below = jnp.where(below, 0.0, MASKED)
    start = pl.multiple_of(i * tq, tq)

    walks = []
    for hh in range(hb):
        q = q_ref[0, hh]
        acc = jnp.zeros((tq, LANES), F32)
        spent, acc = _sb_block(q, kn_ref[0, hh, pl.ds(start, tq), :], vn_ref[0, hh, pl.ds(start, tq), :],
                               u_self, not_below, acc, below)
        first_self, first_past = i - 1, n_past - 1
        if n_q > 1:
            prev = pl.multiple_of(jnp.maximum(i - 1, 0) * tq, tq)
            has_prev = jnp.full((tq, 1), i, jnp.int32) >= 1
            own, acc = _sb_block(q, kn_ref[0, hh, pl.ds(prev, tq), :], vn_ref[0, hh, pl.ds(prev, tq), :],
                                 u_self, jnp.where(has_prev, spent, MASKED), acc)
            spent = spent + jnp.where(has_prev, own, 0.0)
            first_self = i - 2
        elif n_past:
            lastp = (n_past - 1) * tkp
            own, acc = _sb_block(q, kp_ref[0, hh, lastp:lastp + tkp, :], vp_ref[0, hh, lastp:lastp + tkp, :],
                                 u_past, spent, acc)
            spent = spent + own
            first_past = n_past - 2
        walks.append((q, spent, acc, first_self, first_past))

    def walk(q, k_ref, v_ref, hh, tk, u, first, spent, acc):
        def cond(st):
            return (st[0] >= 0) & (jnp.min(st[1]) < -LOG2_W_FLOOR)

        def body(st):
            j, spent, acc = st
            s0 = pl.multiple_of(j * tk, tk)
            own, acc = _sb_block(q, k_ref[0, hh, pl.ds(s0, tk), :], v_ref[0, hh, pl.ds(s0, tk), :], u, spent, acc)
            return j - 1, spent + own, acc

        _, spent, acc = lax.while_loop(cond, body, (first, spent, acc))
        return spent, acc

    for hh, (_, _, acc, _, _) in enumerate(walks):
        o_ref[0, :, hh * LANES:(hh + 1) * LANES] = acc.astype(o_ref.dtype)
    if n_q == 1 and n_past <= 1:
        return

    least = functools.reduce(jnp.minimum, [w[1] for w in walks])

    @pl.when(jnp.min(least) < -LOG2_W_FLOOR)
    def _():
        for hh, (q, spent, acc, first_self, first_past) in enumerate(walks):
            if n_q > 1:
                spent, acc = walk(q, kn_ref, vn_ref, hh, tq, u_self, first_self, spent, acc)
            if n_past:
                spent, acc = walk(q, kp_ref, vp_ref, hh, tkp, u_past, jnp.int32(first_past), spent, acc)
            o_ref[0, :, hh * LANES:(hh + 1) * LANES] = acc.astype(o_ref.dtype)


def _stick_breaking(q, k_new, v_new, k_past, v_past):
    B, H, L, dh = q.shape
    assert dh == LANES
    tq = min(256, L)
    assert L % tq == 0
    n_q = L // tq
    assert tq & (tq - 1) == 0
    hb = H if n_q == 1 else 2
    assert H % hb == 0
    args = [q, k_new, v_new]
    full = lambda n: pl.BlockSpec((1, hb, n, LANES), lambda b, h, i: (b, h, 0, 0))
    in_specs = [pl.BlockSpec((1, hb, tq, LANES), lambda b, h, i: (b, h, i, 0)), full(L), full(L)]
    n_past, tkp = 0, 0
    if k_past is not None:
        P = k_past.shape[2]
        tkp = min(256, P)
        assert P % tkp == 0 and tkp & (tkp - 1) == 0
        n_past = P // tkp
        args += [k_past, v_past]
        in_specs += [full(P), full(P)]
    kern = functools.partial(_sb_kernel, tq=tq, tkp=tkp, n_past=n_past, n_q=n_q, hb=hb)
    return pl.pallas_call(
        kern,
        grid=(B, H // hb, n_q),
        in_specs=in_specs,
        out_specs=pl.BlockSpec((1, tq, hb * LANES), lambda b, h, i: (b, i, h)),
        out_shape=jax.ShapeDtypeStruct((B, L, H * LANES), BF16),
        compiler_params=pltpu.CompilerParams(dimension_semantics=("arbitrary", "arbitrary", "arbitrary"),
                                             vmem_limit_bytes=VMEM_LIMIT),
        name="stick_breaking",
    )(*args)


def _pad_lanes(v):
    return jnp.zeros((1, LANES), F32).at[0, :v.shape[0]].set(v.astype(F32))


def _even_layer(x, s0, qbuf, dbuf, wts, gain_mix, gain_mlp, w_up, w_down):
    (w_all, conv_w, alog_pad, dtb_pad, onorm, dw_w, dw_b, ln_g, ln_b, w_out, n_heads, qkv_w, v_w, c_b) = wts
    qkv, gcol, z, c, qlast, dlast = _even_pre(x, gain_mix, w_all, conv_w, alog_pad, dtb_pad, dw_w, dw_b, ln_g, ln_b,
                                              qbuf, dbuf, n_heads=n_heads, qkv_w=qkv_w, v_w=v_w, c_b=c_b)
    o, s_new = _delta(qkv, gcol, z, onorm, s0, n_heads=n_heads)
    y = _out_mlp(x, [o, c], w_out, gain_mlp, w_up, w_down)
    return y, s_new, qlast, dlast


def _odd_layer(x, k_past, v_past, wts, gain_mix, gain_mlp, w_up, w_down):
    w_qkv, qn, kn, w_out, n_heads = wts
    q, k, v = _odd_pre(x, gain_mix, w_qkv, qn, kn, n_heads=n_heads)
    o = _stick_breaking(q, k, v, k_past, v_past)
    y = _out_mlp(x, [o], w_out, gain_mlp, w_up, w_down)
    return y, k, v


def kernel(x_prompt, x_sample, state_delta, state_qkv_conv, state_dw_conv, cache_k, cache_v, norm_mix, norm_mlp, w_in_e, conv_qkv_e, a_log_e, dt_bias_e, onorm_e, dw_w_e, dw_b_e, ln_g_e, ln_b_e, w_out_e, w_qkv_o, qn_o, kn_o, w_out_o, w_up, w_down):
    depth = norm_mix.shape[0]
    bp = x_prompt.shape[0]
    xp, xs = x_prompt, x_sample
    pd, pq, pw, pk, pv = [], [], [], [], []
    sd, sq, sw, sk, sv = [], [], [], [], []
    row = lambda v: v.astype(F32).reshape(1, -1)
    for i in range(depth):
        gm, gl = row(norm_mix[i]), row(norm_mlp[i])
        wu, wd = w_up[i].astype(BF16), w_down[i].astype(BF16)
        if i % 2 == 0:
            e = i // 2
            n_heads = a_log_e.shape[1]
            qkv_w = conv_qkv_e.shape[2]
            c_b = dw_w_e.shape[2]
            v_w = n_heads * onorm_e.shape[1]
            w_in = w_in_e[e]
            ab0 = qkv_w + v_w
            ab = jnp.zeros((w_in.shape[0], LANES), w_in.dtype).at[:, :2 * n_heads].set(w_in[:, ab0:ab0 + 2 * n_heads])
            w_all = jnp.concatenate([w_in[:, :ab0], w_in[:, ab0 + 2 * n_heads:], ab], axis=1).astype(BF16)
            w_out = w_out_e[e].astype(BF16)
            wts = (w_all, conv_qkv_e[e].astype(F32), _pad_lanes(a_log_e[e]), _pad_lanes(dt_bias_e[e]),
                   row(onorm_e[e]), dw_w_e[e].astype(F32), row(dw_b_e[e]), row(ln_g_e[e]), row(ln_b_e[e]),
                   w_out, n_heads, qkv_w, v_w, c_b)
            kq, kd = conv_qkv_e.shape[1], dw_w_e.shape[1]
            s0 = jnp.zeros((bp,) + state_delta.shape[2:], F32)
            qb0 = jnp.zeros((bp, kq - 1, qkv_w), F32)
            db0 = jnp.zeros((bp, kd - 1, c_b), F32)
            xp, d1, q1, c1 = _even_layer(xp, s0, qb0, db0, wts, gm, gl, wu, wd)
            xs, d2, q2, c2 = _even_layer(xs, state_delta[e].astype(F32), state_qkv_conv[e].astype(F32),
                                         state_dw_conv[e].astype(F32), wts, gm, gl, wu, wd)
            pd.append(d1); pq.append(q1); pw.append(c1)
            sd.append(d2); sq.append(q2); sw.append(c2)
        else:
            o = i // 2
            n_heads = cache_k.shape[2]
            wts = (w_qkv_o[o].astype(BF16), row(qn_o[o]), row(kn_o[o]), w_out_o[o].astype(BF16), n_heads)
            xp, k1, v1 = _odd_layer(xp, None, None, wts, gm, gl, wu, wd)
            xs, k2, v2 = _odd_layer(xs, cache_k[o].astype(F32), cache_v[o].astype(F32), wts, gm, gl, wu, wd)
            pk.append(k1); pv.append(v1)
            sk.append(k2); sv.append(v2)
    return (xp, xs,
            jnp.stack(pd), jnp.stack(pq), jnp.stack(pw), jnp.stack(pk), jnp.stack(pv),
            jnp.stack(sd), jnp.stack(sq), jnp.stack(sw), jnp.stack(sk), jnp.stack(sv))
```

```python
import functools

import jax
import jax.numpy as jnp
from jax import lax
from jax.experimental import pallas as pl
from jax.experimental.pallas import tpu as pltpu

F32 = jnp.float32
BF16 = jnp.bfloat16
EPS = 1e-6
CHUNK = 64
LANES = 128
PAIR = 2 * CHUNK
QBUF_OFF = 8
DBUF_OFF = 32
DW_ROWS = 32
SUBLANES = 8
VMEM_LIMIT = 56 * 1024 * 1024
LOG2E = 1.4426950408889634
LOG2_W_FLOOR = -151.0
SB_TQ = 256
SB_PREV = 1
MASKED = 1e30


def _mm(a, b):
    return jnp.dot(a.astype(BF16), b.astype(BF16), preferred_element_type=F32)


def _mm_nt(a, b):
    return lax.dot_general(a.astype(BF16), b.astype(BF16), (((1,), (1,)), ((), ())),
                           preferred_element_type=F32)


def _sigmoid(x):
    return 0.5 * jnp.tanh(0.5 * x) + 0.5


def _silu(x):
    h = 0.5 * x
    return h + h * jnp.tanh(h)


def _softplus(x):
    return jnp.maximum(x, 0.0) + jnp.log1p(jnp.exp(-jnp.abs(x)))


def _rms_scale(x):
    return lax.rsqrt(jnp.mean(x * x, axis=-1, keepdims=True) + EPS)


def _const_spec(shape):
    nd = len(shape)
    return pl.BlockSpec(shape, lambda *_: (0,) * nd, pipeline_mode=pl.Buffered(1))


def _even_pre_kernel(x_ref, g_ref, w_ref, cw_ref, alog_ref, dtb_ref, dww_ref, dwb_ref, lng_ref, lnb_ref,
                     qbuf_ref, dbuf_ref,
                     qkv_ref, gcol_ref, z_ref, c_ref, qlast_ref, dlast_ref,
                     qext, uext, ushift, *, tl, qkv_w, v_w, c_b, n_heads):
    kq = cw_ref.shape[0]
    kd = dww_ref.shape[0]

    @pl.when(pl.program_id(1) == 0)
    def _():
        qext[QBUF_OFF - (kq - 1):QBUF_OFF, :] = qbuf_ref[0]
        uext[0:DBUF_OFF - (kd - 1), :] = jnp.zeros((DBUF_OFF - (kd - 1), uext.shape[1]), F32)
        uext[DBUF_OFF - (kd - 1):DBUF_OFF, :] = dbuf_ref[0]

    x = x_ref[0]
    h = (x * _rms_scale(x) * g_ref[...]).astype(BF16)

    glu0 = qkv_w + v_w
    qext[QBUF_OFF:QBUF_OFF + tl, :] = jnp.dot(h, w_ref[:, 0:qkv_w], preferred_element_type=F32)
    z_ref[0] = jnp.dot(h, w_ref[:, qkv_w:glu0], preferred_element_type=F32)
    ga = jnp.dot(h, w_ref[:, glu0:glu0 + c_b], preferred_element_type=F32)
    gb = jnp.dot(h, w_ref[:, glu0 + c_b:glu0 + 2 * c_b], preferred_element_type=F32)
    uext[DBUF_OFF:DBUF_OFF + tl, :] = ga * _sigmoid(gb)
    ab = jnp.dot(h, w_ref[:, glu0 + 2 * c_b:glu0 + 2 * c_b + LANES], preferred_element_type=F32)

    for s in range(qkv_w // LANES):
        cols = slice(s * LANES, (s + 1) * LANES)
        y = None
        for j in range(kq):
            r0 = QBUF_OFF - (kq - 1) + j
            t = cw_ref[j:j + 1, cols] * qext[r0:r0 + tl, cols]
            y = t if y is None else y + t
        y = _silu(y)
        if s < 2 * n_heads:
            y = y * lax.rsqrt(jnp.sum(y * y, axis=-1, keepdims=True) + EPS)
        qkv_ref[0, :, cols] = y
    qlast = qext[QBUF_OFF + tl - (kq - 1):QBUF_OFF + tl, :]
    qlast_ref[0] = qlast
    qext[QBUF_OFF - (kq - 1):QBUF_OFF, :] = qlast

    span = tl + DBUF_OFF - SUBLANES
    for b in range(1, SUBLANES):
        ushift[b - 1, 0:span, :] = uext[b:b + span, :]
    for r in range(tl // DW_ROWS):
        acc = None
        for j in range(kd):
            a, b = divmod(DBUF_OFF - (kd - 1) + j, SUBLANES)
            r0 = a * SUBLANES + r * DW_ROWS
            win = uext[r0:r0 + DW_ROWS, :] if b == 0 else ushift[b - 1, r0:r0 + DW_ROWS, :]
            t = dww_ref[j:j + 1, :] * win
            acc = t if acc is None else acc + t
        cpre = acc + dwb_ref[...]
        mu = jnp.mean(cpre, axis=-1, keepdims=True)
        xc = cpre - mu
        var = jnp.mean(xc * xc, axis=-1, keepdims=True)
        y = xc * lax.rsqrt(var + EPS) * lng_ref[...] + lnb_ref[...]
        c_ref[0, r * DW_ROWS:(r + 1) * DW_ROWS, :] = _silu(y).astype(c_ref.dtype)
    dlast = uext[DBUF_OFF + tl - (kd - 1):DBUF_OFF + tl, :]
    dlast_ref[0] = dlast
    uext[DBUF_OFF - (kd - 1):DBUF_OFF, :] = dlast

    lane = lax.broadcasted_iota(jnp.int32, (tl, LANES), 1)
    g = -jnp.exp(alog_ref[...]) * _softplus(ab + dtb_ref[...])
    g = jnp.where(lane < n_heads, g, 0.0)
    beta = _sigmoid(ab)
    row = lax.broadcasted_iota(jnp.int32, (tl, tl), 0)
    col = lax.broadcasted_iota(jnp.int32, (tl, tl), 1)
    tri = (((row ^ col) < CHUNK) & (col <= row)).astype(BF16)
    g_hi = g.astype(BF16)
    g_r1 = g - g_hi.astype(F32)
    g_mid = g_r1.astype(BF16)
    g_lo = (g_r1 - g_mid.astype(F32)).astype(BF16)
    gcum = (jnp.dot(tri, g_hi, preferred_element_type=F32)
            + jnp.dot(tri, g_mid, preferred_element_type=F32)
            + jnp.dot(tri, g_lo, preferred_element_type=F32))
    gcol_ref[0] = jnp.where(lane < n_heads, gcum, jnp.where(lane < 2 * n_heads, beta, 0.0))


def _even_pre(x, gain, w_all, conv_w, alog_pad, dtb_pad, dw_w, dw_b, ln_g, ln_b, qbuf, dbuf, *, n_heads, qkv_w, v_w, c_b):
    n_seq, L, D = x.shape
    tl = min(256, L)
    assert L % tl == 0 and tl % DW_ROWS == 0 and tl % CHUNK == 0
    kq, kd = conv_w.shape[0], dw_w.shape[0]
    assert kq - 1 <= QBUF_OFF and kd - 1 <= DBUF_OFF and tl >= kd - 1
    nw = w_all.shape[1]
    kern = functools.partial(_even_pre_kernel, tl=tl, qkv_w=qkv_w, v_w=v_w, c_b=c_b, n_heads=n_heads)
    seq_blk = lambda w: pl.BlockSpec((1, tl, w), lambda s, j: (s, j, 0))
    per_seq = lambda r, w: pl.BlockSpec((1, r, w), lambda s, j: (s, 0, 0))
    return pl.pallas_call(
        kern,
        grid=(n_seq, L // tl),
        in_specs=[seq_blk(D), _const_spec((1, D)), _const_spec((D, nw)), _const_spec((kq, qkv_w)),
                  _const_spec((1, LANES)), _const_spec((1, LANES)), _const_spec((kd, c_b)),
                  _const_spec((1, c_b)), _const_spec((1, c_b)), _const_spec((1, c_b)),
                  per_seq(kq - 1, qkv_w), per_seq(kd - 1, c_b)],
        out_specs=[seq_blk(qkv_w), seq_blk(LANES), seq_blk(v_w), seq_blk(c_b),
                   per_seq(kq - 1, qkv_w), per_seq(kd - 1, c_b)],
        out_shape=[jax.ShapeDtypeStruct((n_seq, L, qkv_w), F32),
                   jax.ShapeDtypeStruct((n_seq, L, LANES), F32),
                   jax.ShapeDtypeStruct((n_seq, L, v_w), F32),
                   jax.ShapeDtypeStruct((n_seq, L, c_b), BF16),
                   jax.ShapeDtypeStruct((n_seq, kq - 1, qkv_w), F32),
                   jax.ShapeDtypeStruct((n_seq, kd - 1, c_b), F32)],
        scratch_shapes=[pltpu.VMEM((QBUF_OFF + tl, qkv_w), F32), pltpu.VMEM((DBUF_OFF + tl, c_b), F32),
                        pltpu.VMEM((SUBLANES - 1, DBUF_OFF + tl, c_b), F32)],
        compiler_params=pltpu.CompilerParams(dimension_semantics=("arbitrary", "arbitrary"),
                                             vmem_limit_bytes=VMEM_LIMIT),
        name="even_pre",
    )(x, gain, w_all, conv_w, alog_pad, dtb_pad, dw_w, dw_b, ln_g, ln_b, qbuf, dbuf)


def _unit_lower_inverses(a_list, row, col):
    x = row ^ col
    eye = (row == col).astype(F32)
    a8 = [jnp.where(x < 8, a, 0.0) for a in a_list]
    a8_2 = [_mm(t, t) for t in a8]
    p = [eye - t for t in a8]
    a8_4 = [_mm(t, t) for t in a8_2]
    p = [t + _mm(t, sq) for t, sq in zip(p, a8_2)]
    d = [t + _mm(t, sq) for t, sq in zip(p, a8_4)]
    s = 8
    while s < CHUNK:
        off = [jnp.where((x >= s) & (x < 2 * s), a, 0.0) for a in a_list]
        od = [_mm(o, t) for o, t in zip(off, d)]
        d = [t - _mm(t, u) for t, u in zip(d, od)]
        s *= 2
    return d


def _delta_kernel(qkv_ref, gcol_ref, z_ref, onorm_ref, sin_ref, o_ref, sout_ref, *, tb, n_heads, chained):
    dk = dv = LANES
    qk_w = n_heads * dk
    n_pairs = tb // PAIR
    if chained:
        @pl.when(pl.program_id(1) == 0)
        def _():
            sout_ref[...] = sin_ref[...]

    row = lax.broadcasted_iota(jnp.int32, (PAIR, PAIR), 0)
    col = lax.broadcasted_iota(jnp.int32, (PAIR, PAIR), 1)
    same = (row ^ col) < CHUNK
    causal = same & (col <= row)
    strict = same & (col < row)
    in_chunk = [(col >= c * CHUNK) & (col < (c + 1) * CHUNK) for c in range(2)]

    tiles = [(p, h) for p in range(n_pairs) for h in range(n_heads)]
    q, k, kb, vb, gc, a, qk = {}, {}, {}, {}, {}, {}, {}
    for p, h in tiles:
        rows = slice(p * PAIR, (p + 1) * PAIR)
        q[p, h] = qkv_ref[0, rows, h * dk:(h + 1) * dk] * (dk ** -0.5)
        k[p, h] = qkv_ref[0, rows, qk_w + h * dk:qk_w + (h + 1) * dk]
        gc[p, h] = jnp.broadcast_to(gcol_ref[0, rows, h:h + 1], (PAIR, PAIR))
        bt = jnp.broadcast_to(gcol_ref[0, rows, n_heads + h:n_heads + h + 1], (PAIR, PAIR))
        kb[p, h] = k[p, h] * bt
        vb[p, h] = qkv_ref[0, rows, 2 * qk_w + h * dv:2 * qk_w + (h + 1) * dv] * bt
    for t in tiles:
        diff = gc[t] - gc[t].T
        decay = jnp.where(causal, jnp.exp(jnp.where(causal, diff, 0.0)), 0.0)
        a[t] = jnp.where(strict, _mm_nt(kb[t], k[t]) * decay, 0.0)
        qk[t] = jnp.where(causal, _mm_nt(q[t], k[t]) * decay, 0.0)
    inv = dict(zip(tiles, _unit_lower_inverses([a[t] for t in tiles], row, col)))
    uw, kg_t, gl = {}, {}, {}
    for t in tiles:
        eg = jnp.exp(gc[t])
        uw[t] = _mm(inv[t], jnp.concatenate([vb[t], kb[t] * eg], axis=1))
        gl_rows = [jnp.broadcast_to(gc[t][(c + 1) * CHUNK - 1:(c + 1) * CHUNK, :], (PAIR, PAIR)) for c in range(2)]
        gl[t] = [jnp.exp(g) for g in gl_rows]
        gcl = jnp.where(row < CHUNK, gl_rows[0], gl_rows[1])
        kg_t[t] = (k[t] * jnp.exp(gcl - gc[t])).T
        q[t] = q[t] * eg
    qk_uw = {t: _mm(qk[t], uw[t]) for t in tiles}
    kg_uw = {(t, c): _mm(jnp.where(in_chunk[c], kg_t[t], 0.0), uw[t])
             for t in tiles for c in range(2)}

    state = [sout_ref[0, h] for h in range(n_heads)] if chained else None
    for p in range(n_pairs):
        rows = slice(p * PAIR, (p + 1) * PAIR)
        o_parts = {h: [] for h in range(n_heads)}
        for c in range(2):
            r = slice(c * CHUNK, (c + 1) * CHUNK)
            for h in range(n_heads):
                t = (p, h)
                s_old = state[h] if chained else sin_ref[c, h]
                q_eff = q[t][r] - qk_uw[t][r, dv:]
                xs = _mm(jnp.concatenate([kg_uw[t, c][:, dv:], q_eff], axis=0), s_old)
                o_parts[h].append(qk_uw[t][r, :dv] + xs[dk:])
                s_new = s_old * gl[t][c] + (kg_uw[t, c][:, :dv] - xs[:dk])
                if chained:
                    state[h] = s_new
                else:
                    sout_ref[c, h] = s_new
        for h in range(n_heads):
            o = jnp.concatenate(o_parts[h], axis=0)
            zz = z_ref[0, rows, h * dv:(h + 1) * dv].astype(F32)
            y = (o * _rms_scale(o) * onorm_ref[...]) * _silu(zz)
            o_ref[0, rows, h * dv:(h + 1) * dv] = y.astype(o_ref.dtype)
    if chained:
        for h in range(n_heads):
            sout_ref[0, h] = state[h]


def _delta(qkv, gcol, z, onorm, s0, *, n_heads):
    n_seq, L, qkv_w = qkv.shape
    v_w = z.shape[-1]
    assert qkv_w == 3 * n_heads * LANES and v_w == n_heads * LANES
    if L % PAIR == 0:
        chained, n_grp, Lg, per_grp = True, n_seq, L, 1
    else:
        assert L == CHUNK and n_seq % 2 == 0
        chained, n_grp, Lg, per_grp = False, n_seq // 2, PAIR, 2
        qkv, gcol, z = (t.reshape(n_grp, PAIR, t.shape[-1]) for t in (qkv, gcol, z))
    tb = min(512, Lg)
    assert Lg % tb == 0
    kern = functools.partial(_delta_kernel, tb=tb, n_heads=n_heads, chained=chained)
    blk = lambda w: pl.BlockSpec((1, tb, w), lambda s, j: (s, j, 0))
    st = pl.BlockSpec((per_grp, n_heads, LANES, LANES), lambda s, j: (s, 0, 0, 0))
    o, s_new = pl.pallas_call(
        kern,
        grid=(n_grp, Lg // tb),
        in_specs=[blk(qkv_w), blk(LANES), blk(v_w), _const_spec((1, LANES)), st],
        out_specs=[blk(v_w), st],
        out_shape=[jax.ShapeDtypeStruct((n_grp, Lg, v_w), BF16),
                   jax.ShapeDtypeStruct(s0.shape, F32)],
        compiler_params=pltpu.CompilerParams(dimension_semantics=("arbitrary", "arbitrary"),
                                             vmem_limit_bytes=VMEM_LIMIT),
        name="delta_rule",
    )(qkv, gcol, z, onorm, s0)
    return o.reshape(n_seq, L, v_w), s_new


def _out_mlp_kernel(*refs, n_a, tf):
    x_ref = refs[0]
    a_refs = refs[1:1 + n_a]
    w_ref, g_ref, wup_ref, wdn_ref, o_ref = refs[1 + n_a:]
    a = a_refs[0][...] if n_a == 1 else jnp.concatenate([a_ref[...] for a_ref in a_refs], axis=-1)
    x1 = x_ref[...] + jnp.dot(a, w_ref[...], preferred_element_type=F32)
    h = (x1 * _rms_scale(x1) * g_ref[...]).astype(BF16)
    acc = x1
    for f in range(wup_ref.shape[1] // tf):
        r = jnp.maximum(jnp.dot(h, wup_ref[:, f * tf:(f + 1) * tf], preferred_element_type=F32), 0.0)
        acc = acc + jnp.dot((r * r).astype(BF16), wdn_ref[f * tf:(f + 1) * tf, :], preferred_element_type=F32)
    o_ref[...] = acc


def _out_mlp(x, a_list, w_out, gain, w_up, w_down):
    shape = x.shape
    D = shape[-1]
    x2 = x.reshape(-1, D)
    n = x2.shape[0]
    a_list = [a.reshape(n, a.shape[-1]) for a in a_list]
    tm = min(512, n)
    assert n % tm == 0
    F = w_up.shape[1]
    tf = min(1024, F)
    assert F % tf == 0
    kern = functools.partial(_out_mlp_kernel, n_a=len(a_list), tf=tf)
    rows = lambda w: pl.BlockSpec((tm, w), lambda i: (i, 0))
    out = pl.pallas_call(
        kern,
        grid=(n // tm,),
        in_specs=([rows(D)] + [rows(a.shape[1]) for a in a_list] + [_const_spec(w_out.shape)]
                  + [_const_spec((1, D)), _const_spec(w_up.shape), _const_spec(w_down.shape)]),
        out_specs=rows(D),
        out_shape=jax.ShapeDtypeStruct((n, D), F32),
        compiler_params=pltpu.CompilerParams(dimension_semantics=("arbitrary",), vmem_limit_bytes=VMEM_LIMIT),
        name="out_mlp",
    )(x2, *a_list, w_out, gain, w_up, w_down)
    return out.reshape(shape)


def _odd_pre_kernel(x_ref, g_ref, w_ref, qn_ref, kn_ref, q_ref, k_ref, v_ref, *, n_heads):
    dh = LANES
    hd = n_heads * dh
    x = x_ref[0]
    h = (x * _rms_scale(x) * g_ref[...]).astype(BF16)
    q = jnp.dot(h, w_ref[:, 0:hd], preferred_element_type=F32)
    k = jnp.dot(h, w_ref[:, hd:2 * hd], preferred_element_type=F32)
    v = jnp.dot(h, w_ref[:, 2 * hd:3 * hd], preferred_element_type=F32)
    for hh in range(n_heads):
        cols = slice(hh * dh, (hh + 1) * dh)
        qh = q[:, cols]
        kh = k[:, cols]
        q_ref[0, hh] = (qh * _rms_scale(qh) * qn_ref[...] * (dh ** -0.5 * LOG2E)).astype(q_ref.dtype)
        k_ref[0, hh] = kh * _rms_scale(kh) * kn_ref[...]
        v_ref[0, hh] = v[:, cols]


def _odd_pre(x, gain, w_qkv, qn, kn, *, n_heads):
    B, L, D = x.shape
    tl = min(256, L)
    assert L % tl == 0 and w_qkv.shape[1] == 3 * n_heads * LANES
    kern = functools.partial(_odd_pre_kernel, n_heads=n_heads)
    head_major = pl.BlockSpec((1, n_heads, tl, LANES), lambda b, j: (b, 0, j, 0))
    hm_shape = (B, n_heads, L, LANES)
    return pl.pallas_call(
        kern,
        grid=(B, L // tl),
        in_specs=[pl.BlockSpec((1, tl, D), lambda b, j: (b, j, 0)), _const_spec((1, D)),
                  _const_spec(w_qkv.shape), _const_spec((1, LANES)), _const_spec((1, LANES))],
        out_specs=[head_major, head_major, head_major],
        out_shape=[jax.ShapeDtypeStruct(hm_shape, BF16), jax.ShapeDtypeStruct(hm_shape, F32),
                   jax.ShapeDtypeStruct(hm_shape, F32)],
        compiler_params=pltpu.CompilerParams(dimension_semantics=("arbitrary", "arbitrary"),
                                             vmem_limit_bytes=VMEM_LIMIT),
        name="odd_pre",
    )(x, gain, w_qkv, qn, kn)


def _suffix_ones(tk):
    j = lax.broadcasted_iota(jnp.int32, (2 * tk, tk), 0) & (tk - 1)
    s = lax.broadcasted_iota(jnp.int32, (2 * tk, tk), 1)
    return (j > s).astype(BF16)


def _sb_block(q, k_blk, v_blk, u, off, acc, below=None):
    z = _mm_nt(q, k_blk)
    t = jnp.maximum(z, 0.0) + jnp.log(1.0 + jnp.exp2(-jnp.abs(z))) * LOG2E
    if below is not None:
        t = jnp.where(below, t, 0.0)
    t_hi = t.astype(BF16)
    t_lo = (t - t_hi.astype(F32)).astype(BF16)
    later = jnp.dot(jnp.concatenate([t_hi, t_lo], axis=1), u, preferred_element_type=F32)
    wts = jnp.exp2(z - t - (later + off))
    return later[:, 0:1] + t[:, 0:1], acc + _mm(wts, v_blk)


def _sb_kernel(*refs, tq, tkp, n_past, n_q, hb, qb, n_prev):
    if n_past:
        q_ref, kn_ref, vn_ref, kp_ref, vp_ref, o_ref = refs
    else:
        q_ref, kn_ref, vn_ref, o_ref = refs
    u_self = _suffix_ones(tq)
    u_past = _suffix_ones(tkp) if n_past else None
    row = lax.broadcasted_iota(jnp.int32, (tq, tq), 0)
    col = lax.broadcasted_iota(jnp.int32, (tq, tq), 1)
    below = col < row
    not_below = jnp.where(below, 0.0, MASKED)

    walks = []
    for hh, w in [(hh, w) for hh in range(hb) for w in range(qb)]:
        i = pl.program_id(2) * qb + w
        start = pl.multiple_of(i * tq, tq)
        q = q_ref[0, hh, w * tq:(w + 1) * tq, :]
        acc = jnp.zeros((tq, LANES), F32)
        spent, acc = _sb_block(q, kn_ref[0, hh, pl.ds(start, tq), :], vn_ref[0, hh, pl.ds(start, tq), :],
                               u_self, not_below, acc, below)
        first_self, first_past = i - 1, n_past - 1
        for d in range(1, n_prev + 1):
            prev = pl.multiple_of(jnp.maximum(i - d, 0) * tq, tq)
            has_prev = jnp.full((tq, 1), i, jnp.int32) >= d
            own, acc = _sb_block(q, kn_ref[0, hh, pl.ds(prev, tq), :], vn_ref[0, hh, pl.ds(prev, tq), :],
                                 u_self, jnp.where(has_prev, spent, MASKED), acc)
            spent = spent + jnp.where(has_prev, own, 0.0)
            first_self = i - d - 1
        if n_q == 1 and n_past:
            lastp = (n_past - 1) * tkp
            own, acc = _sb_block(q, kp_ref[0, hh, lastp:lastp + tkp, :], vp_ref[0, hh, lastp:lastp + tkp, :],
                                 u_past, spent, acc)
            spent = spent + own
            first_past = n_past - 2
        walks.append((hh, w, q, spent, acc, first_self, first_past))

    def walk(q, k_ref, v_ref, hh, tk, u, first, spent, acc):
        def cond(st):
            return (st[0] >= 0) & (jnp.min(st[1]) < -LOG2_W_FLOOR)

        def body(st):
            j, spent, acc = st
            s0 = pl.multiple_of(j * tk, tk)
            own, acc = _sb_block(q, k_ref[0, hh, pl.ds(s0, tk), :], v_ref[0, hh, pl.ds(s0, tk), :], u, spent, acc)
            return j - 1, spent + own, acc

        _, spent, acc = lax.while_loop(cond, body, (first, spent, acc))
        return spent, acc

    for hh, w, _, _, acc, _, _ in walks:
        o_ref[0, w * tq:(w + 1) * tq, hh * LANES:(hh + 1) * LANES] = acc.astype(o_ref.dtype)
    if n_q == 1 and n_past <= 1:
        return

    least = functools.reduce(jnp.minimum, [wk[3] for wk in walks])

    @pl.when(jnp.min(least) < -LOG2_W_FLOOR)
    def _():
        for hh, w, q, spent, acc, first_self, first_past in walks:
            if n_q > 1:
                spent, acc = walk(q, kn_ref, vn_ref, hh, tq, u_self, first_self, spent, acc)
            if n_past:
                spent, acc = walk(q, kp_ref, vp_ref, hh, tkp, u_past, jnp.int32(first_past), spent, acc)
            o_ref[0, w * tq:(w + 1) * tq, hh * LANES:(hh + 1) * LANES] = acc.astype(o_ref.dtype)


def _stick_breaking(q, k_new, v_new, k_past, v_past):
    B, H, L, dh = q.shape
    assert dh == LANES
    tq = min(SB_TQ, L)
    assert L % tq == 0
    n_q = L // tq
    assert tq & (tq - 1) == 0
    hb, qb = (H, 1) if n_q == 1 else (2, 4)
    n_prev = 0 if n_q == 1 else SB_PREV
    assert H % hb == 0 and n_q % qb == 0
    args = [q, k_new, v_new]
    full = lambda n: pl.BlockSpec((1, hb, n, LANES), lambda b, h, i: (b, h, 0, 0))
    in_specs = [pl.BlockSpec((1, hb, tq * qb, LANES), lambda b, h, i: (b, h, i, 0)), full(L), full(L)]
    n_past, tkp = 0, 0
    if k_past is not None:
        P = k_past.shape[2]
        tkp = min(256, P)
        assert P % tkp == 0 and tkp & (tkp - 1) == 0
        n_past = P // tkp
        args += [k_past, v_past]
        in_specs += [full(P), full(P)]
    kern = functools.partial(_sb_kernel, tq=tq, tkp=tkp, n_past=n_past, n_q=n_q, hb=hb, qb=qb, n_prev=n_prev)
    return pl.pallas_call(
        kern,
        grid=(B, H // hb, n_q // qb),
        in_specs=in_specs,
        out_specs=pl.BlockSpec((1, tq * qb, hb * LANES), lambda b, h, i: (b, i, h)),
        out_shape=jax.ShapeDtypeStruct((B, L, H * LANES), BF16),
        compiler_params=pltpu.CompilerParams(dimension_semantics=("arbitrary", "arbitrary", "arbitrary"),
                                             vmem_limit_bytes=VMEM_LIMIT),
        name="stick_breaking",
    )(*args)


def _pad_lanes(v):
    return jnp.zeros((1, LANES), F32).at[0, :v.shape[0]].set(v.astype(F32))


def _even_layer(x, s0, qbuf, dbuf, wts, gain_mix, gain_mlp, w_up, w_down):
    (w_all, conv_w, alog_pad, dtb_pad, onorm, dw_w, dw_b, ln_g, ln_b, w_out, n_heads, qkv_w, v_w, c_b) = wts
    qkv, gcol, z, c, qlast, dlast = _even_pre(x, gain_mix, w_all, conv_w, alog_pad, dtb_pad, dw_w, dw_b, ln_g, ln_b,
                                              qbuf, dbuf, n_heads=n_heads, qkv_w=qkv_w, v_w=v_w, c_b=c_b)
    o, s_new = _delta(qkv, gcol, z, onorm, s0, n_heads=n_heads)
    y = _out_mlp(x, [o, c], w_out, gain_mlp, w_up, w_down)
    return y, s_new, qlast, dlast


def _odd_layer(x, k_past, v_past, wts, gain_mix, gain_mlp, w_up, w_down):
    w_qkv, qn, kn, w_out, n_heads = wts
    q, k, v = _odd_pre(x, gain_mix, w_qkv, qn, kn, n_heads=n_heads)
    o = _stick_breaking(q, k, v, k_past, v_past)
    y = _out_mlp(x, [o], w_out, gain_mlp, w_up, w_down)
    return y, k, v


def kernel(x_prompt, x_sample, state_delta, state_qkv_conv, state_dw_conv, cache_k, cache_v, norm_mix, norm_mlp, w_in_e, conv_qkv_e, a_log_e, dt_bias_e, onorm_e, dw_w_e, dw_b_e, ln_g_e, ln_b_e, w_out_e, w_qkv_o, qn_o, kn_o, w_out_o, w_up, w_down):
    depth = norm_mix.shape[0]
    bp = x_prompt.shape[0]
    xp, xs = x_prompt, x_sample
    pd, pq, pw, pk, pv = [], [], [], [], []
    sd, sq, sw, sk, sv = [], [], [], [], []
    row = lambda v: v.astype(F32).reshape(1, -1)
    for i in range(depth):
        gm, gl = row(norm_mix[i]), row(norm_mlp[i])
        wu, wd = w_up[i].astype(BF16), w_down[i].astype(BF16)
        if i % 2 == 0:
            e = i // 2
            n_heads = a_log_e.shape[1]
            qkv_w = conv_qkv_e.shape[2]
            c_b = dw_w_e.shape[2]
            v_w = n_heads * onorm_e.shape[1]
            w_in = w_in_e[e]
            ab0 = qkv_w + v_w
            ab = jnp.zeros((w_in.shape[0], LANES), w_in.dtype).at[:, :2 * n_heads].set(w_in[:, ab0:ab0 + 2 * n_heads])
            w_all = jnp.concatenate([w_in[:, :ab0], w_in[:, ab0 + 2 * n_heads:], ab], axis=1).astype(BF16)
            w_out = w_out_e[e].astype(BF16)
            wts = (w_all, conv_qkv_e[e].astype(F32), _pad_lanes(a_log_e[e]), _pad_lanes(dt_bias_e[e]),
                   row(onorm_e[e]), dw_w_e[e].astype(F32), row(dw_b_e[e]), row(ln_g_e[e]), row(ln_b_e[e]),
                   w_out, n_heads, qkv_w, v_w, c_b)
            kq, kd = conv_qkv_e.shape[1], dw_w_e.shape[1]
            s0 = jnp.zeros((bp,) + state_delta.shape[2:], F32)
            qb0 = jnp.zeros((bp, kq - 1, qkv_w), F32)
            db0 = jnp.zeros((bp, kd - 1, c_b), F32)
            xp, d1, q1, c1 = _even_layer(xp, s0, qb0, db0, wts, gm, gl, wu, wd)
            xs, d2, q2, c2 = _even_layer(xs, state_delta[e].astype(F32), state_qkv_conv[e].astype(F32),
                                         state_dw_conv[e].astype(F32), wts, gm, gl, wu, wd)
            pd.append(d1); pq.append(q1); pw.append(c1)
            sd.append(d2); sq.append(q2); sw.append(c2)
        else:
            o = i // 2
            n_heads = cache_k.shape[2]
            wts = (w_qkv_o[o].astype(BF16), row(qn_o[o]), row(kn_o[o]), w_out_o[o].astype(BF16), n_heads)
            xp, k1, v1 = _odd_layer(xp, None, None, wts, gm, gl, wu, wd)
            xs, k2, v2 = _odd_layer(xs, cache_k[o].astype(F32), cache_v[o].astype(F32), wts, gm, gl, wu, wd)
            pk.append(k1); pv.append(v1)
            sk.append(k2); sv.append(v2)
    return (xp, xs,
            jnp.stack(pd), jnp.stack(pq), jnp.stack(pw), jnp.stack(pk), jnp.stack(pv),
            jnp.stack(sd), jnp.stack(sq), jnp.stack(sw), jnp.stack(sk), jnp.stack(sv))
```

```python
import functools

import jax
import jax.numpy as jnp
from jax import lax
from jax.experimental import pallas as pl
from jax.experimental.pallas import tpu as pltpu

F32 = jnp.float32
BF16 = jnp.bfloat16
EPS = 1e-6
CHUNK = 64
LANES = 128
PAIR = 2 * CHUNK
QBUF_OFF = 8
DBUF_OFF = 32
DW_ROWS = 32
SUBLANES = 8
VMEM_LIMIT = 56 * 1024 * 1024
LOG2E = 1.4426950408889634
LOG2_W_FLOOR = -151.0
SB_TQ = 256
SB_PREV = 1
MASKED = 1e30


def _mm(a, b):
    return jnp.dot(a.astype(BF16), b.astype(BF16), preferred_element_type=F32)


def _mm_nt(a, b):
    return lax.dot_general(a.astype(BF16), b.astype(BF16), (((1,), (1,)), ((), ())),
                           preferred_element_type=F32)


def _sigmoid(x):
    return 0.5 * jnp.tanh(0.5 * x) + 0.5


def _silu(x):
    h = 0.5 * x
    return h + h * jnp.tanh(h)


def _softplus(x):
    return jnp.maximum(x, 0.0) + jnp.log1p(jnp.exp(-jnp.abs(x)))


def _rms_scale(x):
    return lax.rsqrt(jnp.mean(x * x, axis=-1, keepdims=True) + EPS)


def _const_spec(shape):
    nd = len(shape)
    return pl.BlockSpec(shape, lambda *_: (0,) * nd, pipeline_mode=pl.Buffered(1))


def _even_pre_kernel(x_ref, g_ref, w_ref, cw_ref, alog_ref, dtb_ref, dww_ref, dwb_ref, lng_ref, lnb_ref,
                     qbuf_ref, dbuf_ref,
                     qkv_ref, gcol_ref, z_ref, c_ref, qlast_ref, dlast_ref,
                     qext, uext, ushift, *, tl, qkv_w, v_w, c_b, n_heads):
    kq = cw_ref.shape[0] // SUBLANES
    kd = dww_ref.shape[0] // SUBLANES

    @pl.when(pl.program_id(1) == 0)
    def _():
        qext[QBUF_OFF - (kq - 1):QBUF_OFF, :] = qbuf_ref[0]
        uext[0:DBUF_OFF - (kd - 1), :] = jnp.zeros((DBUF_OFF - (kd - 1), uext.shape[1]), F32)
        uext[DBUF_OFF - (kd - 1):DBUF_OFF, :] = dbuf_ref[0]

    x = x_ref[0]
    h = (x * _rms_scale(x) * g_ref[...]).astype(BF16)

    glu0 = qkv_w + v_w
    qext[QBUF_OFF:QBUF_OFF + tl, :] = jnp.dot(h, w_ref[:, 0:qkv_w], preferred_element_type=F32)
    z_ref[0] = jnp.dot(h, w_ref[:, qkv_w:glu0], preferred_element_type=F32)
    ga = jnp.dot(h, w_ref[:, glu0:glu0 + c_b], preferred_element_type=F32)
    gb = jnp.dot(h, w_ref[:, glu0 + c_b:glu0 + 2 * c_b], preferred_element_type=F32)
    uext[DBUF_OFF:DBUF_OFF + tl, :] = ga * _sigmoid(gb)
    ab = jnp.dot(h, w_ref[:, glu0 + 2 * c_b:glu0 + 2 * c_b + LANES], preferred_element_type=F32)

    for s in range(qkv_w // LANES):
        cols = slice(s * LANES, (s + 1) * LANES)
        y = None
        for j in range(kq):
            r0 = QBUF_OFF - (kq - 1) + j
            wj = cw_ref[j * SUBLANES:(j + 1) * SUBLANES, cols]
            t = wj[None] * qext[r0:r0 + tl, cols].reshape(tl // SUBLANES, SUBLANES, LANES)
            y = t if y is None else y + t
        y = _silu(y.reshape(tl, LANES))
        if s < 2 * n_heads:
            y = y * lax.rsqrt(jnp.sum(y * y, axis=-1, keepdims=True) + EPS)
        qkv_ref[0, :, cols] = y
    qlast = qext[QBUF_OFF + tl - (kq - 1):QBUF_OFF + tl, :]
    qlast_ref[0] = qlast
    qext[QBUF_OFF - (kq - 1):QBUF_OFF, :] = qlast

    span = tl + DBUF_OFF - SUBLANES
    for b in range(1, SUBLANES):
        ushift[b - 1, 0:span, :] = uext[b:b + span, :]
    for r in range(tl // DW_ROWS):
        acc = None
        for j in range(kd):
            a, b = divmod(DBUF_OFF - (kd - 1) + j, SUBLANES)
            r0 = a * SUBLANES + r * DW_ROWS
            win = uext[r0:r0 + DW_ROWS, :] if b == 0 else ushift[b - 1, r0:r0 + DW_ROWS, :]
            wj = dww_ref[j * SUBLANES:(j + 1) * SUBLANES, :]
            t = wj[None] * win.reshape(DW_ROWS // SUBLANES, SUBLANES, c_b)
            acc = t if acc is None else acc + t
        cpre = acc.reshape(DW_ROWS, c_b) + dwb_ref[...]
        mu = jnp.mean(cpre, axis=-1, keepdims=True)
        xc = cpre - mu
        var = jnp.mean(xc * xc, axis=-1, keepdims=True)
        y = xc * lax.rsqrt(var + EPS) * lng_ref[...] + lnb_ref[...]
        c_ref[0, r * DW_ROWS:(r + 1) * DW_ROWS, :] = _silu(y).astype(c_ref.dtype)
    dlast = uext[DBUF_OFF + tl - (kd - 1):DBUF_OFF + tl, :]
    dlast_ref[0] = dlast
    uext[DBUF_OFF - (kd - 1):DBUF_OFF, :] = dlast

    lane = lax.broadcasted_iota(jnp.int32, (tl, LANES), 1)
    g = -jnp.exp(alog_ref[...]) * _softplus(ab + dtb_ref[...])
    g = jnp.where(lane < n_heads, g, 0.0)
    beta = _sigmoid(ab)
    row = lax.broadcasted_iota(jnp.int32, (tl, tl), 0)
    col = lax.broadcasted_iota(jnp.int32, (tl, tl), 1)
    tri = (((row ^ col) < CHUNK) & (col <= row)).astype(BF16)
    g_hi = g.astype(BF16)
    g_r1 = g - g_hi.astype(F32)
    g_mid = g_r1.astype(BF16)
    g_lo = (g_r1 - g_mid.astype(F32)).astype(BF16)
    gcum = (jnp.dot(tri, g_hi, preferred_element_type=F32)
            + jnp.dot(tri, g_mid, preferred_element_type=F32)
            + jnp.dot(tri, g_lo, preferred_element_type=F32))
    gcol_ref[0] = jnp.where(lane < n_heads, gcum, jnp.where(lane < 2 * n_heads, beta, 0.0))


def _even_pre(x, gain, w_all, conv_w, alog_pad, dtb_pad, dw_w, dw_b, ln_g, ln_b, qbuf, dbuf, *, n_heads, qkv_w, v_w, c_b):
    n_seq, L, D = x.shape
    tl = min(256, L)
    assert L % tl == 0 and tl % DW_ROWS == 0 and tl % CHUNK == 0
    kq, kd = conv_w.shape[0], dw_w.shape[0]
    assert kq - 1 <= QBUF_OFF and kd - 1 <= DBUF_OFF and tl >= kd - 1
    conv_w = jnp.repeat(conv_w, SUBLANES, axis=0)
    dw_w = jnp.repeat(dw_w, SUBLANES, axis=0)
    nw = w_all.shape[1]
    kern = functools.partial(_even_pre_kernel, tl=tl, qkv_w=qkv_w, v_w=v_w, c_b=c_b, n_heads=n_heads)
    seq_blk = lambda w: pl.BlockSpec((1, tl, w), lambda s, j: (s, j, 0))
    per_seq = lambda r, w: pl.BlockSpec((1, r, w), lambda s, j: (s, 0, 0))
    return pl.pallas_call(
        kern,
        grid=(n_seq, L // tl),
        in_specs=[seq_blk(D), _const_spec((1, D)), _const_spec((D, nw)), _const_spec(conv_w.shape),
                  _const_spec((1, LANES)), _const_spec((1, LANES)), _const_spec(dw_w.shape),
                  _const_spec((1, c_b)), _const_spec((1, c_b)), _const_spec((1, c_b)),
                  per_seq(kq - 1, qkv_w), per_seq(kd - 1, c_b)],
        out_specs=[seq_blk(qkv_w), seq_blk(LANES), seq_blk(v_w), seq_blk(c_b),
                   per_seq(kq - 1, qkv_w), per_seq(kd - 1, c_b)],
        out_shape=[jax.ShapeDtypeStruct((n_seq, L, qkv_w), F32),
                   jax.ShapeDtypeStruct((n_seq, L, LANES), F32),
                   jax.ShapeDtypeStruct((n_seq, L, v_w), F32),
                   jax.ShapeDtypeStruct((n_seq, L, c_b), BF16),
                   jax.ShapeDtypeStruct((n_seq, kq - 1, qkv_w), F32),
                   jax.ShapeDtypeStruct((n_seq, kd - 1, c_b), F32)],
        scratch_shapes=[pltpu.VMEM((QBUF_OFF + tl, qkv_w), F32), pltpu.VMEM((DBUF_OFF + tl, c_b), F32),
                        pltpu.VMEM((SUBLANES - 1, DBUF_OFF + tl, c_b), F32)],
        compiler_params=pltpu.CompilerParams(dimension_semantics=("arbitrary", "arbitrary"),
                                             vmem_limit_bytes=VMEM_LIMIT),
        name="even_pre",
    )(x, gain, w_all, conv_w, alog_pad, dtb_pad, dw_w, dw_b, ln_g, ln_b, qbuf, dbuf)


def _unit_lower_inverses(a_list, row, col):
    x = row ^ col
    eye = (row == col).astype(F32)
    a8 = [jnp.where(x < 8, a, 0.0) for a in a_list]
    a8_2 = [_mm(t, t) for t in a8]
    p = [eye - t for t in a8]
    a8_4 = [_mm(t, t) for t in a8_2]
    p = [t + _mm(t, sq) for t, sq in zip(p, a8_2)]
    d = [t + _mm(t, sq) for t, sq in zip(p, a8_4)]
    s = 8
    while s < CHUNK:
        off = [jnp.where((x >= s) & (x < 2 * s), a, 0.0) for a in a_list]
        od = [_mm(o, t) for o, t in zip(off, d)]
        d = [t - _mm(t, u) for t, u in zip(d, od)]
        s *= 2
    return d


def _delta_kernel(qkv_ref, gcol_ref, z_ref, onorm_ref, sin_ref, o_ref, sout_ref, *, tb, n_heads, chained):
    dk = dv = LANES
    qk_w = n_heads * dk
    n_pairs = tb // PAIR
    if chained:
        @pl.when(pl.program_id(1) == 0)
        def _():
            sout_ref[...] = sin_ref[...]

    row = lax.broadcasted_iota(jnp.int32, (PAIR, PAIR), 0)
    col = lax.broadcasted_iota(jnp.int32, (PAIR, PAIR), 1)
    same = (row ^ col) < CHUNK
    causal = same & (col <= row)
    strict = same & (col < row)
    in_chunk = [(col >= c * CHUNK) & (col < (c + 1) * CHUNK) for c in range(2)]

    tiles = [(p, h) for p in range(n_pairs) for h in range(n_heads)]
    q, k, kb, vb, gc, a, qk = {}, {}, {}, {}, {}, {}, {}
    for p, h in tiles:
        rows = slice(p * PAIR, (p + 1) * PAIR)
        q[p, h] = qkv_ref[0, rows, h * dk:(h + 1) * dk] * (dk ** -0.5)
        k[p, h] = qkv_ref[0, rows, qk_w + h * dk:qk_w + (h + 1) * dk]
        gc[p, h] = jnp.broadcast_to(gcol_ref[0, rows, h:h + 1], (PAIR, PAIR))
        bt = jnp.broadcast_to(gcol_ref[0, rows, n_heads + h:n_heads + h + 1], (PAIR, PAIR))
        kb[p, h] = k[p, h] * bt
        vb[p, h] = qkv_ref[0, rows, 2 * qk_w + h * dv:2 * qk_w + (h + 1) * dv] * bt
    for t in tiles:
        diff = gc[t] - gc[t].T
        decay = jnp.where(causal, jnp.exp(jnp.where(causal, diff, 0.0)), 0.0)
        a[t] = jnp.where(strict, _mm_nt(kb[t], k[t]) * decay, 0.0)
        qk[t] = jnp.where(causal, _mm_nt(q[t], k[t]) * decay, 0.0)
    inv = dict(zip(tiles, _unit_lower_inverses([a[t] for t in tiles], row, col)))
    uw, kg_t, gl = {}, {}, {}
    for t in tiles:
        eg = jnp.exp(gc[t])
        uw[t] = _mm(inv[t], jnp.concatenate([vb[t], kb[t] * eg], axis=1))
        gl_rows = [jnp.broadcast_to(gc[t][(c + 1) * CHUNK - 1:(c + 1) * CHUNK, :], (PAIR, PAIR)) for c in range(2)]
        gl[t] = [jnp.exp(g) for g in gl_rows]
        gcl = jnp.where(row < CHUNK, gl_rows[0], gl_rows[1])
        kg_t[t] = (k[t] * jnp.exp(gcl - gc[t])).T
        q[t] = q[t] * eg
    qk_uw = {t: _mm(qk[t], uw[t]) for t in tiles}
    kg_uw = {(t, c): _mm(jnp.where(in_chunk[c], kg_t[t], 0.0), uw[t])
             for t in tiles for c in range(2)}

    state = [sout_ref[0, h] for h in range(n_heads)] if chained else None
    for p in range(n_pairs):
        rows = slice(p * PAIR, (p + 1) * PAIR)
        o_parts = {h: [] for h in range(n_heads)}
        for c in range(2):
            r = slice(c * CHUNK, (c + 1) * CHUNK)
            for h in range(n_heads):
                t = (p, h)
                s_old = state[h] if chained else sin_ref[c, h]
                q_eff = q[t][r] - qk_uw[t][r, dv:]
                xs = _mm(jnp.concatenate([kg_uw[t, c][:, dv:], q_eff], axis=0), s_old)
                o_parts[h].append(qk_uw[t][r, :dv] + xs[dk:])
                s_new = s_old * gl[t][c] + (kg_uw[t, c][:, :dv] - xs[:dk])
                if chained:
                    state[h] = s_new
                else:
                    sout_ref[c, h] = s_new
        for h in range(n_heads):
            o = jnp.concatenate(o_parts[h], axis=0)
            zz = z_ref[0, rows, h * dv:(h + 1) * dv].astype(F32)
            y = (o * _rms_scale(o) * onorm_ref[...]) * _silu(zz)
            o_ref[0, rows, h * dv:(h + 1) * dv] = y.astype(o_ref.dtype)
    if chained:
        for h in range(n_heads):
            sout_ref[0, h] = state[h]


def _delta(qkv, gcol, z, onorm, s0, *, n_heads):
    n_seq, L, qkv_w = qkv.shape
    v_w = z.shape[-1]
    assert qkv_w == 3 * n_heads * LANES and v_w == n_heads * LANES
    if L % PAIR == 0:
        chained, n_grp, Lg, per_grp = True, n_seq, L, 1
    else:
        assert L == CHUNK and n_seq % 2 == 0
        chained, n_grp, Lg, per_grp = False, n_seq // 2, PAIR, 2
        qkv, gcol, z = (t.reshape(n_grp, PAIR, t.shape[-1]) for t in (qkv, gcol, z))
    tb = min(512, Lg)
    assert Lg % tb == 0
    kern = functools.partial(_delta_kernel, tb=tb, n_heads=n_heads, chained=chained)
    blk = lambda w: pl.BlockSpec((1, tb, w), lambda s, j: (s, j, 0))
    st = pl.BlockSpec((per_grp, n_heads, LANES, LANES), lambda s, j: (s, 0, 0, 0))
    o, s_new = pl.pallas_call(
        kern,
        grid=(n_grp, Lg // tb),
        in_specs=[blk(qkv_w), blk(LANES), blk(v_w), _const_spec((1, LANES)), st],
        out_specs=[blk(v_w), st],
        out_shape=[jax.ShapeDtypeStruct((n_grp, Lg, v_w), BF16),
                   jax.ShapeDtypeStruct(s0.shape, F32)],
        compiler_params=pltpu.CompilerParams(dimension_semantics=("arbitrary", "arbitrary"),
                                             vmem_limit_bytes=VMEM_LIMIT),
        name="delta_rule",
    )(qkv, gcol, z, onorm, s0)
    return o.reshape(n_seq, L, v_w), s_new


def _out_mlp_kernel(*refs, n_a, tf):
    x_ref = refs[0]
    a_refs = refs[1:1 + n_a]
    w_ref, g_ref, wup_ref, wdn_ref, o_ref = refs[1 + n_a:]
    a = a_refs[0][...] if n_a == 1 else jnp.concatenate([a_ref[...] for a_ref in a_refs], axis=-1)
    x1 = x_ref[...] + jnp.dot(a, w_ref[...], preferred_element_type=F32)
    h = (x1 * _rms_scale(x1) * g_ref[...]).astype(BF16)
    acc = x1
    for f in range(wup_ref.shape[1] // tf):
        r = jnp.maximum(jnp.dot(h, wup_ref[:, f * tf:(f + 1) * tf], preferred_element_type=F32), 0.0)
        acc = acc + jnp.dot((r * r).astype(BF16), wdn_ref[f * tf:(f + 1) * tf, :], preferred_element_type=F32)
    o_ref[...] = acc


def _out_mlp(x, a_list, w_out, gain, w_up, w_down):
    shape = x.shape
    D = shape[-1]
    x2 = x.reshape(-1, D)
    n = x2.shape[0]
    a_list = [a.reshape(n, a.shape[-1]) for a in a_list]
    tm = min(512, n)
    assert n % tm == 0
    F = w_up.shape[1]
    tf = min(1024, F)
    assert F % tf == 0
    kern = functools.partial(_out_mlp_kernel, n_a=len(a_list), tf=tf)
    rows = lambda w: pl.BlockSpec((tm, w), lambda i: (i, 0))
    out = pl.pallas_call(
        kern,
        grid=(n // tm,),
        in_specs=([rows(D)] + [rows(a.shape[1]) for a in a_list] + [_const_spec(w_out.shape)]
                  + [_const_spec((1, D)), _const_spec(w_up.shape), _const_spec(w_down.shape)]),
        out_specs=rows(D),
        out_shape=jax.ShapeDtypeStruct((n, D), F32),
        compiler_params=pltpu.CompilerParams(dimension_semantics=("arbitrary",), vmem_limit_bytes=VMEM_LIMIT),
        name="out_mlp",
    )(x2, *a_list, w_out, gain, w_up, w_down)
    return out.reshape(shape)


def _odd_pre_kernel(x_ref, g_ref, w_ref, qn_ref, kn_ref, q_ref, k_ref, v_ref, *, n_heads):
    dh = LANES
    hd = n_heads * dh
    x = x_ref[0]
    h = (x * _rms_scale(x) * g_ref[...]).astype(BF16)
    q = jnp.dot(h, w_ref[:, 0:hd], preferred_element_type=F32)
    k = jnp.dot(h, w_ref[:, hd:2 * hd], preferred_element_type=F32)
    v = jnp.dot(h, w_ref[:, 2 * hd:3 * hd], preferred_element_type=F32)
    for hh in range(n_heads):
        cols = slice(hh * dh, (hh + 1) * dh)
        qh = q[:, cols]
        kh = k[:, cols]
        q_ref[0, hh] = (qh * _rms_scale(qh) * qn_ref[...] * (dh ** -0.5 * LOG2E)).astype(q_ref.dtype)
        k_ref[0, hh] = kh * _rms_scale(kh) * kn_ref[...]
        v_ref[0, hh] = v[:, cols]


def _odd_pre(x, gain, w_qkv, qn, kn, *, n_heads):
    B, L, D = x.shape
    tl = min(256, L)
    assert L % tl == 0 and w_qkv.shape[1] == 3 * n_heads * LANES
    kern = functools.partial(_odd_pre_kernel, n_heads=n_heads)
    head_major = pl.BlockSpec((1, n_heads, tl, LANES), lambda b, j: (b, 0, j, 0))
    hm_shape = (B, n_heads, L, LANES)
    return pl.pallas_call(
        kern,
        grid=(B, L // tl),
        in_specs=[pl.BlockSpec((1, tl, D), lambda b, j: (b, j, 0)), _const_spec((1, D)),
                  _const_spec(w_qkv.shape), _const_spec((1, LANES)), _const_spec((1, LANES))],
        out_specs=[head_major, head_major, head_major],
        out_shape=[jax.ShapeDtypeStruct(hm_shape, BF16), jax.ShapeDtypeStruct(hm_shape, F32),
                   jax.ShapeDtypeStruct(hm_shape, F32)],
        compiler_params=pltpu.CompilerParams(dimension_semantics=("arbitrary", "arbitrary"),
                                             vmem_limit_bytes=VMEM_LIMIT),
        name="odd_pre",
    )(x, gain, w_qkv, qn, kn)


def _suffix_ones(tk):
    j = lax.broadcasted_iota(jnp.int32, (2 * tk, tk), 0) & (tk - 1)
    s = lax.broadcasted_iota(jnp.int32, (2 * tk, tk), 1)
    return (j > s).astype(BF16)


def _sb_block(q, k_blk, v_blk, u, off, acc, below=None):
    z = _mm_nt(q, k_blk)
    t = jnp.maximum(z, 0.0) + jnp.log(1.0 + jnp.exp2(-jnp.abs(z))) * LOG2E
    if below is not None:
        t = jnp.where(below, t, 0.0)
    t_hi = t.astype(BF16)
    t_lo = (t - t_hi.astype(F32)).astype(BF16)
    later = jnp.dot(jnp.concatenate([t_hi, t_lo], axis=1), u, preferred_element_type=F32)
    wts = jnp.exp2(z - t - (later + off))
    return later[:, 0:1] + t[:, 0:1], acc + _mm(wts, v_blk)


def _sb_kernel(*refs, tq, tkp, n_past, n_q, hb, qb, n_prev):
    lazy_past = bool(n_past) and n_q == 1
    if lazy_past:
        q_ref, kn_ref, vn_ref, kl_ref, vl_ref, kp_ref, vp_ref, o_ref, kbuf, vbuf, sem = refs
    elif n_past:
        q_ref, kn_ref, vn_ref, kp_ref, vp_ref, o_ref = refs
    else:
        q_ref, kn_ref, vn_ref, o_ref = refs
    u_self = _suffix_ones(tq)
    u_past = _suffix_ones(tkp) if n_past else None
    row = lax.broadcasted_iota(jnp.int32, (tq, tq), 0)
    col = lax.broadcasted_iota(jnp.int32, (tq, tq), 1)
    below = col < row
    not_below = jnp.where(below, 0.0, MASKED)

    walks = []
    for hh, w in [(hh, w) for hh in range(hb) for w in range(qb)]:
        i = pl.program_id(2) * qb + w
        start = pl.multiple_of(i * tq, tq)
        q = q_ref[0, hh, w * tq:(w + 1) * tq, :]
        acc = jnp.zeros((tq, LANES), F32)
        spent, acc = _sb_block(q, kn_ref[0, hh, pl.ds(start, tq), :], vn_ref[0, hh, pl.ds(start, tq), :],
                               u_self, not_below, acc, below)
        first_self, first_past = i - 1, n_past - 1
        for d in range(1, n_prev + 1):
            prev = pl.multiple_of(jnp.maximum(i - d, 0) * tq, tq)
            has_prev = jnp.full((tq, 1), i, jnp.int32) >= d
            own, acc = _sb_block(q, kn_ref[0, hh, pl.ds(prev, tq), :], vn_ref[0, hh, pl.ds(prev, tq), :],
                                 u_self, jnp.where(has_prev, spent, MASKED), acc)
            spent = spent + jnp.where(has_prev, own, 0.0)
            first_self = i - d - 1
        if lazy_past:
            own, acc = _sb_block(q, kl_ref[0, hh], vl_ref[0, hh], u_past, spent, acc)
            spent = spent + own
            first_past = n_past - 2
        walks.append((hh, w, q, spent, acc, first_self, first_past))

    def walk(q, k_ref, v_ref, hh, tk, u, first, spent, acc, from_hbm=False):
        def cond(st):
            return (st[0] >= 0) & (jnp.min(st[1]) < -LOG2_W_FLOOR)

        def body(st):
            j, spent, acc = st
            s0 = pl.multiple_of(j * tk, tk)
            if from_hbm:
                head = pl.program_id(1) * hb + hh
                copies = [pltpu.make_async_copy(src.at[pl.program_id(0), head, pl.ds(s0, tk), :], dst, sem.at[n])
                          for n, (src, dst) in enumerate(((k_ref, kbuf), (v_ref, vbuf)))]
                for cp in copies:
                    cp.start()
                for cp in copies:
                    cp.wait()
                k_blk, v_blk = kbuf[...], vbuf[...]
            else:
                k_blk, v_blk = k_ref[0, hh, pl.ds(s0, tk), :], v_ref[0, hh, pl.ds(s0, tk), :]
            own, acc = _sb_block(q, k_blk, v_blk, u, spent, acc)
            return j - 1, spent + own, acc

        _, spent, acc = lax.while_loop(cond, body, (first, spent, acc))
        return spent, acc

    for hh, w, _, _, acc, _, _ in walks:
        o_ref[0, w * tq:(w + 1) * tq, hh * LANES:(hh + 1) * LANES] = acc.astype(o_ref.dtype)
    if n_q == 1 and n_past <= 1:
        return

    least = functools.reduce(jnp.minimum, [wk[3] for wk in walks])

    @pl.when(jnp.min(least) < -LOG2_W_FLOOR)
    def _():
        for hh, w, q, spent, acc, first_self, first_past in walks:
            if n_q > 1:
                spent, acc = walk(q, kn_ref, vn_ref, hh, tq, u_self, first_self, spent, acc)
            if n_past:
                spent, acc = walk(q, kp_ref, vp_ref, hh, tkp, u_past, jnp.int32(first_past), spent, acc,
                                  from_hbm=lazy_past)
            o_ref[0, w * tq:(w + 1) * tq, hh * LANES:(hh + 1) * LANES] = acc.astype(o_ref.dtype)


def _stick_breaking(q, k_new, v_new, k_past, v_past):
    B, H, L, dh = q.shape
    assert dh == LANES
    tq = min(SB_TQ, L)
    assert L % tq == 0
    n_q = L // tq
    assert tq & (tq - 1) == 0
    hb, qb = (H, 1) if n_q == 1 else (2, max(d for d in (4, 2, 1) if n_q % d == 0))
    n_prev = 0 if n_q == 1 else SB_PREV
    assert H % hb == 0 and n_q % qb == 0
    args = [q, k_new, v_new]
    full = lambda n: pl.BlockSpec((1, hb, n, LANES), lambda b, h, i: (b, h, 0, 0))
    in_specs = [pl.BlockSpec((1, hb, tq * qb, LANES), lambda b, h, i: (b, h, i, 0)), full(L), full(L)]
    n_past, tkp = 0, 0
    scratch = []
    if k_past is not None:
        P = k_past.shape[2]
        tkp = min(256, P)
        assert P % tkp == 0 and tkp & (tkp - 1) == 0
        n_past = P // tkp
        if n_q == 1:
            last = pl.BlockSpec((1, hb, tkp, LANES), lambda b, h, i: (b, h, n_past - 1, 0))
            hbm = pl.BlockSpec(memory_space=pl.ANY)
            args += [k_past, v_past, k_past, v_past]
            in_specs += [last, last, hbm, hbm]
            scratch = [pltpu.VMEM((tkp, LANES), k_past.dtype), pltpu.VMEM((tkp, LANES), v_past.dtype),
                       pltpu.SemaphoreType.DMA((2,))]
        else:
            args += [k_past, v_past]
            in_specs += [full(P), full(P)]
    kern = functools.partial(_sb_kernel, tq=tq, tkp=tkp, n_past=n_past, n_q=n_q, hb=hb, qb=qb, n_prev=n_prev)
    return pl.pallas_call(
        kern,
        grid=(B, H // hb, n_q // qb),
        in_specs=in_specs,
        out_specs=pl.BlockSpec((1, tq * qb, hb * LANES), lambda b, h, i: (b, i, h)),
        scratch_shapes=scratch,
        out_shape=jax.ShapeDtypeStruct((B, L, H * LANES), BF16),
        compiler_params=pltpu.CompilerParams(dimension_semantics=("arbitrary", "arbitrary", "arbitrary"),
                                             vmem_limit_bytes=VMEM_LIMIT),
        name="stick_breaking",
    )(*args)


def _pad_lanes(v):
    return jnp.zeros((1, LANES), F32).at[0, :v.shape[0]].set(v.astype(F32))


def _even_layer(x, s0, qbuf, dbuf, wts, gain_mix, gain_mlp, w_up, w_down):
    (w_all, conv_w, alog_pad, dtb_pad, onorm, dw_w, dw_b, ln_g, ln_b, w_out, n_heads, qkv_w, v_w, c_b) = wts
    qkv, gcol, z, c, qlast, dlast = _even_pre(x, gain_mix, w_all, conv_w, alog_pad, dtb_pad, dw_w, dw_b, ln_g, ln_b,
                                              qbuf, dbuf, n_heads=n_heads, qkv_w=qkv_w, v_w=v_w, c_b=c_b)
    o, s_new = _delta(qkv, gcol, z, onorm, s0, n_heads=n_heads)
    y = _out_mlp(x, [o, c], w_out, gain_mlp, w_up, w_down)
    return y, s_new, qlast, dlast


def _odd_layer(x, k_past, v_past, wts, gain_mix, gain_mlp, w_up, w_down):
    w_qkv, qn, kn, w_out, n_heads = wts
    q, k, v = _odd_pre(x, gain_mix, w_qkv, qn, kn, n_heads=n_heads)
    o = _stick_breaking(q, k, v, k_past, v_past)
    y = _out_mlp(x, [o], w_out, gain_mlp, w_up, w_down)
    return y, k, v


def kernel(x_prompt, x_sample, state_delta, state_qkv_conv, state_dw_conv, cache_k, cache_v, norm_mix, norm_mlp, w_in_e, conv_qkv_e, a_log_e, dt_bias_e, onorm_e, dw_w_e, dw_b_e, ln_g_e, ln_b_e, w_out_e, w_qkv_o, qn_o, kn_o, w_out_o, w_up, w_down):
    depth = norm_mix.shape[0]
    bp = x_prompt.shape[0]
    xp, xs = x_prompt, x_sample
    pd, pq, pw, pk, pv = [], [], [], [], []
    sd, sq, sw, sk, sv = [], [], [], [], []
    row = lambda v: v.astype(F32).reshape(1, -1)
    for i in range(depth):
        gm, gl = row(norm_mix[i]), row(norm_mlp[i])
        wu, wd = w_up[i].astype(BF16), w_down[i].astype(BF16)
        if i % 2 == 0:
            e = i // 2
            n_heads = a_log_e.shape[1]
            qkv_w = conv_qkv_e.shape[2]
            c_b = dw_w_e.shape[2]
            v_w = n_heads * onorm_e.shape[1]
            w_in = w_in_e[e]
            ab0 = qkv_w + v_w
            ab = jnp.zeros((w_in.shape[0], LANES), w_in.dtype).at[:, :2 * n_heads].set(w_in[:, ab0:ab0 + 2 * n_heads])
            w_all = jnp.concatenate([w_in[:, :ab0], w_in[:, ab0 + 2 * n_heads:], ab], axis=1).astype(BF16)
            w_out = w_out_e[e].astype(BF16)
            wts = (w_all, conv_qkv_e[e].astype(F32), _pad_lanes(a_log_e[e]), _pad_lanes(dt_bias_e[e]),
                   row(onorm_e[e]), dw_w_e[e].astype(F32), row(dw_b_e[e]), row(ln_g_e[e]), row(ln_b_e[e]),
                   w_out, n_heads, qkv_w, v_w, c_b)
            kq, kd = conv_qkv_e.shape[1], dw_w_e.shape[1]
            s0 = jnp.zeros((bp,) + state_delta.shape[2:], F32)
            qb0 = jnp.zeros((bp, kq - 1, qkv_w), F32)
            db0 = jnp.zeros((bp, kd - 1, c_b), F32)
            xp, d1, q1, c1 = _even_layer(xp, s0, qb0, db0, wts, gm, gl, wu, wd)
            xs, d2, q2, c2 = _even_layer(xs, state_delta[e].astype(F32), state_qkv_conv[e].astype(F32),
                                         state_dw_conv[e].astype(F32), wts, gm, gl, wu, wd)
            pd.append(d1); pq.append(q1); pw.append(c1)
            sd.append(d2); sq.append(q2); sw.append(c2)
        else:
            o = i // 2
            n_heads = cache_k.shape[2]
            wts = (w_qkv_o[o].astype(BF16), row(qn_o[o]), row(kn_o[o]), w_out_o[o].astype(BF16), n_heads)
            xp, k1, v1 = _odd_layer(xp, None, None, wts, gm, gl, wu, wd)
            xs, k2, v2 = _odd_layer(xs, cache_k[o].astype(F32), cache_v[o].astype(F32), wts, gm, gl, wu, wd)
            pk.append(k1); pv.append(v1)
            sk.append(k2); sv.append(v2)
    return (xp, xs,
            jnp.stack(pd), jnp.stack(pq), jnp.stack(pw), jnp.stack(pk), jnp.stack(pv),
            jnp.stack(sd), jnp.stack(sq), jnp.stack(sw), jnp.stack(sk), jnp.stack(sv))
```

```python
import functools

import jax
import jax.numpy as jnp
from jax import lax
from jax.experimental import pallas as pl
from jax.experimental.pallas import tpu as pltpu

F32 = jnp.float32
BF16 = jnp.bfloat16
EPS = 1e-6
CHUNK = 64
LANES = 128
PAIR = 2 * CHUNK
QBUF_OFF = 8
DBUF_OFF = 32
DW_ROWS = 32
SUBLANES = 8
VMEM_LIMIT = 56 * 1024 * 1024
LOG2E = 1.4426950408889634
LOG2_W_FLOOR = -151.0
SB_TQ = 256
SB_PREV = 1
SB_KV_VMEM = 36 * 1024 * 1024
SB_CHAINS = 16
MASKED = 1e30


def _mm(a, b):
    return jnp.dot(a.astype(BF16), b.astype(BF16), preferred_element_type=F32)


def _mm_nt(a, b):
    return lax.dot_general(a.astype(BF16), b.astype(BF16), (((1,), (1,)), ((), ())),
                           preferred_element_type=F32)


def _sigmoid(x):
    return 0.5 * jnp.tanh(0.5 * x) + 0.5


def _silu(x):
    h = 0.5 * x
    return h + h * jnp.tanh(h)


def _softplus(x):
    return jnp.maximum(x, 0.0) + jnp.log1p(jnp.exp(-jnp.abs(x)))


def _rms_scale(x):
    return lax.rsqrt(jnp.mean(x * x, axis=-1, keepdims=True) + EPS)


def _const_spec(shape):
    nd = len(shape)
    return pl.BlockSpec(shape, lambda *_: (0,) * nd, pipeline_mode=pl.Buffered(1))


def _even_pre_kernel(x_ref, g_ref, w_ref, cw_ref, alog_ref, dtb_ref, dww_ref, dwb_ref, lng_ref, lnb_ref,
                     qbuf_ref, dbuf_ref,
                     qkv_ref, gcol_ref, z_ref, c_ref, qlast_ref, dlast_ref,
                     qext, uext, ushift, *, ns, tl, qkv_w, v_w, c_b, n_heads):
    kq = cw_ref.shape[0] // SUBLANES
    kd = dww_ref.shape[0] // SUBLANES
    m = ns * tl

    @pl.when(pl.program_id(1) == 0)
    def _():
        for sq in range(ns):
            qext[sq, QBUF_OFF - (kq - 1):QBUF_OFF, :] = qbuf_ref[sq]
            uext[sq, 0:DBUF_OFF - (kd - 1), :] = jnp.zeros((DBUF_OFF - (kd - 1), c_b), F32)
            uext[sq, DBUF_OFF - (kd - 1):DBUF_OFF, :] = dbuf_ref[sq]

    x = x_ref[...].reshape(m, x_ref.shape[-1])
    h = (x * _rms_scale(x) * g_ref[...]).astype(BF16)

    glu0 = qkv_w + v_w
    qext[:, QBUF_OFF:QBUF_OFF + tl, :] = jnp.dot(h, w_ref[:, 0:qkv_w],
                                                 preferred_element_type=F32).reshape(ns, tl, qkv_w)
    z_ref[...] = jnp.dot(h, w_ref[:, qkv_w:glu0], preferred_element_type=F32).reshape(ns, tl, v_w)
    ga = jnp.dot(h, w_ref[:, glu0:glu0 + c_b], preferred_element_type=F32)
    gb = jnp.dot(h, w_ref[:, glu0 + c_b:glu0 + 2 * c_b], preferred_element_type=F32)
    uext[:, DBUF_OFF:DBUF_OFF + tl, :] = (ga * _sigmoid(gb)).reshape(ns, tl, c_b)
    ab = jnp.dot(h, w_ref[:, glu0 + 2 * c_b:glu0 + 2 * c_b + LANES], preferred_element_type=F32)

    for sq in range(ns):
        for s in range(qkv_w // LANES):
            cols = slice(s * LANES, (s + 1) * LANES)
            y = None
            for j in range(kq):
                r0 = QBUF_OFF - (kq - 1) + j
                wj = cw_ref[j * SUBLANES:(j + 1) * SUBLANES, cols]
                t = wj[None] * qext[sq, r0:r0 + tl, cols].reshape(tl // SUBLANES, SUBLANES, LANES)
                y = t if y is None else y + t
            y = _silu(y.reshape(tl, LANES))
            if s < 2 * n_heads:
                y = y * lax.rsqrt(jnp.sum(y * y, axis=-1, keepdims=True) + EPS)
            qkv_ref[sq, :, cols] = y
        qlast = qext[sq, QBUF_OFF + tl - (kq - 1):QBUF_OFF + tl, :]
        qlast_ref[sq] = qlast
        qext[sq, QBUF_OFF - (kq - 1):QBUF_OFF, :] = qlast

        span = tl + DBUF_OFF - SUBLANES
        for b in range(1, SUBLANES):
            ushift[b - 1, 0:span, :] = uext[sq, b:b + span, :]
        for r in range(tl // DW_ROWS):
            acc = None
            for j in range(kd):
                a, b = divmod(DBUF_OFF - (kd - 1) + j, SUBLANES)
                r0 = a * SUBLANES + r * DW_ROWS
                win = uext[sq, r0:r0 + DW_ROWS, :] if b == 0 else ushift[b - 1, r0:r0 + DW_ROWS, :]
                wj = dww_ref[j * SUBLANES:(j + 1) * SUBLANES, :]
                t = wj[None] * win.reshape(DW_ROWS // SUBLANES, SUBLANES, c_b)
                acc = t if acc is None else acc + t
            cpre = acc.reshape(DW_ROWS, c_b) + dwb_ref[...]
            mu = jnp.mean(cpre, axis=-1, keepdims=True)
            xc = cpre - mu
            var = jnp.mean(xc * xc, axis=-1, keepdims=True)
            y = xc * lax.rsqrt(var + EPS) * lng_ref[...] + lnb_ref[...]
            c_ref[sq, r * DW_ROWS:(r + 1) * DW_ROWS, :] = _silu(y).astype(c_ref.dtype)
        dlast = uext[sq, DBUF_OFF + tl - (kd - 1):DBUF_OFF + tl, :]
        dlast_ref[sq] = dlast
        uext[sq, DBUF_OFF - (kd - 1):DBUF_OFF, :] = dlast

    lane = lax.broadcasted_iota(jnp.int32, (m, LANES), 1)
    g = -jnp.exp(alog_ref[...]) * _softplus(ab + dtb_ref[...])
    g = jnp.where(lane < n_heads, g, 0.0)
    beta = _sigmoid(ab)
    row = lax.broadcasted_iota(jnp.int32, (m, m), 0)
    col = lax.broadcasted_iota(jnp.int32, (m, m), 1)
    tri = (((row ^ col) < CHUNK) & (col <= row)).astype(BF16)
    g_hi = g.astype(BF16)
    g_r1 = g - g_hi.astype(F32)
    g_mid = g_r1.astype(BF16)
    g_lo = (g_r1 - g_mid.astype(F32)).astype(BF16)
    gcum = (jnp.dot(tri, g_hi, preferred_element_type=F32)
            + jnp.dot(tri, g_mid, preferred_element_type=F32)
            + jnp.dot(tri, g_lo, preferred_element_type=F32))
    gcol = jnp.where(lane < n_heads, gcum, jnp.where(lane < 2 * n_heads, beta, 0.0))
    gcol_ref[...] = gcol.reshape(ns, tl, LANES)


def _even_pre(x, gain, w_all, conv_w, alog_pad, dtb_pad, dw_w, dw_b, ln_g, ln_b, qbuf, dbuf, *, n_heads, qkv_w, v_w, c_b):
    n_seq, L, D = x.shape
    tl = min(256, L)
    assert L % tl == 0 and tl % DW_ROWS == 0 and tl % CHUNK == 0
    kq, kd = conv_w.shape[0], dw_w.shape[0]
    assert kq - 1 <= QBUF_OFF and kd - 1 <= DBUF_OFF and tl >= kd - 1
    conv_w = jnp.repeat(conv_w, SUBLANES, axis=0)
    dw_w = jnp.repeat(dw_w, SUBLANES, axis=0)
    nw = w_all.shape[1]
    ns = max(d for d in range(1, max(1, 256 // tl) + 1) if n_seq % d == 0)
    kern = functools.partial(_even_pre_kernel, ns=ns, tl=tl, qkv_w=qkv_w, v_w=v_w, c_b=c_b, n_heads=n_heads)
    seq_blk = lambda w: pl.BlockSpec((ns, tl, w), lambda s, j: (s, j, 0))
    per_seq = lambda r, w: pl.BlockSpec((ns, r, w), lambda s, j: (s, 0, 0))
    return pl.pallas_call(
        kern,
        grid=(n_seq // ns, L // tl),
        in_specs=[seq_blk(D), _const_spec((1, D)), _const_spec((D, nw)), _const_spec(conv_w.shape),
                  _const_spec((1, LANES)), _const_spec((1, LANES)), _const_spec(dw_w.shape),
                  _const_spec((1, c_b)), _const_spec((1, c_b)), _const_spec((1, c_b)),
                  per_seq(kq - 1, qkv_w), per_seq(kd - 1, c_b)],
        out_specs=[seq_blk(qkv_w), seq_blk(LANES), seq_blk(v_w), seq_blk(c_b),
                   per_seq(kq - 1, qkv_w), per_seq(kd - 1, c_b)],
        out_shape=[jax.ShapeDtypeStruct((n_seq, L, qkv_w), F32),
                   jax.ShapeDtypeStruct((n_seq, L, LANES), F32),
                   jax.ShapeDtypeStruct((n_seq, L, v_w), F32),
                   jax.ShapeDtypeStruct((n_seq, L, c_b), BF16),
                   jax.ShapeDtypeStruct((n_seq, kq - 1, qkv_w), F32),
                   jax.ShapeDtypeStruct((n_seq, kd - 1, c_b), F32)],
        scratch_shapes=[pltpu.VMEM((ns, QBUF_OFF + tl, qkv_w), F32), pltpu.VMEM((ns, DBUF_OFF + tl, c_b), F32),
                        pltpu.VMEM((SUBLANES - 1, DBUF_OFF + tl, c_b), F32)],
        compiler_params=pltpu.CompilerParams(dimension_semantics=("arbitrary", "arbitrary"),
                                             vmem_limit_bytes=VMEM_LIMIT),
        name="even_pre",
    )(x, gain, w_all, conv_w, alog_pad, dtb_pad, dw_w, dw_b, ln_g, ln_b, qbuf, dbuf)


def _unit_lower_inverses(a_list, row, col):
    x = row ^ col
    eye = (row == col).astype(F32)
    a8 = [jnp.where(x < 8, a, 0.0) for a in a_list]
    a8_2 = [_mm(t, t) for t in a8]
    p = [eye - t for t in a8]
    a8_4 = [_mm(t, t) for t in a8_2]
    p = [t + _mm(t, sq) for t, sq in zip(p, a8_2)]
    d = [t + _mm(t, sq) for t, sq in zip(p, a8_4)]
    s = 8
    while s < CHUNK:
        off = [jnp.where((x >= s) & (x < 2 * s), a, 0.0) for a in a_list]
        od = [_mm(o, t) for o, t in zip(off, d)]
        d = [t - _mm(t, u) for t, u in zip(d, od)]
        s *= 2
    return d


def _delta_kernel(qkv_ref, gcol_ref, z_ref, onorm_ref, sin_ref, o_ref, sout_ref, *, tb, n_heads, chained):
    dk = dv = LANES
    qk_w = n_heads * dk
    n_pairs = tb // PAIR
    if chained:
        @pl.when(pl.program_id(1) == 0)
        def _():
            sout_ref[...] = sin_ref[...]

    row = lax.broadcasted_iota(jnp.int32, (PAIR, PAIR), 0)
    col = lax.broadcasted_iota(jnp.int32, (PAIR, PAIR), 1)
    same = (row ^ col) < CHUNK
    causal = same & (col <= row)
    strict = same & (col < row)
    in_chunk = [(col >= c * CHUNK) & (col < (c + 1) * CHUNK) for c in range(2)]

    tiles = [(p, h) for p in range(n_pairs) for h in range(n_heads)]
    q, k, kb, vb, gc, a, qk = {}, {}, {}, {}, {}, {}, {}
    for p, h in tiles:
        rows = slice(p * PAIR, (p + 1) * PAIR)
        q[p, h] = qkv_ref[0, rows, h * dk:(h + 1) * dk] * (dk ** -0.5)
        k[p, h] = qkv_ref[0, rows, qk_w + h * dk:qk_w + (h + 1) * dk]
        gc[p, h] = jnp.broadcast_to(gcol_ref[0, rows, h:h + 1], (PAIR, PAIR))
        bt = jnp.broadcast_to(gcol_ref[0, rows, n_heads + h:n_heads + h + 1], (PAIR, PAIR))
        kb[p, h] = k[p, h] * bt
        vb[p, h] = qkv_ref[0, rows, 2 * qk_w + h * dv:2 * qk_w + (h + 1) * dv] * bt
    for t in tiles:
        diff = gc[t] - gc[t].T
        decay = jnp.where(causal, jnp.exp(jnp.where(causal, diff, 0.0)), 0.0)
        a[t] = jnp.where(strict, _mm_nt(kb[t], k[t]) * decay, 0.0)
        qk[t] = jnp.where(causal, _mm_nt(q[t], k[t]) * decay, 0.0)
    inv = dict(zip(tiles, _unit_lower_inverses([a[t] for t in tiles], row, col)))
    uw, kg_t, gl = {}, {}, {}
    for t in tiles:
        eg = jnp.exp(gc[t])
        uw[t] = _mm(inv[t], jnp.concatenate([vb[t], kb[t] * eg], axis=1))
        gl_rows = [jnp.broadcast_to(gc[t][(c + 1) * CHUNK - 1:(c + 1) * CHUNK, :], (PAIR, PAIR)) for c in range(2)]
        gl[t] = [jnp.exp(g) for g in gl_rows]
        gcl = jnp.where(row < CHUNK, gl_rows[0], gl_rows[1])
        kg_t[t] = (k[t] * jnp.exp(gcl - gc[t])).T
        q[t] = q[t] * eg
    qk_uw = {t: _mm(qk[t], uw[t]) for t in tiles}
    kg_uw = {(t, c): _mm(jnp.where(in_chunk[c], kg_t[t], 0.0), uw[t])
             for t in tiles for c in range(2)}

    state = [sout_ref[0, h] for h in range(n_heads)] if chained else None
    for p in range(n_pairs):
        rows = slice(p * PAIR, (p + 1) * PAIR)
        o_parts = {h: [] for h in range(n_heads)}
        for c in range(2):
            r = slice(c * CHUNK, (c + 1) * CHUNK)
            for h in range(n_heads):
                t = (p, h)
                s_old = state[h] if chained else sin_ref[c, h]
                q_eff = q[t][r] - qk_uw[t][r, dv:]
                xs = _mm(jnp.concatenate([kg_uw[t, c][:, dv:], q_eff], axis=0), s_old)
                o_parts[h].append(qk_uw[t][r, :dv] + xs[dk:])
                s_new = s_old * gl[t][c] + (kg_uw[t, c][:, :dv] - xs[:dk])
                if chained:
                    state[h] = s_new
                else:
                    sout_ref[c, h] = s_new
        for h in range(n_heads):
            o = jnp.concatenate(o_parts[h], axis=0)
            zz = z_ref[0, rows, h * dv:(h + 1) * dv].astype(F32)
            y = (o * _rms_scale(o) * onorm_ref[...]) * _silu(zz)
            o_ref[0, rows, h * dv:(h + 1) * dv] = y.astype(o_ref.dtype)
    if chained:
        for h in range(n_heads):
            sout_ref[0, h] = state[h]


def _delta(qkv, gcol, z, onorm, s0, *, n_heads):
    n_seq, L, qkv_w = qkv.shape
    v_w = z.shape[-1]
    assert qkv_w == 3 * n_heads * LANES and v_w == n_heads * LANES
    if L % PAIR == 0:
        chained, n_grp, Lg, per_grp = True, n_seq, L, 1
    else:
        assert L == CHUNK and n_seq % 2 == 0
        chained, n_grp, Lg, per_grp = False, n_seq // 2, PAIR, 2
        qkv, gcol, z = (t.reshape(n_grp, PAIR, t.shape[-1]) for t in (qkv, gcol, z))
    tb = min(512, Lg)
    assert Lg % tb == 0
    kern = functools.partial(_delta_kernel, tb=tb, n_heads=n_heads, chained=chained)
    blk = lambda w: pl.BlockSpec((1, tb, w), lambda s, j: (s, j, 0))
    st = pl.BlockSpec((per_grp, n_heads, LANES, LANES), lambda s, j: (s, 0, 0, 0))
    o, s_new = pl.pallas_call(
        kern,
        grid=(n_grp, Lg // tb),
        in_specs=[blk(qkv_w), blk(LANES), blk(v_w), _const_spec((1, LANES)), st],
        out_specs=[blk(v_w), st],
        out_shape=[jax.ShapeDtypeStruct((n_grp, Lg, v_w), BF16),
                   jax.ShapeDtypeStruct(s0.shape, F32)],
        compiler_params=pltpu.CompilerParams(dimension_semantics=("arbitrary", "arbitrary"),
                                             vmem_limit_bytes=VMEM_LIMIT),
        name="delta_rule",
    )(qkv, gcol, z, onorm, s0)
    return o.reshape(n_seq, L, v_w), s_new


def _out_mlp_kernel(*refs, n_a, tf):
    x_ref = refs[0]
    a_refs = refs[1:1 + n_a]
    w_ref, g_ref, wup_ref, wdn_ref, o_ref = refs[1 + n_a:]
    a = a_refs[0][...] if n_a == 1 else jnp.concatenate([a_ref[...] for a_ref in a_refs], axis=-1)
    x1 = x_ref[...] + jnp.dot(a, w_ref[...], preferred_element_type=F32)
    h = (x1 * _rms_scale(x1) * g_ref[...]).astype(BF16)
    acc = x1
    for f in range(wup_ref.shape[1] // tf):
        r = jnp.maximum(jnp.dot(h, wup_ref[:, f * tf:(f + 1) * tf], preferred_element_type=F32), 0.0)
        acc = acc + jnp.dot((r * r).astype(BF16), wdn_ref[f * tf:(f + 1) * tf, :], preferred_element_type=F32)
    o_ref[...] = acc


def _out_mlp(x, a_list, w_out, gain, w_up, w_down):
    shape = x.shape
    D = shape[-1]
    x2 = x.reshape(-1, D)
    n = x2.shape[0]
    a_list = [a.reshape(n, a.shape[-1]) for a in a_list]
    tm = min(512, n)
    assert n % tm == 0
    F = w_up.shape[1]
    tf = min(1024, F)
    assert F % tf == 0
    kern = functools.partial(_out_mlp_kernel, n_a=len(a_list), tf=tf)
    rows = lambda w: pl.BlockSpec((tm, w), lambda i: (i, 0))
    out = pl.pallas_call(
        kern,
        grid=(n // tm,),
        in_specs=([rows(D)] + [rows(a.shape[1]) for a in a_list] + [_const_spec(w_out.shape)]
                  + [_const_spec((1, D)), _const_spec(w_up.shape), _const_spec(w_down.shape)]),
        out_specs=rows(D),
        out_shape=jax.ShapeDtypeStruct((n, D), F32),
        compiler_params=pltpu.CompilerParams(dimension_semantics=("arbitrary",), vmem_limit_bytes=VMEM_LIMIT),
        name="out_mlp",
    )(x2, *a_list, w_out, gain, w_up, w_down)
    return out.reshape(shape)


def _odd_pre_kernel(x_ref, g_ref, w_ref, qn_ref, kn_ref, q_ref, k_ref, v_ref, *mxu_copies, n_heads):
    dh = LANES
    hd = n_heads * dh
    ns, tl, d_model = x_ref.shape
    x = x_ref[...].reshape(ns * tl, d_model)
    h = (x * _rms_scale(x) * g_ref[...]).astype(BF16)
    q = jnp.dot(h, w_ref[:, 0:hd], preferred_element_type=F32)
    k = jnp.dot(h, w_ref[:, hd:2 * hd], preferred_element_type=F32)
    v = jnp.dot(h, w_ref[:, 2 * hd:3 * hd], preferred_element_type=F32)
    per_seq = lambda t: t.reshape(ns, tl, dh)
    for hh in range(n_heads):
        cols = slice(hh * dh, (hh + 1) * dh)
        qh = q[:, cols]
        kh = k[:, cols]
        q_ref[:, hh] = per_seq((qh * _rms_scale(qh) * qn_ref[...] * (dh ** -0.5 * LOG2E)).astype(q_ref.dtype))
        kh = kh * _rms_scale(kh) * kn_ref[...]
        k_ref[:, hh] = per_seq(kh)
        v_ref[:, hh] = per_seq(v[:, cols])
        if mxu_copies:
            mxu_copies[0][:, hh] = per_seq(kh.astype(BF16))
            mxu_copies[1][:, hh] = per_seq(v[:, cols].astype(BF16))


def _odd_pre(x, gain, w_qkv, qn, kn, *, n_heads, mxu_copies):
    B, L, D = x.shape
    tl = min(256, L)
    assert L % tl == 0 and w_qkv.shape[1] == 3 * n_heads * LANES
    kern = functools.partial(_odd_pre_kernel, n_heads=n_heads)
    ns = max(d for d in range(1, max(1, 256 // tl) + 1) if B % d == 0)
    head_major = pl.BlockSpec((ns, n_heads, tl, LANES), lambda b, j: (b, 0, j, 0))
    hm_shape = (B, n_heads, L, LANES)
    n_copies = 2 if mxu_copies else 0
    return pl.pallas_call(
        kern,
        grid=(B // ns, L // tl),
        in_specs=[pl.BlockSpec((ns, tl, D), lambda b, j: (b, j, 0)), _const_spec((1, D)),
                  _const_spec(w_qkv.shape), _const_spec((1, LANES)), _const_spec((1, LANES))],
        out_specs=[head_major] * (3 + n_copies),
        out_shape=([jax.ShapeDtypeStruct(hm_shape, BF16), jax.ShapeDtypeStruct(hm_shape, F32),
                    jax.ShapeDtypeStruct(hm_shape, F32)] + [jax.ShapeDtypeStruct(hm_shape, BF16)] * n_copies),
        compiler_params=pltpu.CompilerParams(dimension_semantics=("arbitrary", "arbitrary"),
                                             vmem_limit_bytes=VMEM_LIMIT),
        name="odd_pre",
    )(x, gain, w_qkv, qn, kn)


def _suffix_ones(tk):
    j = lax.broadcasted_iota(jnp.int32, (2 * tk, tk), 0) & (tk - 1)
    s = lax.broadcasted_iota(jnp.int32, (2 * tk, tk), 1)
    return (j > s).astype(BF16)


def _sb_block(q, k_blk, v_blk, u, off, acc, below=None):
    z = _mm_nt(q, k_blk)
    t = jnp.maximum(z, 0.0) + jnp.log(1.0 + jnp.exp2(-jnp.abs(z))) * LOG2E
    if below is not None:
        t = jnp.where(below, t, 0.0)
    t_hi = t.astype(BF16)
    t_lo = (t - t_hi.astype(F32)).astype(BF16)
    later = jnp.dot(jnp.concatenate([t_hi, t_lo], axis=1), u, preferred_element_type=F32)
    wts = jnp.exp2(z - t - (later + off))
    return later[:, 0:1] + t[:, 0:1], acc + _mm(wts, v_blk)


def _sb_kernel(*refs, tq, tkp, n_past, n_q, hb, qb, n_prev):
    lazy_past = bool(n_past) and n_q == 1
    if lazy_past:
        q_ref, kn_ref, vn_ref, kl_ref, vl_ref, kp_ref, vp_ref, o_ref, kbuf, vbuf, sem = refs
    elif n_past:
        q_ref, kn_ref, vn_ref, kp_ref, vp_ref, o_ref = refs
    else:
        q_ref, kn_ref, vn_ref, o_ref = refs
    u_self = _suffix_ones(tq)
    u_past = _suffix_ones(tkp) if n_past else None
    row = lax.broadcasted_iota(jnp.int32, (tq, tq), 0)
    col = lax.broadcasted_iota(jnp.int32, (tq, tq), 1)
    below = col < row
    not_below = jnp.where(below, 0.0, MASKED)

    walks = []
    for hh, w in [(hh, w) for hh in range(hb) for w in range(qb)]:
        i = pl.program_id(2) * qb + w
        start = pl.multiple_of(i * tq, tq)
        q = q_ref[0, hh, w * tq:(w + 1) * tq, :]
        acc = jnp.zeros((tq, LANES), F32)
        spent, acc = _sb_block(q, kn_ref[0, hh, pl.ds(start, tq), :], vn_ref[0, hh, pl.ds(start, tq), :],
                               u_self, not_below, acc, below)
        first_self, first_past = i - 1, n_past - 1
        for d in range(1, n_prev + 1):
            prev = pl.multiple_of(jnp.maximum(i - d, 0) * tq, tq)
            has_prev = jnp.full((tq, 1), i, jnp.int32) >= d
            own, acc = _sb_block(q, kn_ref[0, hh, pl.ds(prev, tq), :], vn_ref[0, hh, pl.ds(prev, tq), :],
                                 u_self, jnp.where(has_prev, spent, MASKED), acc)
            spent = spent + jnp.where(has_prev, own, 0.0)
            first_self = i - d - 1
        if lazy_past:
            own, acc = _sb_block(q, kl_ref[0, hh], vl_ref[0, hh], u_past, spent, acc)
            spent = spent + own
            first_past = n_past - 2
        walks.append((hh, w, q, spent, acc, first_self, first_past))

    def walk(q, k_ref, v_ref, hh, tk, u, first, spent, acc, from_hbm=False):
        def cond(st):
            return (st[0] >= 0) & (jnp.min(st[1]) < -LOG2_W_FLOOR)

        def body(st):
            j, spent, acc = st
            s0 = pl.multiple_of(j * tk, tk)
            if from_hbm:
                head = pl.program_id(1) * hb + hh
                copies = [pltpu.make_async_copy(src.at[pl.program_id(0), head, pl.ds(s0, tk), :], dst, sem.at[n])
                          for n, (src, dst) in enumerate(((k_ref, kbuf), (v_ref, vbuf)))]
                for cp in copies:
                    cp.start()
                for cp in copies:
                    cp.wait()
                k_blk, v_blk = kbuf[...], vbuf[...]
            else:
                k_blk, v_blk = k_ref[0, hh, pl.ds(s0, tk), :], v_ref[0, hh, pl.ds(s0, tk), :]
            own, acc = _sb_block(q, k_blk, v_blk, u, spent, acc)
            return j - 1, spent + own, acc

        _, spent, acc = lax.while_loop(cond, body, (first, spent, acc))
        return spent, acc

    for hh, w, _, _, acc, _, _ in walks:
        o_ref[0, w * tq:(w + 1) * tq, hh * LANES:(hh + 1) * LANES] = acc.astype(o_ref.dtype)
    if n_q == 1 and n_past <= 1:
        return

    least = functools.reduce(jnp.minimum, [wk[3] for wk in walks])

    @pl.when(jnp.min(least) < -LOG2_W_FLOOR)
    def _():
        for hh, w, q, spent, acc, first_self, first_past in walks:
            if n_q > 1:
                spent, acc = walk(q, kn_ref, vn_ref, hh, tq, u_self, first_self, spent, acc)
            if n_past:
                spent, acc = walk(q, kp_ref, vp_ref, hh, tkp, u_past, jnp.int32(first_past), spent, acc,
                                  from_hbm=lazy_past)
            o_ref[0, w * tq:(w + 1) * tq, hh * LANES:(hh + 1) * LANES] = acc.astype(o_ref.dtype)


def _stick_breaking(q, k_new, v_new, k_past, v_past):
    B, H, L, dh = q.shape
    assert dh == LANES
    tq = min(SB_TQ, L)
    assert L % tq == 0
    n_q = L // tq
    assert tq & (tq - 1) == 0
    if n_q == 1:
        hb, qb = H, 1
    else:
        kv_bytes = 2 * 2 * L * LANES * k_new.dtype.itemsize
        hb = max(d for d in (4, 2, 1) if H % d == 0 and d * kv_bytes <= SB_KV_VMEM)
        qb = max(d for d in (SB_CHAINS // hb, 2, 1) if n_q % d == 0)
    n_prev = 0 if n_q == 1 else SB_PREV
    assert H % hb == 0 and n_q % qb == 0
    args = [q, k_new, v_new]
    full = lambda n: pl.BlockSpec((1, hb, n, LANES), lambda b, h, i: (b, h, 0, 0))
    in_specs = [pl.BlockSpec((1, hb, tq * qb, LANES), lambda b, h, i: (b, h, i, 0)), full(L), full(L)]
    n_past, tkp = 0, 0
    scratch = []
    if k_past is not None:
        P = k_past.shape[2]
        tkp = min(256, P)
        assert P % tkp == 0 and tkp & (tkp - 1) == 0
        n_past = P // tkp
        if n_q == 1:
            last = pl.BlockSpec((1, hb, tkp, LANES), lambda b, h, i: (b, h, n_past - 1, 0))
            hbm = pl.BlockSpec(memory_space=pl.ANY)
            args += [k_past, v_past, k_past, v_past]
            in_specs += [last, last, hbm, hbm]
            scratch = [pltpu.VMEM((tkp, LANES), k_past.dtype), pltpu.VMEM((tkp, LANES), v_past.dtype),
                       pltpu.SemaphoreType.DMA((2,))]
        else:
            args += [k_past, v_past]
            in_specs += [full(P), full(P)]
    kern = functools.partial(_sb_kernel, tq=tq, tkp=tkp, n_past=n_past, n_q=n_q, hb=hb, qb=qb, n_prev=n_prev)
    return pl.pallas_call(
        kern,
        grid=(B, H // hb, n_q // qb),
        in_specs=in_specs,
        out_specs=pl.BlockSpec((1, tq * qb, hb * LANES), lambda b, h, i: (b, i, h)),
        scratch_shapes=scratch,
        out_shape=jax.ShapeDtypeStruct((B, L, H * LANES), BF16),
        compiler_params=pltpu.CompilerParams(dimension_semantics=("arbitrary", "arbitrary", "arbitrary"),
                                             vmem_limit_bytes=VMEM_LIMIT),
        name="stick_breaking",
    )(*args)


def _pad_lanes(v):
    return jnp.zeros((1, LANES), F32).at[0, :v.shape[0]].set(v.astype(F32))


def _even_layer(x, s0, qbuf, dbuf, wts, gain_mix, gain_mlp, w_up, w_down):
    (w_all, conv_w, alog_pad, dtb_pad, onorm, dw_w, dw_b, ln_g, ln_b, w_out, n_heads, qkv_w, v_w, c_b) = wts
    qkv, gcol, z, c, qlast, dlast = _even_pre(x, gain_mix, w_all, conv_w, alog_pad, dtb_pad, dw_w, dw_b, ln_g, ln_b,
                                              qbuf, dbuf, n_heads=n_heads, qkv_w=qkv_w, v_w=v_w, c_b=c_b)
    o, s_new = _delta(qkv, gcol, z, onorm, s0, n_heads=n_heads)
    y = _out_mlp(x, [o, c], w_out, gain_mlp, w_up, w_down)
    return y, s_new, qlast, dlast


def _odd_layer(x, k_past, v_past, wts, gain_mix, gain_mlp, w_up, w_down):
    w_qkv, qn, kn, w_out, n_heads = wts
    mxu_copies = x.shape[1] > SB_TQ
    q, k, v, *kv_mxu = _odd_pre(x, gain_mix, w_qkv, qn, kn, n_heads=n_heads, mxu_copies=mxu_copies)
    o = _stick_breaking(q, *(kv_mxu or (k, v)), k_past, v_past)
    y = _out_mlp(x, [o], w_out, gain_mlp, w_up, w_down)
    return y, k, v


def kernel(x_prompt, x_sample, state_delta, state_qkv_conv, state_dw_conv, cache_k, cache_v, norm_mix, norm_mlp, w_in_e, conv_qkv_e, a_log_e, dt_bias_e, onorm_e, dw_w_e, dw_b_e, ln_g_e, ln_b_e, w_out_e, w_qkv_o, qn_o, kn_o, w_out_o, w_up, w_down):
    depth = norm_mix.shape[0]
    bp = x_prompt.shape[0]
    xp, xs = x_prompt, x_sample
    pd, pq, pw, pk, pv = [], [], [], [], []
    sd, sq, sw, sk, sv = [], [], [], [], []
    row = lambda v: v.astype(F32).reshape(1, -1)
    for i in range(depth):
        gm, gl = row(norm_mix[i]), row(norm_mlp[i])
        wu, wd = w_up[i].astype(BF16), w_down[i].astype(BF16)
        if i % 2 == 0:
            e = i // 2
            n_heads = a_log_e.shape[1]
            qkv_w = conv_qkv_e.shape[2]
            c_b = dw_w_e.shape[2]
            v_w = n_heads * onorm_e.shape[1]
            w_in = w_in_e[e]
            ab0 = qkv_w + v_w
            ab = jnp.zeros((w_in.shape[0], LANES), w_in.dtype).at[:, :2 * n_heads].set(w_in[:, ab0:ab0 + 2 * n_heads])
            w_all = jnp.concatenate([w_in[:, :ab0], w_in[:, ab0 + 2 * n_heads:], ab], axis=1).astype(BF16)
            w_out = w_out_e[e].astype(BF16)
            wts = (w_all, conv_qkv_e[e].astype(F32), _pad_lanes(a_log_e[e]), _pad_lanes(dt_bias_e[e]),
                   row(onorm_e[e]), dw_w_e[e].astype(F32), row(dw_b_e[e]), row(ln_g_e[e]), row(ln_b_e[e]),
                   w_out, n_heads, qkv_w, v_w, c_b)
            kq, kd = conv_qkv_e.shape[1], dw_w_e.shape[1]
            s0 = jnp.zeros((bp,) + state_delta.shape[2:], F32)
            qb0 = jnp.zeros((bp, kq - 1, qkv_w), F32)
            db0 = jnp.zeros((bp, kd - 1, c_b), F32)
            xp, d1, q1, c1 = _even_layer(xp, s0, qb0, db0, wts, gm, gl, wu, wd)
            xs, d2, q2, c2 = _even_layer(xs, state_delta[e].astype(F32), state_qkv_conv[e].astype(F32),
                                         state_dw_conv[e].astype(F32), wts, gm, gl, wu, wd)
            pd.append(d1); pq.append(q1); pw.append(c1)
            sd.append(d2); sq.append(q2); sw.append(c2)
        else:
            o = i // 2
            n_heads = cache_k.shape[2]
            wts = (w_qkv_o[o].astype(BF16), row(qn_o[o]), row(kn_o[o]), w_out_o[o].astype(BF16), n_heads)
            xp, k1, v1 = _odd_layer(xp, None, None, wts, gm, gl, wu, wd)
            xs, k2, v2 = _odd_layer(xs, cache_k[o].astype(F32), cache_v[o].astype(F32), wts, gm, gl, wu, wd)
            pk.append(k1); pv.append(v1)
            sk.append(k2); sv.append(v2)
    return (xp, xs,
            jnp.stack(pd), jnp.stack(pq), jnp.stack(pw), jnp.stack(pk), jnp.stack(pv),
            jnp.stack(sd), jnp.stack(sq), jnp.stack(sw), jnp.stack(sk), jnp.stack(sv))
```

```python
import functools

import jax
import jax.numpy as jnp
from jax import lax
from jax.experimental import pallas as pl
from jax.experimental.pallas import tpu as pltpu

F32 = jnp.float32
BF16 = jnp.bfloat16
EPS = 1e-6
CHUNK = 64
LANES = 128
PAIR = 2 * CHUNK
QBUF_OFF = 8
DBUF_OFF = 32
DW_ROWS = 32
SUBLANES = 8
VMEM_LIMIT = 56 * 1024 * 1024
LOG2E = 1.4426950408889634
LOG2_W_FLOOR = -151.0
SB_TQ = 256
SB_PREV = 1
SB_KV_VMEM = 36 * 1024 * 1024
SB_CHAINS = 16
MASKED = 1e30


def _mm(a, b):
    return jnp.dot(a.astype(BF16), b.astype(BF16), preferred_element_type=F32)


def _mm_nt(a, b):
    return lax.dot_general(a.astype(BF16), b.astype(BF16), (((1,), (1,)), ((), ())),
                           preferred_element_type=F32)


def _sigmoid(x):
    return 0.5 * jnp.tanh(0.5 * x) + 0.5


def _silu(x):
    h = 0.5 * x
    return h + h * jnp.tanh(h)


def _softplus(x):
    return jnp.maximum(x, 0.0) + jnp.log1p(jnp.exp(-jnp.abs(x)))


def _rms_scale(x):
    return lax.rsqrt(jnp.mean(x * x, axis=-1, keepdims=True) + EPS)


def _const_spec(shape):
    nd = len(shape)
    return pl.BlockSpec(shape, lambda *_: (0,) * nd, pipeline_mode=pl.Buffered(1))


def _even_pre_kernel(x_ref, g_ref, w_ref, cw_ref, alog_ref, dtb_ref, dww_ref, dwb_ref, lng_ref, lnb_ref,
                     qbuf_ref, dbuf_ref,
                     qkv_ref, gcol_ref, z_ref, c_ref, qlast_ref, dlast_ref,
                     qext, uext, ushift, *, ns, tl, qkv_w, v_w, c_b, n_heads):
    kq = cw_ref.shape[0] // SUBLANES
    kd = dww_ref.shape[0] // SUBLANES
    m = ns * tl

    @pl.when(pl.program_id(1) == 0)
    def _():
        for sq in range(ns):
            qext[sq, QBUF_OFF - (kq - 1):QBUF_OFF, :] = qbuf_ref[sq]
            uext[sq, 0:DBUF_OFF - (kd - 1), :] = jnp.zeros((DBUF_OFF - (kd - 1), c_b), F32)
            uext[sq, DBUF_OFF - (kd - 1):DBUF_OFF, :] = dbuf_ref[sq]

    x = x_ref[...].reshape(m, x_ref.shape[-1])
    h = (x * _rms_scale(x) * g_ref[...]).astype(BF16)

    glu0 = qkv_w + v_w
    qext[:, QBUF_OFF:QBUF_OFF + tl, :] = jnp.dot(h, w_ref[:, 0:qkv_w],
                                                 preferred_element_type=F32).reshape(ns, tl, qkv_w)
    z_ref[...] = jnp.dot(h, w_ref[:, qkv_w:glu0], preferred_element_type=F32).reshape(ns, tl, v_w)
    ga = jnp.dot(h, w_ref[:, glu0:glu0 + c_b], preferred_element_type=F32)
    gb = jnp.dot(h, w_ref[:, glu0 + c_b:glu0 + 2 * c_b], preferred_element_type=F32)
    uext[:, DBUF_OFF:DBUF_OFF + tl, :] = (ga * _sigmoid(gb)).reshape(ns, tl, c_b)
    ab = jnp.dot(h, w_ref[:, glu0 + 2 * c_b:glu0 + 2 * c_b + LANES], preferred_element_type=F32)

    for sq in range(ns):
        for s in range(qkv_w // LANES):
            cols = slice(s * LANES, (s + 1) * LANES)
            y = None
            for j in range(kq):
                r0 = QBUF_OFF - (kq - 1) + j
                wj = cw_ref[j * SUBLANES:(j + 1) * SUBLANES, cols]
                t = wj[None] * qext[sq, r0:r0 + tl, cols].reshape(tl // SUBLANES, SUBLANES, LANES)
                y = t if y is None else y + t
            y = _silu(y.reshape(tl, LANES))
            if s < 2 * n_heads:
                y = y * lax.rsqrt(jnp.sum(y * y, axis=-1, keepdims=True) + EPS)
            qkv_ref[sq, :, cols] = y
        qlast = qext[sq, QBUF_OFF + tl - (kq - 1):QBUF_OFF + tl, :]
        qlast_ref[sq] = qlast
        qext[sq, QBUF_OFF - (kq - 1):QBUF_OFF, :] = qlast

        span = tl + DBUF_OFF - SUBLANES
        for b in range(1, SUBLANES):
            ushift[b - 1, 0:span, :] = uext[sq, b:b + span, :]
        for r in range(tl // DW_ROWS):
            acc = None
            for j in range(kd):
                a, b = divmod(DBUF_OFF - (kd - 1) + j, SUBLANES)
                r0 = a * SUBLANES + r * DW_ROWS
                win = uext[sq, r0:r0 + DW_ROWS, :] if b == 0 else ushift[b - 1, r0:r0 + DW_ROWS, :]
                wj = dww_ref[j * SUBLANES:(j + 1) * SUBLANES, :]
                t = wj[None] * win.reshape(DW_ROWS // SUBLANES, SUBLANES, c_b)
                acc = t if acc is None else acc + t
            cpre = acc.reshape(DW_ROWS, c_b) + dwb_ref[...]
            mu = jnp.mean(cpre, axis=-1, keepdims=True)
            xc = cpre - mu
            var = jnp.mean(xc * xc, axis=-1, keepdims=True)
            y = xc * lax.rsqrt(var + EPS) * lng_ref[...] + lnb_ref[...]
            c_ref[sq, r * DW_ROWS:(r + 1) * DW_ROWS, :] = _silu(y).astype(c_ref.dtype)
        dlast = uext[sq, DBUF_OFF + tl - (kd - 1):DBUF_OFF + tl, :]
        dlast_ref[sq] = dlast
        uext[sq, DBUF_OFF - (kd - 1):DBUF_OFF, :] = dlast

    lane = lax.broadcasted_iota(jnp.int32, (m, LANES), 1)
    g = -jnp.exp(alog_ref[...]) * _softplus(ab + dtb_ref[...])
    g = jnp.where(lane < n_heads, g, 0.0)
    beta = _sigmoid(ab)
    row = lax.broadcasted_iota(jnp.int32, (m, m), 0)
    col = lax.broadcasted_iota(jnp.int32, (m, m), 1)
    tri = (((row ^ col) < CHUNK) & (col <= row)).astype(BF16)
    g_hi = g.astype(BF16)
    g_r1 = g - g_hi.astype(F32)
    g_mid = g_r1.astype(BF16)
    g_lo = (g_r1 - g_mid.astype(F32)).astype(BF16)
    gcum = (jnp.dot(tri, g_hi, preferred_element_type=F32)
            + jnp.dot(tri, g_mid, preferred_element_type=F32)
            + jnp.dot(tri, g_lo, preferred_element_type=F32))
    gcol = jnp.where(lane < n_heads, gcum, jnp.where(lane < 2 * n_heads, beta, 0.0))
    gcol_ref[...] = gcol.reshape(ns, tl, LANES)


def _even_pre(x, gain, w_all, conv_w, alog_pad, dtb_pad, dw_w, dw_b, ln_g, ln_b, qbuf, dbuf, *, n_heads, qkv_w, v_w, c_b):
    n_seq, L, D = x.shape
    tl = min(256, L)
    assert L % tl == 0 and tl % DW_ROWS == 0 and tl % CHUNK == 0
    kq, kd = conv_w.shape[0], dw_w.shape[0]
    assert kq - 1 <= QBUF_OFF and kd - 1 <= DBUF_OFF and tl >= kd - 1
    conv_w = jnp.repeat(conv_w, SUBLANES, axis=0)
    dw_w = jnp.repeat(dw_w, SUBLANES, axis=0)
    nw = w_all.shape[1]
    ns = max(d for d in range(1, max(1, 256 // tl) + 1) if n_seq % d == 0)
    kern = functools.partial(_even_pre_kernel, ns=ns, tl=tl, qkv_w=qkv_w, v_w=v_w, c_b=c_b, n_heads=n_heads)
    seq_blk = lambda w: pl.BlockSpec((ns, tl, w), lambda s, j: (s, j, 0))
    per_seq = lambda r, w: pl.BlockSpec((ns, r, w), lambda s, j: (s, 0, 0))
    return pl.pallas_call(
        kern,
        grid=(n_seq // ns, L // tl),
        in_specs=[seq_blk(D), _const_spec((1, D)), _const_spec((D, nw)), _const_spec(conv_w.shape),
                  _const_spec((1, LANES)), _const_spec((1, LANES)), _const_spec(dw_w.shape),
                  _const_spec((1, c_b)), _const_spec((1, c_b)), _const_spec((1, c_b)),
                  per_seq(kq - 1, qkv_w), per_seq(kd - 1, c_b)],
        out_specs=[seq_blk(qkv_w), seq_blk(LANES), seq_blk(v_w), seq_blk(c_b),
                   per_seq(kq - 1, qkv_w), per_seq(kd - 1, c_b)],
        out_shape=[jax.ShapeDtypeStruct((n_seq, L, qkv_w), F32),
                   jax.ShapeDtypeStruct((n_seq, L, LANES), F32),
                   jax.ShapeDtypeStruct((n_seq, L, v_w), F32),
                   jax.ShapeDtypeStruct((n_seq, L, c_b), BF16),
                   jax.ShapeDtypeStruct((n_seq, kq - 1, qkv_w), F32),
                   jax.ShapeDtypeStruct((n_seq, kd - 1, c_b), F32)],
        scratch_shapes=[pltpu.VMEM((ns, QBUF_OFF + tl, qkv_w), F32), pltpu.VMEM((ns, DBUF_OFF + tl, c_b), F32),
                        pltpu.VMEM((SUBLANES - 1, DBUF_OFF + tl, c_b), F32)],
        compiler_params=pltpu.CompilerParams(dimension_semantics=("arbitrary", "arbitrary"),
                                             vmem_limit_bytes=VMEM_LIMIT),
        name="even_pre",
    )(x, gain, w_all, conv_w, alog_pad, dtb_pad, dw_w, dw_b, ln_g, ln_b, qbuf, dbuf)


def _unit_lower_inverses(a_list, row, col):
    x = row ^ col
    eye = (row == col).astype(F32)
    a8 = [jnp.where(x < 8, a, 0.0) for a in a_list]
    a8_2 = [_mm(t, t) for t in a8]
    p = [eye - t for t in a8]
    a8_4 = [_mm(t, t) for t in a8_2]
    p = [t + _mm(t, sq) for t, sq in zip(p, a8_2)]
    d = [t + _mm(t, sq) for t, sq in zip(p, a8_4)]
    s = 8
    while s < CHUNK:
        off = [jnp.where((x >= s) & (x < 2 * s), a, 0.0) for a in a_list]
        od = [_mm(o, t) for o, t in zip(off, d)]
        d = [t - _mm(t, u) for t, u in zip(d, od)]
        s *= 2
    return d


def _delta_kernel(qkv_ref, gcol_ref, z_ref, onorm_ref, sin_ref, o_ref, sout_ref, *, tb, n_heads, chained):
    dk = dv = LANES
    qk_w = n_heads * dk
    n_pairs = tb // PAIR
    if chained:
        @pl.when(pl.program_id(1) == 0)
        def _():
            sout_ref[...] = sin_ref[...]

    row = lax.broadcasted_iota(jnp.int32, (PAIR, PAIR), 0)
    col = lax.broadcasted_iota(jnp.int32, (PAIR, PAIR), 1)
    same = (row ^ col) < CHUNK
    causal = same & (col <= row)
    strict = same & (col < row)
    in_chunk = [(col >= c * CHUNK) & (col < (c + 1) * CHUNK) for c in range(2)]

    tiles = [(p, h) for p in range(n_pairs) for h in range(n_heads)]
    q, k, kb, vb, gc, a, qk = {}, {}, {}, {}, {}, {}, {}
    for p, h in tiles:
        rows = slice(p * PAIR, (p + 1) * PAIR)
        q[p, h] = qkv_ref[0, rows, h * dk:(h + 1) * dk] * (dk ** -0.5)
        k[p, h] = qkv_ref[0, rows, qk_w + h * dk:qk_w + (h + 1) * dk]
        gc[p, h] = jnp.broadcast_to(gcol_ref[0, rows, h:h + 1], (PAIR, PAIR))
        bt = jnp.broadcast_to(gcol_ref[0, rows, n_heads + h:n_heads + h + 1], (PAIR, PAIR))
        kb[p, h] = k[p, h] * bt
        vb[p, h] = qkv_ref[0, rows, 2 * qk_w + h * dv:2 * qk_w + (h + 1) * dv] * bt
    for t in tiles:
        diff = gc[t] - gc[t].T
        decay = jnp.where(causal, jnp.exp(jnp.where(causal, diff, 0.0)), 0.0)
        a[t] = jnp.where(strict, _mm_nt(kb[t], k[t]) * decay, 0.0)
        qk[t] = jnp.where(causal, _mm_nt(q[t], k[t]) * decay, 0.0)
    inv = dict(zip(tiles, _unit_lower_inverses([a[t] for t in tiles], row, col)))
    uw, kg_t, gl = {}, {}, {}
    for t in tiles:
        eg = jnp.exp(gc[t])
        uw[t] = _mm(inv[t], jnp.concatenate([vb[t], kb[t] * eg], axis=1))
        gl_rows = [jnp.broadcast_to(gc[t][(c + 1) * CHUNK - 1:(c + 1) * CHUNK, :], (PAIR, PAIR)) for c in range(2)]
        gl[t] = [jnp.exp(g) for g in gl_rows]
        gcl = jnp.where(row < CHUNK, gl_rows[0], gl_rows[1])
        kg_t[t] = (k[t] * jnp.exp(gcl - gc[t])).T
        q[t] = q[t] * eg
    qk_uw = {t: _mm(qk[t], uw[t]) for t in tiles}
    kg_uw = {(t, c): _mm(jnp.where(in_chunk[c], kg_t[t], 0.0), uw[t])
             for t in tiles for c in range(2)}

    state = [sout_ref[0, h] for h in range(n_heads)] if chained else None
    for p in range(n_pairs):
        rows = slice(p * PAIR, (p + 1) * PAIR)
        o_parts = {h: [] for h in range(n_heads)}
        for c in range(2):
            r = slice(c * CHUNK, (c + 1) * CHUNK)
            for h in range(n_heads):
                t = (p, h)
                s_old = state[h] if chained else sin_ref[c, h]
                q_eff = q[t][r] - qk_uw[t][r, dv:]
                xs = _mm(jnp.concatenate([kg_uw[t, c][:, dv:], q_eff], axis=0), s_old)
                o_parts[h].append(qk_uw[t][r, :dv] + xs[dk:])
                s_new = s_old * gl[t][c] + (kg_uw[t, c][:, :dv] - xs[:dk])
                if chained:
                    state[h] = s_new
                else:
                    sout_ref[c, h] = s_new
        for h in range(n_heads):
            o = jnp.concatenate(o_parts[h], axis=0)
            zz = z_ref[0, rows, h * dv:(h + 1) * dv].astype(F32)
            y = (o * _rms_scale(o) * onorm_ref[...]) * _silu(zz)
            o_ref[0, rows, h * dv:(h + 1) * dv] = y.astype(o_ref.dtype)
    if chained:
        for h in range(n_heads):
            sout_ref[0, h] = state[h]


def _delta(qkv, gcol, z, onorm, s0, *, n_heads):
    n_seq, L, qkv_w = qkv.shape
    v_w = z.shape[-1]
    assert qkv_w == 3 * n_heads * LANES and v_w == n_heads * LANES
    if L % PAIR == 0:
        chained, n_grp, Lg, per_grp = True, n_seq, L, 1
    else:
        assert L == CHUNK and n_seq % 2 == 0
        chained, n_grp, Lg, per_grp = False, n_seq // 2, PAIR, 2
        qkv, gcol, z = (t.reshape(n_grp, PAIR, t.shape[-1]) for t in (qkv, gcol, z))
    tb = min(512, Lg)
    assert Lg % tb == 0
    kern = functools.partial(_delta_kernel, tb=tb, n_heads=n_heads, chained=chained)
    blk = lambda w: pl.BlockSpec((1, tb, w), lambda s, j: (s, j, 0))
    st = pl.BlockSpec((per_grp, n_heads, LANES, LANES), lambda s, j: (s, 0, 0, 0))
    o, s_new = pl.pallas_call(
        kern,
        grid=(n_grp, Lg // tb),
        in_specs=[blk(qkv_w), blk(LANES), blk(v_w), _const_spec((1, LANES)), st],
        out_specs=[blk(v_w), st],
        out_shape=[jax.ShapeDtypeStruct((n_grp, Lg, v_w), BF16),
                   jax.ShapeDtypeStruct(s0.shape, F32)],
        compiler_params=pltpu.CompilerParams(dimension_semantics=("arbitrary", "arbitrary"),
                                             vmem_limit_bytes=VMEM_LIMIT),
        name="delta_rule",
    )(qkv, gcol, z, onorm, s0)
    return o.reshape(n_seq, L, v_w), s_new


def _out_mlp_kernel(*refs, n_a, steps, tf):
    n_s = len(steps)
    per = 1 + n_a
    w_ref, g_ref, wup_ref, wdn_ref = refs[n_s * per:n_s * per + 4]
    o_refs = refs[n_s * per + 4:]

    def tile(x_ref, a_refs, o_ref):
        a = a_refs[0][...] if n_a == 1 else jnp.concatenate([a_ref[...] for a_ref in a_refs], axis=-1)
        x1 = x_ref[...] + jnp.dot(a, w_ref[...], preferred_element_type=F32)
        h = (x1 * _rms_scale(x1) * g_ref[...]).astype(BF16)
        acc = x1
        for f in range(wup_ref.shape[1] // tf):
            r = jnp.maximum(jnp.dot(h, wup_ref[:, f * tf:(f + 1) * tf], preferred_element_type=F32), 0.0)
            acc = acc + jnp.dot((r * r).astype(BF16), wdn_ref[f * tf:(f + 1) * tf, :],
                                preferred_element_type=F32)
        o_ref[...] = acc

    i = pl.program_id(0)
    first = 0
    for s in range(n_s):
        run = functools.partial(tile, refs[s * per], refs[s * per + 1:(s + 1) * per], o_refs[s])
        if n_s == 1:
            run()
        else:
            pl.when((i >= first) & (i < first + steps[s]))(run)
        first += steps[s]


def _out_mlp(streams, w_out, gain, w_up, w_down):
    D = streams[0][0].shape[-1]
    flat = [(x.reshape(-1, D), [a.reshape(-1, a.shape[-1]) for a in a_list]) for x, a_list in streams]
    tm = min([512] + [x.shape[0] for x, _ in flat])
    assert all(x.shape[0] % tm == 0 for x, _ in flat)
    steps = [x.shape[0] // tm for x, _ in flat]
    n_a = len(flat[0][1])
    F = w_up.shape[1]
    tf = min(1024, F)
    assert F % tf == 0
    kern = functools.partial(_out_mlp_kernel, n_a=n_a, steps=tuple(steps), tf=tf)

    def rows(w, first, n):
        return pl.BlockSpec((tm, w), lambda i: (jnp.minimum(jnp.maximum(i - first, 0), n - 1), 0))

    in_specs, args, out_specs, first = [], [], [], 0
    for (x, a_list), n in zip(flat, steps):
        in_specs += [rows(D, first, n)] + [rows(a.shape[1], first, n) for a in a_list]
        args += [x] + a_list
        out_specs.append(rows(D, first, n))
        first += n
    outs = pl.pallas_call(
        kern,
        grid=(sum(steps),),
        in_specs=in_specs + [_const_spec(w_out.shape), _const_spec((1, D)), _const_spec(w_up.shape),
                             _const_spec(w_down.shape)],
        out_specs=out_specs,
        out_shape=[jax.ShapeDtypeStruct(x.shape, F32) for x, _ in flat],
        compiler_params=pltpu.CompilerParams(dimension_semantics=("arbitrary",), vmem_limit_bytes=VMEM_LIMIT),
        name="out_mlp",
    )(*args, w_out, gain, w_up, w_down)
    return [o.reshape(x.shape) for o, (x, _) in zip(outs, streams)]


def _odd_pre_kernel(x_ref, g_ref, w_ref, qn_ref, kn_ref, q_ref, k_ref, v_ref, *mxu_copies, n_heads):
    dh = LANES
    hd = n_heads * dh
    ns, tl, d_model = x_ref.shape
    x = x_ref[...].reshape(ns * tl, d_model)
    h = (x * _rms_scale(x) * g_ref[...]).astype(BF16)
    q = jnp.dot(h, w_ref[:, 0:hd], preferred_element_type=F32)
    k = jnp.dot(h, w_ref[:, hd:2 * hd], preferred_element_type=F32)
    v = jnp.dot(h, w_ref[:, 2 * hd:3 * hd], preferred_element_type=F32)
    per_seq = lambda t: t.reshape(ns, tl, dh)
    for hh in range(n_heads):
        cols = slice(hh * dh, (hh + 1) * dh)
        qh = q[:, cols]
        kh = k[:, cols]
        q_ref[:, hh] = per_seq((qh * _rms_scale(qh) * qn_ref[...] * (dh ** -0.5 * LOG2E)).astype(q_ref.dtype))
        kh = kh * _rms_scale(kh) * kn_ref[...]
        k_ref[:, hh] = per_seq(kh)
        v_ref[:, hh] = per_seq(v[:, cols])
        if mxu_copies:
            mxu_copies[0][:, hh] = per_seq(kh.astype(BF16))
            mxu_copies[1][:, hh] = per_seq(v[:, cols].astype(BF16))


def _odd_pre(x, gain, w_qkv, qn, kn, *, n_heads, mxu_copies):
    B, L, D = x.shape
    tl = min(256, L)
    assert L % tl == 0 and w_qkv.shape[1] == 3 * n_heads * LANES
    kern = functools.partial(_odd_pre_kernel, n_heads=n_heads)
    ns = max(d for d in range(1, max(1, 256 // tl) + 1) if B % d == 0)
    head_major = pl.BlockSpec((ns, n_heads, tl, LANES), lambda b, j: (b, 0, j, 0))
    hm_shape = (B, n_heads, L, LANES)
    n_copies = 2 if mxu_copies else 0
    return pl.pallas_call(
        kern,
        grid=(B // ns, L // tl),
        in_specs=[pl.BlockSpec((ns, tl, D), lambda b, j: (b, j, 0)), _const_spec((1, D)),
                  _const_spec(w_qkv.shape), _const_spec((1, LANES)), _const_spec((1, LANES))],
        out_specs=[head_major] * (3 + n_copies),
        out_shape=([jax.ShapeDtypeStruct(hm_shape, BF16), jax.ShapeDtypeStruct(hm_shape, F32),
                    jax.ShapeDtypeStruct(hm_shape, F32)] + [jax.ShapeDtypeStruct(hm_shape, BF16)] * n_copies),
        compiler_params=pltpu.CompilerParams(dimension_semantics=("arbitrary", "arbitrary"),
                                             vmem_limit_bytes=VMEM_LIMIT),
        name="odd_pre",
    )(x, gain, w_qkv, qn, kn)


def _suffix_ones(tk):
    j = lax.broadcasted_iota(jnp.int32, (2 * tk, tk), 0) & (tk - 1)
    s = lax.broadcasted_iota(jnp.int32, (2 * tk, tk), 1)
    return (j > s).astype(BF16)


def _sb_block(q, k_blk, v_blk, u, off, acc, below=None):
    many = isinstance(q, (list, tuple))
    qs, ks, vs, accs = (q, k_blk, v_blk, acc) if many else ([q], [k_blk], [v_blk], [acc])
    zs = [_mm_nt(a, b) for a, b in zip(qs, ks)]
    z = jnp.concatenate(zs, axis=0) if many else zs[0]
    t = jnp.maximum(z, 0.0) + jnp.log(1.0 + jnp.exp2(-jnp.abs(z))) * LOG2E
    if below is not None:
        t = jnp.where(below, t, 0.0)
    t_hi = t.astype(BF16)
    t_lo = (t - t_hi.astype(F32)).astype(BF16)
    later = jnp.dot(jnp.concatenate([t_hi, t_lo], axis=1), u, preferred_element_type=F32)
    wts = jnp.exp2(z - t - (later + off))
    rows = qs[0].shape[0]
    accs = [a + _mm(wts[n * rows:(n + 1) * rows], v) for n, (a, v) in enumerate(zip(accs, vs))]
    return later[:, 0:1] + t[:, 0:1], accs if many else accs[0]


def _sb_kernel(*refs, tq, tkp, n_past, n_q, hb, qb, n_prev):
    lazy_past = bool(n_past) and n_q == 1
    if lazy_past:
        q_ref, kn_ref, vn_ref, kl_ref, vl_ref, kp_ref, vp_ref, o_ref, kbuf, vbuf, sem = refs
    elif n_past:
        q_ref, kn_ref, vn_ref, kp_ref, vp_ref, o_ref = refs
    else:
        q_ref, kn_ref, vn_ref, o_ref = refs
    u_self = _suffix_ones(tq)
    u_past = _suffix_ones(tkp) if n_past else None
    row = lax.broadcasted_iota(jnp.int32, (tq, tq), 0)
    col = lax.broadcasted_iota(jnp.int32, (tq, tq), 1)
    below = col < row
    not_below = jnp.where(below, 0.0, MASKED)

    walks = []
    if n_q == 1:
        heads = range(hb)
        qs = [q_ref[0, hh] for hh in heads]
        below_all = jnp.concatenate([below] * hb, axis=0)
        spent, accs = _sb_block(qs, [kn_ref[0, hh] for hh in heads], [vn_ref[0, hh] for hh in heads], u_self,
                                jnp.concatenate([not_below] * hb, axis=0),
                                [jnp.zeros((tq, LANES), F32)] * hb, below_all)
        if lazy_past:
            own, accs = _sb_block(qs, [kl_ref[0, hh] for hh in heads], [vl_ref[0, hh] for hh in heads], u_past,
                                  spent, accs)
            spent = spent + own
        walks = [(hh, 0, qs[hh], spent[hh * tq:(hh + 1) * tq], accs[hh], -1, n_past - 2) for hh in heads]
    for hh, w in [(hh, w) for hh in range(hb) for w in range(qb) if n_q > 1]:
        i = pl.program_id(2) * qb + w
        start = pl.multiple_of(i * tq, tq)
        q = q_ref[0, hh, w * tq:(w + 1) * tq, :]
        acc = jnp.zeros((tq, LANES), F32)
        spent, acc = _sb_block(q, kn_ref[0, hh, pl.ds(start, tq), :], vn_ref[0, hh, pl.ds(start, tq), :],
                               u_self, not_below, acc, below)
        first_self, first_past = i - 1, n_past - 1
        for d in range(1, n_prev + 1):
            prev = pl.multiple_of(jnp.maximum(i - d, 0) * tq, tq)
            has_prev = jnp.full((tq, 1), i, jnp.int32) >= d
            own, acc = _sb_block(q, kn_ref[0, hh, pl.ds(prev, tq), :], vn_ref[0, hh, pl.ds(prev, tq), :],
                                 u_self, jnp.where(has_prev, spent, MASKED), acc)
            spent = spent + jnp.where(has_prev, own, 0.0)
            first_self = i - d - 1
        if lazy_past:
            own, acc = _sb_block(q, kl_ref[0, hh], vl_ref[0, hh], u_past, spent, acc)
            spent = spent + own
            first_past = n_past - 2
        walks.append((hh, w, q, spent, acc, first_self, first_past))

    def walk(q, k_ref, v_ref, hh, tk, u, first, spent, acc, from_hbm=False):
        def cond(st):
            return (st[0] >= 0) & (jnp.min(st[1]) < -LOG2_W_FLOOR)

        def body(st):
            j, spent, acc = st
            s0 = pl.multiple_of(j * tk, tk)
            if from_hbm:
                head = pl.program_id(1) * hb + hh
                copies = [pltpu.make_async_copy(src.at[pl.program_id(0), head, pl.ds(s0, tk), :], dst, sem.at[n])
                          for n, (src, dst) in enumerate(((k_ref, kbuf), (v_ref, vbuf)))]
                for cp in copies:
                    cp.start()
                for cp in copies:
                    cp.wait()
                k_blk, v_blk = kbuf[...], vbuf[...]
            else:
                k_blk, v_blk = k_ref[0, hh, pl.ds(s0, tk), :], v_ref[0, hh, pl.ds(s0, tk), :]
            own, acc = _sb_block(q, k_blk, v_blk, u, spent, acc)
            return j - 1, spent + own, acc

        _, spent, acc = lax.while_loop(cond, body, (first, spent, acc))
        return spent, acc

    for hh, w, _, _, acc, _, _ in walks:
        o_ref[0, w * tq:(w + 1) * tq, hh * LANES:(hh + 1) * LANES] = acc.astype(o_ref.dtype)
    if n_q == 1 and n_past <= 1:
        return

    least = functools.reduce(jnp.minimum, [wk[3] for wk in walks])

    @pl.when(jnp.min(least) < -LOG2_W_FLOOR)
    def _():
        for hh, w, q, spent, acc, first_self, first_past in walks:
            if n_q > 1:
                spent, acc = walk(q, kn_ref, vn_ref, hh, tq, u_self, first_self, spent, acc)
            if n_past:
                spent, acc = walk(q, kp_ref, vp_ref, hh, tkp, u_past, jnp.int32(first_past), spent, acc,
                                  from_hbm=lazy_past)
            o_ref[0, w * tq:(w + 1) * tq, hh * LANES:(hh + 1) * LANES] = acc.astype(o_ref.dtype)


def _stick_breaking(q, k_new, v_new, k_past, v_past):
    B, H, L, dh = q.shape
    assert dh == LANES
    tq = min(SB_TQ, L)
    assert L % tq == 0
    n_q = L // tq
    assert tq & (tq - 1) == 0
    if n_q == 1:
        hb, qb = H, 1
    else:
        kv_bytes = 2 * 2 * L * LANES * k_new.dtype.itemsize
        hb = max(d for d in (4, 2, 1) if H % d == 0 and d * kv_bytes <= SB_KV_VMEM)
        qb = max(d for d in (SB_CHAINS // hb, 2, 1) if n_q % d == 0)
    n_prev = 0 if n_q == 1 else SB_PREV
    assert H % hb == 0 and n_q % qb == 0
    args = [q, k_new, v_new]
    full = lambda n: pl.BlockSpec((1, hb, n, LANES), lambda b, h, i: (b, h, 0, 0))
    in_specs = [pl.BlockSpec((1, hb, tq * qb, LANES), lambda b, h, i: (b, h, i, 0)), full(L), full(L)]
    n_past, tkp = 0, 0
    scratch = []
    if k_past is not None:
        P = k_past.shape[2]
        tkp = min(256, P)
        assert P % tkp == 0 and tkp & (tkp - 1) == 0
        n_past = P // tkp
        if n_q == 1:
            last = pl.BlockSpec((1, hb, tkp, LANES), lambda b, h, i: (b, h, n_past - 1, 0))
            hbm = pl.BlockSpec(memory_space=pl.ANY)
            args += [k_past, v_past, k_past, v_past]
            in_specs += [last, last, hbm, hbm]
            scratch = [pltpu.VMEM((tkp, LANES), k_past.dtype), pltpu.VMEM((tkp, LANES), v_past.dtype),
                       pltpu.SemaphoreType.DMA((2,))]
        else:
            args += [k_past, v_past]
            in_specs += [full(P), full(P)]
    kern = functools.partial(_sb_kernel, tq=tq, tkp=tkp, n_past=n_past, n_q=n_q, hb=hb, qb=qb, n_prev=n_prev)
    return pl.pallas_call(
        kern,
        grid=(B, H // hb, n_q // qb),
        in_specs=in_specs,
        out_specs=pl.BlockSpec((1, tq * qb, hb * LANES), lambda b, h, i: (b, i, h)),
        scratch_shapes=scratch,
        out_shape=jax.ShapeDtypeStruct((B, L, H * LANES), BF16),
        compiler_params=pltpu.CompilerParams(dimension_semantics=("arbitrary", "arbitrary", "arbitrary"),
                                             vmem_limit_bytes=VMEM_LIMIT),
        name="stick_breaking",
    )(*args)


def _pad_lanes(v):
    return jnp.zeros((1, LANES), F32).at[0, :v.shape[0]].set(v.astype(F32))


def _even_mixers(x, s0, qbuf, dbuf, wts, gain_mix):
    (w_all, conv_w, alog_pad, dtb_pad, onorm, dw_w, dw_b, ln_g, ln_b, n_heads, qkv_w, v_w, c_b) = wts
    qkv, gcol, z, c, qlast, dlast = _even_pre(x, gain_mix, w_all, conv_w, alog_pad, dtb_pad, dw_w, dw_b, ln_g, ln_b,
                                              qbuf, dbuf, n_heads=n_heads, qkv_w=qkv_w, v_w=v_w, c_b=c_b)
    o, s_new = _delta(qkv, gcol, z, onorm, s0, n_heads=n_heads)
    return [o, c], s_new, qlast, dlast


def _odd_mixer(x, k_past, v_past, wts, gain_mix):
    w_qkv, qn, kn, n_heads = wts
    mxu_copies = x.shape[1] > SB_TQ
    q, k, v, *kv_mxu = _odd_pre(x, gain_mix, w_qkv, qn, kn, n_heads=n_heads, mxu_copies=mxu_copies)
    o = _stick_breaking(q, *(kv_mxu or (k, v)), k_past, v_past)
    return [o], k, v


def kernel(x_prompt, x_sample, state_delta, state_qkv_conv, state_dw_conv, cache_k, cache_v, norm_mix, norm_mlp, w_in_e, conv_qkv_e, a_log_e, dt_bias_e, onorm_e, dw_w_e, dw_b_e, ln_g_e, ln_b_e, w_out_e, w_qkv_o, qn_o, kn_o, w_out_o, w_up, w_down):
    depth = norm_mix.shape[0]
    bp = x_prompt.shape[0]
    xp, xs = x_prompt, x_sample
    pd, pq, pw, pk, pv = [], [], [], [], []
    sd, sq, sw, sk, sv = [], [], [], [], []
    row = lambda v: v.astype(F32).reshape(1, -1)
    for i in range(depth):
        gm, gl = row(norm_mix[i]), row(norm_mlp[i])
        wu, wd = w_up[i].astype(BF16), w_down[i].astype(BF16)
        if i % 2 == 0:
            e = i // 2
            n_heads = a_log_e.shape[1]
            qkv_w = conv_qkv_e.shape[2]
            c_b = dw_w_e.shape[2]
            v_w = n_heads * onorm_e.shape[1]
            w_in = w_in_e[e]
            ab0 = qkv_w + v_w
            ab = jnp.zeros((w_in.shape[0], LANES), w_in.dtype).at[:, :2 * n_heads].set(w_in[:, ab0:ab0 + 2 * n_heads])
            w_all = jnp.concatenate([w_in[:, :ab0], w_in[:, ab0 + 2 * n_heads:], ab], axis=1).astype(BF16)
            w_out = w_out_e[e].astype(BF16)
            wts = (w_all, conv_qkv_e[e].astype(F32), _pad_lanes(a_log_e[e]), _pad_lanes(dt_bias_e[e]),
                   row(onorm_e[e]), dw_w_e[e].astype(F32), row(dw_b_e[e]), row(ln_g_e[e]), row(ln_b_e[e]),
                   n_heads, qkv_w, v_w, c_b)
            kq, kd = conv_qkv_e.shape[1], dw_w_e.shape[1]
            s0 = jnp.zeros((bp,) + state_delta.shape[2:], F32)
            qb0 = jnp.zeros((bp, kq - 1, qkv_w), F32)
            db0 = jnp.zeros((bp, kd - 1, c_b), F32)
            ap, d1, q1, c1 = _even_mixers(xp, s0, qb0, db0, wts, gm)
            as_, d2, q2, c2 = _even_mixers(xs, state_delta[e].astype(F32), state_qkv_conv[e].astype(F32),
                                           state_dw_conv[e].astype(F32), wts, gm)
            pd.append(d1); pq.append(q1); pw.append(c1)
            sd.append(d2); sq.append(q2); sw.append(c2)
        else:
            o = i // 2
            n_heads = cache_k.shape[2]
            w_out = w_out_o[o].astype(BF16)
            wts = (w_qkv_o[o].astype(BF16), row(qn_o[o]), row(kn_o[o]), n_heads)
            ap, k1, v1 = _odd_mixer(xp, None, None, wts, gm)
            as_, k2, v2 = _odd_mixer(xs, cache_k[o].astype(F32), cache_v[o].astype(F32), wts, gm)
            pk.append(k1); pv.append(v1)
            sk.append(k2); sv.append(v2)
        xp, xs = _out_mlp([(xp, ap), (xs, as_)], w_out, gl, wu, wd)
    return (xp, xs,
            jnp.stack(pd), jnp.stack(pq), jnp.stack(pw), jnp.stack(pk), jnp.stack(pv),
            jnp.stack(sd), jnp.stack(sq), jnp.stack(sw), jnp.stack(sk), jnp.stack(sv))
```

```python
import functools

import jax
import jax.numpy as jnp
from jax import lax
from jax.experimental import pallas as pl
from jax.experimental.pallas import tpu as pltpu

F32 = jnp.float32
BF16 = jnp.bfloat16
EPS = 1e-6
CHUNK = 64
LANES = 128
PAIR = 2 * CHUNK
QBUF_OFF = 8
DBUF_OFF = 32
DW_ROWS = 32
SUBLANES = 8
VMEM_LIMIT = 56 * 1024 * 1024
LOG2E = 1.4426950408889634
LOG2_W_FLOOR = -151.0
SB_TQ = 256
SB_PREV = 1
SB_KV_VMEM = 36 * 1024 * 1024
SB_CHAINS = 16
MASKED = 1e30


def _mm(a, b):
    return jnp.dot(a.astype(BF16), b.astype(BF16), preferred_element_type=F32)


def _mm_nt(a, b):
    return lax.dot_general(a.astype(BF16), b.astype(BF16), (((1,), (1,)), ((), ())),
                           preferred_element_type=F32)


def _sigmoid(x):
    return 0.5 * jnp.tanh(0.5 * x) + 0.5


def _silu(x):
    h = 0.5 * x
    return h + h * jnp.tanh(h)


def _softplus(x):
    return jnp.maximum(x, 0.0) + jnp.log1p(jnp.exp(-jnp.abs(x)))


def _rms_scale(x):
    return lax.rsqrt(jnp.mean(x * x, axis=-1, keepdims=True) + EPS)


def _const_spec(shape):
    nd = len(shape)
    return pl.BlockSpec(shape, lambda *_: (0,) * nd, pipeline_mode=pl.Buffered(1))


def _even_pre_kernel(x_ref, g_ref, w_ref, cw_ref, alog_ref, dtb_ref, dww_ref, dwb_ref, lng_ref, lnb_ref,
                     qbuf_ref, dbuf_ref,
                     qkv_ref, gcol_ref, z_ref, c_ref, qlast_ref, dlast_ref,
                     qext, uext, ushift, *, ns, tl, qkv_w, v_w, c_b, n_heads):
    kq = cw_ref.shape[0] // SUBLANES
    kd = dww_ref.shape[0] // SUBLANES
    m = ns * tl

    @pl.when(pl.program_id(1) == 0)
    def _():
        for sq in range(ns):
            qext[sq, QBUF_OFF - (kq - 1):QBUF_OFF, :] = qbuf_ref[sq]
            uext[sq, 0:DBUF_OFF - (kd - 1), :] = jnp.zeros((DBUF_OFF - (kd - 1), c_b), F32)
            uext[sq, DBUF_OFF - (kd - 1):DBUF_OFF, :] = dbuf_ref[sq]

    x = x_ref[...].reshape(m, x_ref.shape[-1])
    h = (x * _rms_scale(x) * g_ref[...]).astype(BF16)

    glu0 = qkv_w + v_w
    qext[:, QBUF_OFF:QBUF_OFF + tl, :] = jnp.dot(h, w_ref[:, 0:qkv_w],
                                                 preferred_element_type=F32).reshape(ns, tl, qkv_w)
    z_ref[...] = jnp.dot(h, w_ref[:, qkv_w:glu0], preferred_element_type=F32).reshape(ns, tl, v_w)
    ga = jnp.dot(h, w_ref[:, glu0:glu0 + c_b], preferred_element_type=F32)
    gb = jnp.dot(h, w_ref[:, glu0 + c_b:glu0 + 2 * c_b], preferred_element_type=F32)
    uext[:, DBUF_OFF:DBUF_OFF + tl, :] = (ga * _sigmoid(gb)).reshape(ns, tl, c_b)
    ab = jnp.dot(h, w_ref[:, glu0 + 2 * c_b:glu0 + 2 * c_b + LANES], preferred_element_type=F32)

    for sq in range(ns):
        for s in range(qkv_w // LANES):
            cols = slice(s * LANES, (s + 1) * LANES)
            y = None
            for j in range(kq):
                r0 = QBUF_OFF - (kq - 1) + j
                wj = cw_ref[j * SUBLANES:(j + 1) * SUBLANES, cols]
                t = wj[None] * qext[sq, r0:r0 + tl, cols].reshape(tl // SUBLANES, SUBLANES, LANES)
                y = t if y is None else y + t
            y = _silu(y.reshape(tl, LANES))
            if s < 2 * n_heads:
                y = y * lax.rsqrt(jnp.sum(y * y, axis=-1, keepdims=True) + EPS)
            qkv_ref[sq, :, cols] = y
        qlast = qext[sq, QBUF_OFF + tl - (kq - 1):QBUF_OFF + tl, :]
        qlast_ref[sq] = qlast
        qext[sq, QBUF_OFF - (kq - 1):QBUF_OFF, :] = qlast

        span = tl + DBUF_OFF - SUBLANES
        for b in range(1, SUBLANES):
            ushift[b - 1, 0:span, :] = uext[sq, b:b + span, :]
        for r in range(tl // DW_ROWS):
            acc = None
            for j in range(kd):
                a, b = divmod(DBUF_OFF - (kd - 1) + j, SUBLANES)
                r0 = a * SUBLANES + r * DW_ROWS
                win = uext[sq, r0:r0 + DW_ROWS, :] if b == 0 else ushift[b - 1, r0:r0 + DW_ROWS, :]
                wj = dww_ref[j * SUBLANES:(j + 1) * SUBLANES, :]
                t = wj[None] * win.reshape(DW_ROWS // SUBLANES, SUBLANES, c_b)
                acc = t if acc is None else acc + t
            cpre = acc.reshape(DW_ROWS, c_b) + dwb_ref[...]
            mu = jnp.mean(cpre, axis=-1, keepdims=True)
            xc = cpre - mu
            var = jnp.mean(xc * xc, axis=-1, keepdims=True)
            y = xc * lax.rsqrt(var + EPS) * lng_ref[...] + lnb_ref[...]
            c_ref[sq, r * DW_ROWS:(r + 1) * DW_ROWS, :] = _silu(y).astype(c_ref.dtype)
        dlast = uext[sq, DBUF_OFF + tl - (kd - 1):DBUF_OFF + tl, :]
        dlast_ref[sq] = dlast
        uext[sq, DBUF_OFF - (kd - 1):DBUF_OFF, :] = dlast

    lane = lax.broadcasted_iota(jnp.int32, (m, LANES), 1)
    g = -jnp.exp(alog_ref[...]) * _softplus(ab + dtb_ref[...])
    g = jnp.where(lane < n_heads, g, 0.0)
    beta = _sigmoid(ab)
    row = lax.broadcasted_iota(jnp.int32, (m, m), 0)
    col = lax.broadcasted_iota(jnp.int32, (m, m), 1)
    tri = (((row ^ col) < CHUNK) & (col <= row)).astype(BF16)
    g_hi = g.astype(BF16)
    g_r1 = g - g_hi.astype(F32)
    g_mid = g_r1.astype(BF16)
    g_lo = (g_r1 - g_mid.astype(F32)).astype(BF16)
    gcum = (jnp.dot(tri, g_hi, preferred_element_type=F32)
            + jnp.dot(tri, g_mid, preferred_element_type=F32)
            + jnp.dot(tri, g_lo, preferred_element_type=F32))
    gcol = jnp.where(lane < n_heads, gcum, jnp.where(lane < 2 * n_heads, beta, 0.0))
    gcol_ref[...] = gcol.reshape(ns, tl, LANES)


def _even_pre(x, gain, w_all, conv_w, alog_pad, dtb_pad, dw_w, dw_b, ln_g, ln_b, qbuf, dbuf, *, n_heads, qkv_w, v_w, c_b):
    n_seq, L, D = x.shape
    tl = min(256, L)
    assert L % tl == 0 and tl % DW_ROWS == 0 and tl % CHUNK == 0
    kq, kd = conv_w.shape[0], dw_w.shape[0]
    assert kq - 1 <= QBUF_OFF and kd - 1 <= DBUF_OFF and tl >= kd - 1
    conv_w = jnp.repeat(conv_w, SUBLANES, axis=0)
    dw_w = jnp.repeat(dw_w, SUBLANES, axis=0)
    nw = w_all.shape[1]
    ns = max(d for d in range(1, max(1, 256 // tl) + 1) if n_seq % d == 0)
    kern = functools.partial(_even_pre_kernel, ns=ns, tl=tl, qkv_w=qkv_w, v_w=v_w, c_b=c_b, n_heads=n_heads)
    seq_blk = lambda w: pl.BlockSpec((ns, tl, w), lambda s, j: (s, j, 0))
    per_seq = lambda r, w: pl.BlockSpec((ns, r, w), lambda s, j: (s, 0, 0))
    return pl.pallas_call(
        kern,
        grid=(n_seq // ns, L // tl),
        in_specs=[seq_blk(D), _const_spec((1, D)), _const_spec((D, nw)), _const_spec(conv_w.shape),
                  _const_spec((1, LANES)), _const_spec((1, LANES)), _const_spec(dw_w.shape),
                  _const_spec((1, c_b)), _const_spec((1, c_b)), _const_spec((1, c_b)),
                  per_seq(kq - 1, qkv_w), per_seq(kd - 1, c_b)],
        out_specs=[seq_blk(qkv_w), seq_blk(LANES), seq_blk(v_w), seq_blk(c_b),
                   per_seq(kq - 1, qkv_w), per_seq(kd - 1, c_b)],
        out_shape=[jax.ShapeDtypeStruct((n_seq, L, qkv_w), F32),
                   jax.ShapeDtypeStruct((n_seq, L, LANES), F32),
                   jax.ShapeDtypeStruct((n_seq, L, v_w), F32),
                   jax.ShapeDtypeStruct((n_seq, L, c_b), BF16),
                   jax.ShapeDtypeStruct((n_seq, kq - 1, qkv_w), F32),
                   jax.ShapeDtypeStruct((n_seq, kd - 1, c_b), F32)],
        scratch_shapes=[pltpu.VMEM((ns, QBUF_OFF + tl, qkv_w), F32), pltpu.VMEM((ns, DBUF_OFF + tl, c_b), F32),
                        pltpu.VMEM((SUBLANES - 1, DBUF_OFF + tl, c_b), F32)],
        compiler_params=pltpu.CompilerParams(dimension_semantics=("arbitrary", "arbitrary"),
                                             vmem_limit_bytes=VMEM_LIMIT),
        name="even_pre",
    )(x, gain, w_all, conv_w, alog_pad, dtb_pad, dw_w, dw_b, ln_g, ln_b, qbuf, dbuf)


def _unit_lower_inverses(a_list, row, col):
    x = row ^ col
    eye = (row == col).astype(F32)
    a8 = [jnp.where(x < 8, a, 0.0) for a in a_list]
    a8_2 = [_mm(t, t) for t in a8]
    p = [eye - t for t in a8]
    a8_4 = [_mm(t, t) for t in a8_2]
    p = [t + _mm(t, sq) for t, sq in zip(p, a8_2)]
    d = [t + _mm(t, sq) for t, sq in zip(p, a8_4)]
    s = 8
    while s < CHUNK:
        off = [jnp.where((x >= s) & (x < 2 * s), a, 0.0) for a in a_list]
        od = [_mm(o, t) for o, t in zip(off, d)]
        d = [t - _mm(t, u) for t, u in zip(d, od)]
        s *= 2
    return d


def _delta_kernel(qkv_ref, gcol_ref, z_ref, onorm_ref, sin_ref, o_ref, sout_ref, *, tb, n_heads, chained):
    dk = dv = LANES
    qk_w = n_heads * dk
    n_pairs = tb // PAIR
    if chained:
        @pl.when(pl.program_id(1) == 0)
        def _():
            sout_ref[...] = sin_ref[...]

    row = lax.broadcasted_iota(jnp.int32, (PAIR, PAIR), 0)
    col = lax.broadcasted_iota(jnp.int32, (PAIR, PAIR), 1)
    same = (row ^ col) < CHUNK
    causal = same & (col <= row)
    strict = same & (col < row)
    in_chunk = [(col >= c * CHUNK) & (col < (c + 1) * CHUNK) for c in range(2)]

    tiles = [(p, h) for p in range(n_pairs) for h in range(n_heads)]
    q, k, kb, vb, gc, a, qk = {}, {}, {}, {}, {}, {}, {}
    for p, h in tiles:
        rows = slice(p * PAIR, (p + 1) * PAIR)
        q[p, h] = qkv_ref[0, rows, h * dk:(h + 1) * dk] * (dk ** -0.5)
        k[p, h] = qkv_ref[0, rows, qk_w + h * dk:qk_w + (h + 1) * dk]
        gc[p, h] = jnp.broadcast_to(gcol_ref[0, rows, h:h + 1], (PAIR, PAIR))
        bt = jnp.broadcast_to(gcol_ref[0, rows, n_heads + h:n_heads + h + 1], (PAIR, PAIR))
        kb[p, h] = k[p, h] * bt
        vb[p, h] = qkv_ref[0, rows, 2 * qk_w + h * dv:2 * qk_w + (h + 1) * dv] * bt
    for t in tiles:
        diff = gc[t] - gc[t].T
        decay = jnp.where(causal, jnp.exp(jnp.where(causal, diff, 0.0)), 0.0)
        a[t] = jnp.where(strict, _mm_nt(kb[t], k[t]) * decay, 0.0)
        qk[t] = jnp.where(causal, _mm_nt(q[t], k[t]) * decay, 0.0)
    inv = dict(zip(tiles, _unit_lower_inverses([a[t] for t in tiles], row, col)))
    uw, kg_t, gl = {}, {}, {}
    for t in tiles:
        eg = jnp.exp(gc[t])
        uw[t] = _mm(inv[t], jnp.concatenate([vb[t], kb[t] * eg], axis=1))
        gl_rows = [jnp.broadcast_to(gc[t][(c + 1) * CHUNK - 1:(c + 1) * CHUNK, :], (PAIR, PAIR)) for c in range(2)]
        gl[t] = [jnp.exp(g) for g in gl_rows]
        gcl = jnp.where(row < CHUNK, gl_rows[0], gl_rows[1])
        kg_t[t] = (k[t] * jnp.exp(gcl - gc[t])).T
        q[t] = q[t] * eg
    qk_uw = {t: _mm(qk[t], uw[t]) for t in tiles}
    kg_uw = {(t, c): _mm(jnp.where(in_chunk[c], kg_t[t], 0.0), uw[t])
             for t in tiles for c in range(2)}

    state = [sout_ref[0, h] for h in range(n_heads)] if chained else None
    for p in range(n_pairs):
        rows = slice(p * PAIR, (p + 1) * PAIR)
        o_parts = {h: [] for h in range(n_heads)}
        for c in range(2):
            r = slice(c * CHUNK, (c + 1) * CHUNK)
            for h in range(n_heads):
                t = (p, h)
                s_old = state[h] if chained else sin_ref[c, h]
                q_eff = q[t][r] - qk_uw[t][r, dv:]
                xs = _mm(jnp.concatenate([kg_uw[t, c][:, dv:], q_eff], axis=0), s_old)
                o_parts[h].append(qk_uw[t][r, :dv] + xs[dk:])
                s_new = s_old * gl[t][c] + (kg_uw[t, c][:, :dv] - xs[:dk])
                if chained:
                    state[h] = s_new
                else:
                    sout_ref[c, h] = s_new
        for h in range(n_heads):
            o = jnp.concatenate(o_parts[h], axis=0)
            zz = z_ref[0, rows, h * dv:(h + 1) * dv].astype(F32)
            y = (o * _rms_scale(o) * onorm_ref[...]) * _silu(zz)
            o_ref[0, rows, h * dv:(h + 1) * dv] = y.astype(o_ref.dtype)
    if chained:
        for h in range(n_heads):
            sout_ref[0, h] = state[h]


def _delta(qkv, gcol, z, onorm, s0, *, n_heads):
    n_seq, L, qkv_w = qkv.shape
    v_w = z.shape[-1]
    assert qkv_w == 3 * n_heads * LANES and v_w == n_heads * LANES
    if L % PAIR == 0:
        chained, n_grp, Lg, per_grp = True, n_seq, L, 1
    else:
        assert L == CHUNK and n_seq % 2 == 0
        chained, n_grp, Lg, per_grp = False, n_seq // 2, PAIR, 2
        qkv, gcol, z = (t.reshape(n_grp, PAIR, t.shape[-1]) for t in (qkv, gcol, z))
    tb = min(512, Lg)
    assert Lg % tb == 0
    kern = functools.partial(_delta_kernel, tb=tb, n_heads=n_heads, chained=chained)
    blk = lambda w: pl.BlockSpec((1, tb, w), lambda s, j: (s, j, 0))
    st = pl.BlockSpec((per_grp, n_heads, LANES, LANES), lambda s, j: (s, 0, 0, 0))
    o, s_new = pl.pallas_call(
        kern,
        grid=(n_grp, Lg // tb),
        in_specs=[blk(qkv_w), blk(LANES), blk(v_w), _const_spec((1, LANES)), st],
        out_specs=[blk(v_w), st],
        out_shape=[jax.ShapeDtypeStruct((n_grp, Lg, v_w), BF16),
                   jax.ShapeDtypeStruct(s0.shape, F32)],
        compiler_params=pltpu.CompilerParams(dimension_semantics=("arbitrary", "arbitrary"),
                                             vmem_limit_bytes=VMEM_LIMIT),
        name="delta_rule",
    )(qkv, gcol, z, onorm, s0)
    return o.reshape(n_seq, L, v_w), s_new


def _out_mlp_kernel(*refs, n_a, steps, tf):
    n_s = len(steps)
    per = 1 + n_a
    w_ref, g_ref, wup_ref, wdn_ref = refs[n_s * per:n_s * per + 4]
    o_refs = refs[n_s * per + 4:]

    def tile(x_ref, a_refs, o_ref):
        a = a_refs[0][...] if n_a == 1 else jnp.concatenate([a_ref[...] for a_ref in a_refs], axis=-1)
        x1 = x_ref[...] + jnp.dot(a, w_ref[...], preferred_element_type=F32)
        h = (x1 * _rms_scale(x1) * g_ref[...]).astype(BF16)
        acc = x1
        for f in range(wup_ref.shape[1] // tf):
            r = jnp.maximum(jnp.dot(h, wup_ref[:, f * tf:(f + 1) * tf], preferred_element_type=F32), 0.0)
            acc = acc + jnp.dot((r * r).astype(BF16), wdn_ref[f * tf:(f + 1) * tf, :],
                                preferred_element_type=F32)
        o_ref[...] = acc

    i = pl.program_id(0)
    first = 0
    for s in range(n_s):
        run = functools.partial(tile, refs[s * per], refs[s * per + 1:(s + 1) * per], o_refs[s])
        if n_s == 1:
            run()
        else:
            pl.when((i >= first) & (i < first + steps[s]))(run)
        first += steps[s]


def _out_mlp(streams, w_out, gain, w_up, w_down):
    D = streams[0][0].shape[-1]
    flat = [(x.reshape(-1, D), [a.reshape(-1, a.shape[-1]) for a in a_list]) for x, a_list in streams]
    tm = min([512] + [x.shape[0] for x, _ in flat])
    assert all(x.shape[0] % tm == 0 for x, _ in flat)
    steps = [x.shape[0] // tm for x, _ in flat]
    n_a = len(flat[0][1])
    F = w_up.shape[1]
    tf = min(1024, F)
    assert F % tf == 0
    kern = functools.partial(_out_mlp_kernel, n_a=n_a, steps=tuple(steps), tf=tf)

    def rows(w, first, n):
        return pl.BlockSpec((tm, w), lambda i: (jnp.minimum(jnp.maximum(i - first, 0), n - 1), 0))

    in_specs, args, out_specs, first = [], [], [], 0
    for (x, a_list), n in zip(flat, steps):
        in_specs += [rows(D, first, n)] + [rows(a.shape[1], first, n) for a in a_list]
        args += [x] + a_list
        out_specs.append(rows(D, first, n))
        first += n
    outs = pl.pallas_call(
        kern,
        grid=(sum(steps),),
        in_specs=in_specs + [_const_spec(w_out.shape), _const_spec((1, D)), _const_spec(w_up.shape),
                             _const_spec(w_down.shape)],
        out_specs=out_specs,
        out_shape=[jax.ShapeDtypeStruct(x.shape, F32) for x, _ in flat],
        compiler_params=pltpu.CompilerParams(dimension_semantics=("arbitrary",), vmem_limit_bytes=VMEM_LIMIT),
        name="out_mlp",
    )(*args, w_out, gain, w_up, w_down)
    return [o.reshape(x.shape) for o, (x, _) in zip(outs, streams)]


def _odd_pre_kernel(x_ref, g_ref, w_ref, qn_ref, kn_ref, q_ref, k_ref, v_ref, *mxu_copies, n_heads):
    dh = LANES
    hd = n_heads * dh
    ns, tl, d_model = x_ref.shape
    x = x_ref[...].reshape(ns * tl, d_model)
    h = (x * _rms_scale(x) * g_ref[...]).astype(BF16)
    q = jnp.dot(h, w_ref[:, 0:hd], preferred_element_type=F32)
    k = jnp.dot(h, w_ref[:, hd:2 * hd], preferred_element_type=F32)
    v = jnp.dot(h, w_ref[:, 2 * hd:3 * hd], preferred_element_type=F32)
    per_seq = lambda t: t.reshape(ns, tl, dh)
    for hh in range(n_heads):
        cols = slice(hh * dh, (hh + 1) * dh)
        qh = q[:, cols]
        kh = k[:, cols]
        q_ref[:, hh] = per_seq((qh * _rms_scale(qh) * qn_ref[...] * (dh ** -0.5 * LOG2E)).astype(q_ref.dtype))
        kh = kh * _rms_scale(kh) * kn_ref[...]
        k_ref[:, hh] = per_seq(kh)
        v_ref[:, hh] = per_seq(v[:, cols])
        if mxu_copies:
            mxu_copies[0][:, hh] = per_seq(kh.astype(BF16))
            mxu_copies[1][:, hh] = per_seq(v[:, cols].astype(BF16))


def _odd_pre(x, gain, w_qkv, qn, kn, *, n_heads, mxu_copies):
    B, L, D = x.shape
    tl = min(256, L)
    assert L % tl == 0 and w_qkv.shape[1] == 3 * n_heads * LANES
    kern = functools.partial(_odd_pre_kernel, n_heads=n_heads)
    ns = max(d for d in range(1, max(1, 256 // tl) + 1) if B % d == 0)
    head_major = pl.BlockSpec((ns, n_heads, tl, LANES), lambda b, j: (b, 0, j, 0))
    hm_shape = (B, n_heads, L, LANES)
    n_copies = 2 if mxu_copies else 0
    return pl.pallas_call(
        kern,
        grid=(B // ns, L // tl),
        in_specs=[pl.BlockSpec((ns, tl, D), lambda b, j: (b, j, 0)), _const_spec((1, D)),
                  _const_spec(w_qkv.shape), _const_spec((1, LANES)), _const_spec((1, LANES))],
        out_specs=[head_major] * (3 + n_copies),
        out_shape=([jax.ShapeDtypeStruct(hm_shape, BF16), jax.ShapeDtypeStruct(hm_shape, F32),
                    jax.ShapeDtypeStruct(hm_shape, F32)] + [jax.ShapeDtypeStruct(hm_shape, BF16)] * n_copies),
        compiler_params=pltpu.CompilerParams(dimension_semantics=("arbitrary", "arbitrary"),
                                             vmem_limit_bytes=VMEM_LIMIT),
        name="odd_pre",
    )(x, gain, w_qkv, qn, kn)


def _suffix_ones(tk):
    j = lax.broadcasted_iota(jnp.int32, (2 * tk, tk), 0) & (tk - 1)
    s = lax.broadcasted_iota(jnp.int32, (2 * tk, tk), 1)
    return (j > s).astype(BF16)


def _sb_logits(q, k_blk, u, below=None):
    zs = [_mm_nt(a, b) for a, b in zip(q, k_blk)] if isinstance(q, (list, tuple)) else [_mm_nt(q, k_blk)]
    z = jnp.concatenate(zs, axis=0) if len(zs) > 1 else zs[0]
    t = jnp.maximum(z, 0.0) + jnp.log(1.0 + jnp.exp2(-jnp.abs(z))) * LOG2E
    if below is not None:
        t = jnp.where(below, t, 0.0)
    t_hi = t.astype(BF16)
    t_lo = (t - t_hi.astype(F32)).astype(BF16)
    later = jnp.dot(jnp.concatenate([t_hi, t_lo], axis=1), u, preferred_element_type=F32)
    return z, t, later


def _sb_weights(z, t, later, off, v_blk, acc):
    wts = jnp.exp2(z - t - (later + off))
    own = later[:, 0:1] + t[:, 0:1]
    if not isinstance(v_blk, (list, tuple)):
        return own, acc + _mm(wts, v_blk)
    rows = z.shape[0] // len(v_blk)
    return own, [a + _mm(wts[n * rows:(n + 1) * rows], v) for n, (a, v) in enumerate(zip(acc, v_blk))]


def _sb_block(q, k_blk, v_blk, u, off, acc, below=None):
    return _sb_weights(*_sb_logits(q, k_blk, u, below), off, v_blk, acc)


def _sb_kernel(*refs, tq, tkp, n_past, n_q, hb, qb, n_prev):
    lazy_past = bool(n_past) and n_q == 1
    if lazy_past:
        q_ref, kn_ref, vn_ref, kl_ref, vl_ref, kp_ref, vp_ref, o_ref, kbuf, vbuf, sem = refs
    elif n_past:
        q_ref, kn_ref, vn_ref, kp_ref, vp_ref, o_ref = refs
    else:
        q_ref, kn_ref, vn_ref, o_ref = refs
    u_self = _suffix_ones(tq)
    u_past = _suffix_ones(tkp) if n_past else None
    row = lax.broadcasted_iota(jnp.int32, (tq, tq), 0)
    col = lax.broadcasted_iota(jnp.int32, (tq, tq), 1)
    below = col < row
    not_below = jnp.where(below, 0.0, MASKED)

    walks = []
    if n_q == 1:
        heads = range(hb)
        qs = [q_ref[0, hh] for hh in heads]
        below_all = jnp.concatenate([below] * hb, axis=0)
        spent, accs = _sb_block(qs, [kn_ref[0, hh] for hh in heads], [vn_ref[0, hh] for hh in heads], u_self,
                                jnp.concatenate([not_below] * hb, axis=0),
                                [jnp.zeros((tq, LANES), F32)] * hb, below_all)
        if lazy_past:
            own, accs = _sb_block(qs, [kl_ref[0, hh] for hh in heads], [vl_ref[0, hh] for hh in heads], u_past,
                                  spent, accs)
            spent = spent + own
        walks = [(hh, 0, qs[hh], spent[hh * tq:(hh + 1) * tq], accs[hh], -1, n_past - 2) for hh in heads]
    chains = [(hh, w) for hh in range(hb) for w in range(qb) if n_q > 1]
    logits = {}
    for c, (hh, w) in enumerate(chains):
        i = pl.program_id(2) * qb + w
        q = q_ref[0, hh, w * tq:(w + 1) * tq, :]
        for d in range(n_prev + 1):
            blk = pl.multiple_of(jnp.maximum(i - d, 0) * tq, tq)
            logits[c, d] = _sb_logits(q, kn_ref[0, hh, pl.ds(blk, tq), :], u_self, below if d == 0 else None)
    for c, (hh, w) in enumerate(chains):
        i = pl.program_id(2) * qb + w
        q = q_ref[0, hh, w * tq:(w + 1) * tq, :]
        start = pl.multiple_of(i * tq, tq)
        spent, acc = _sb_weights(*logits[c, 0], not_below, vn_ref[0, hh, pl.ds(start, tq), :],
                                 jnp.zeros((tq, LANES), F32))
        for d in range(1, n_prev + 1):
            prev = pl.multiple_of(jnp.maximum(i - d, 0) * tq, tq)
            has_prev = jnp.full((tq, 1), i, jnp.int32) >= d
            own, acc = _sb_weights(*logits[c, d], jnp.where(has_prev, spent, MASKED),
                                   vn_ref[0, hh, pl.ds(prev, tq), :], acc)
            spent = spent + jnp.where(has_prev, own, 0.0)
        walks.append((hh, w, q, spent, acc, i - n_prev - 1, n_past - 1))

    def walk(q, k_ref, v_ref, hh, tk, u, first, spent, acc, from_hbm=False):
        def cond(st):
            return (st[0] >= 0) & (jnp.min(st[1]) < -LOG2_W_FLOOR)

        def body(st):
            j, spent, acc = st
            s0 = pl.multiple_of(j * tk, tk)
            if from_hbm:
                head = pl.program_id(1) * hb + hh
                copies = [pltpu.make_async_copy(src.at[pl.program_id(0), head, pl.ds(s0, tk), :], dst, sem.at[n])
                          for n, (src, dst) in enumerate(((k_ref, kbuf), (v_ref, vbuf)))]
                for cp in copies:
                    cp.start()
                for cp in copies:
                    cp.wait()
                k_blk, v_blk = kbuf[...], vbuf[...]
            else:
                k_blk, v_blk = k_ref[0, hh, pl.ds(s0, tk), :], v_ref[0, hh, pl.ds(s0, tk), :]
            own, acc = _sb_block(q, k_blk, v_blk, u, spent, acc)
            return j - 1, spent + own, acc

        _, spent, acc = lax.while_loop(cond, body, (first, spent, acc))
        return spent, acc

    for hh, w, _, _, acc, _, _ in walks:
        o_ref[0, w * tq:(w + 1) * tq, hh * LANES:(hh + 1) * LANES] = acc.astype(o_ref.dtype)
    if n_q == 1 and n_past <= 1:
        return

    least = functools.reduce(jnp.minimum, [wk[3] for wk in walks])

    @pl.when(jnp.min(least) < -LOG2_W_FLOOR)
    def _():
        for hh, w, q, spent, acc, first_self, first_past in walks:
            if n_q > 1:
                spent, acc = walk(q, kn_ref, vn_ref, hh, tq, u_self, first_self, spent, acc)
            if n_past:
                spent, acc = walk(q, kp_ref, vp_ref, hh, tkp, u_past, jnp.int32(first_past), spent, acc,
                                  from_hbm=lazy_past)
            o_ref[0, w * tq:(w + 1) * tq, hh * LANES:(hh + 1) * LANES] = acc.astype(o_ref.dtype)


def _stick_breaking(q, k_new, v_new, k_past, v_past):
    B, H, L, dh = q.shape
    assert dh == LANES
    tq = min(SB_TQ, L)
    assert L % tq == 0
    n_q = L // tq
    assert tq & (tq - 1) == 0
    if n_q == 1:
        hb, qb = H, 1
    else:
        kv_bytes = 2 * 2 * L * LANES * k_new.dtype.itemsize
        hb = max(d for d in (4, 2, 1) if H % d == 0 and d * kv_bytes <= SB_KV_VMEM)
        qb = max(d for d in (SB_CHAINS // hb, 2, 1) if n_q % d == 0)
    n_prev = 0 if n_q == 1 else SB_PREV
    assert H % hb == 0 and n_q % qb == 0
    args = [q, k_new, v_new]
    full = lambda n: pl.BlockSpec((1, hb, n, LANES), lambda b, h, i: (b, h, 0, 0))
    in_specs = [pl.BlockSpec((1, hb, tq * qb, LANES), lambda b, h, i: (b, h, i, 0)), full(L), full(L)]
    n_past, tkp = 0, 0
    scratch = []
    if k_past is not None:
        P = k_past.shape[2]
        tkp = min(256, P)
        assert P % tkp == 0 and tkp & (tkp - 1) == 0
        n_past = P // tkp
        if n_q == 1:
            last = pl.BlockSpec((1, hb, tkp, LANES), lambda b, h, i: (b, h, n_past - 1, 0))
            hbm = pl.BlockSpec(memory_space=pl.ANY)
            args += [k_past, v_past, k_past, v_past]
            in_specs += [last, last, hbm, hbm]
            scratch = [pltpu.VMEM((tkp, LANES), k_past.dtype), pltpu.VMEM((tkp, LANES), v_past.dtype),
                       pltpu.SemaphoreType.DMA((2,))]
        else:
            args += [k_past, v_past]
            in_specs += [full(P), full(P)]
    kern = functools.partial(_sb_kernel, tq=tq, tkp=tkp, n_past=n_past, n_q=n_q, hb=hb, qb=qb, n_prev=n_prev)
    return pl.pallas_call(
        kern,
        grid=(B, H // hb, n_q // qb),
        in_specs=in_specs,
        out_specs=pl.BlockSpec((1, tq * qb, hb * LANES), lambda b, h, i: (b, i, h)),
        scratch_shapes=scratch,
        out_shape=jax.ShapeDtypeStruct((B, L, H * LANES), BF16),
        compiler_params=pltpu.CompilerParams(dimension_semantics=("arbitrary", "arbitrary", "arbitrary"),
                                             vmem_limit_bytes=VMEM_LIMIT),
        name="stick_breaking",
    )(*args)


def _pad_lanes(v):
    return jnp.zeros((1, LANES), F32).at[0, :v.shape[0]].set(v.astype(F32))


def _even_mixers(x, s0, qbuf, dbuf, wts, gain_mix):
    (w_all, conv_w, alog_pad, dtb_pad, onorm, dw_w, dw_b, ln_g, ln_b, n_heads, qkv_w, v_w, c_b) = wts
    qkv, gcol, z, c, qlast, dlast = _even_pre(x, gain_mix, w_all, conv_w, alog_pad, dtb_pad, dw_w, dw_b, ln_g, ln_b,
                                              qbuf, dbuf, n_heads=n_heads, qkv_w=qkv_w, v_w=v_w, c_b=c_b)
    o, s_new = _delta(qkv, gcol, z, onorm, s0, n_heads=n_heads)
    return [o, c], s_new, qlast, dlast


def _odd_mixer(x, k_past, v_past, wts, gain_mix):
    w_qkv, qn, kn, n_heads = wts
    mxu_copies = x.shape[1] > SB_TQ
    q, k, v, *kv_mxu = _odd_pre(x, gain_mix, w_qkv, qn, kn, n_heads=n_heads, mxu_copies=mxu_copies)
    o = _stick_breaking(q, *(kv_mxu or (k, v)), k_past, v_past)
    return [o], k, v


def kernel(x_prompt, x_sample, state_delta, state_qkv_conv, state_dw_conv, cache_k, cache_v, norm_mix, norm_mlp, w_in_e, conv_qkv_e, a_log_e, dt_bias_e, onorm_e, dw_w_e, dw_b_e, ln_g_e, ln_b_e, w_out_e, w_qkv_o, qn_o, kn_o, w_out_o, w_up, w_down):
    depth = norm_mix.shape[0]
    bp = x_prompt.shape[0]
    xp, xs = x_prompt, x_sample
    pd, pq, pw, pk, pv = [], [], [], [], []
    sd, sq, sw, sk, sv = [], [], [], [], []
    row = lambda v: v.astype(F32).reshape(1, -1)
    for i in range(depth):
        gm, gl = row(norm_mix[i]), row(norm_mlp[i])
        wu, wd = w_up[i].astype(BF16), w_down[i].astype(BF16)
        if i % 2 == 0:
            e = i // 2
            n_heads = a_log_e.shape[1]
            qkv_w = conv_qkv_e.shape[2]
            c_b = dw_w_e.shape[2]
            v_w = n_heads * onorm_e.shape[1]
            w_in = w_in_e[e]
            ab0 = qkv_w + v_w
            ab = jnp.zeros((w_in.shape[0], LANES), w_in.dtype).at[:, :2 * n_heads].set(w_in[:, ab0:ab0 + 2 * n_heads])
            w_all = jnp.concatenate([w_in[:, :ab0], w_in[:, ab0 + 2 * n_heads:], ab], axis=1).astype(BF16)
            w_out = w_out_e[e].astype(BF16)
            wts = (w_all, conv_qkv_e[e].astype(F32), _pad_lanes(a_log_e[e]), _pad_lanes(dt_bias_e[e]),
                   row(onorm_e[e]), dw_w_e[e].astype(F32), row(dw_b_e[e]), row(ln_g_e[e]), row(ln_b_e[e]),
                   n_heads, qkv_w, v_w, c_b)
            kq, kd = conv_qkv_e.shape[1], dw_w_e.shape[1]
            s0 = jnp.zeros((bp,) + state_delta.shape[2:], F32)
            qb0 = jnp.zeros((bp, kq - 1, qkv_w), F32)
            db0 = jnp.zeros((bp, kd - 1, c_b), F32)
            ap, d1, q1, c1 = _even_mixers(xp, s0, qb0, db0, wts, gm)
            as_, d2, q2, c2 = _even_mixers(xs, state_delta[e].astype(F32), state_qkv_conv[e].astype(F32),
                                           state_dw_conv[e].astype(F32), wts, gm)
            pd.append(d1); pq.append(q1); pw.append(c1)
            sd.append(d2); sq.append(q2); sw.append(c2)
        else:
            o = i // 2
            n_heads = cache_k.shape[2]
            w_out = w_out_o[o].astype(BF16)
            wts = (w_qkv_o[o].astype(BF16), row(qn_o[o]), row(kn_o[o]), n_heads)
            ap, k1, v1 = _odd_mixer(xp, None, None, wts, gm)
            as_, k2, v2 = _odd_mixer(xs, cache_k[o].astype(F32), cache_v[o].astype(F32), wts, gm)
            pk.append(k1); pv.append(v1)
            sk.append(k2); sv.append(v2)
        xp, xs = _out_mlp([(xp, ap), (xs, as_)], w_out, gl, wu, wd)
    return (xp, xs,
            jnp.stack(pd), jnp.stack(pq), jnp.stack(pw), jnp.stack(pk), jnp.stack(pv),
            jnp.stack(sd), jnp.stack(sq), jnp.stack(sw), jnp.stack(sk), jnp.stack(sv))
```

```python
import functools

import jax
import jax.numpy as jnp
from jax import lax
from jax.experimental import pallas as pl
from jax.experimental.pallas import tpu as pltpu

F32 = jnp.float32
BF16 = jnp.bfloat16
EPS = 1e-6
CHUNK = 64
LANES = 128
PAIR = 2 * CHUNK
QBUF_OFF = 8
DBUF_OFF = 32
DW_ROWS = 32
SUBLANES = 8
EVEN_SUB = 4
VMEM_LIMIT = 56 * 1024 * 1024
LOG2E = 1.4426950408889634
LOG2_W_FLOOR = -151.0
SB_TQ = 256
SB_PREV = 1
SB_KV_VMEM = 36 * 1024 * 1024
SB_CHAINS = 16
MASKED = 1e30


def _mm(a, b):
    return jnp.dot(a.astype(BF16), b.astype(BF16), preferred_element_type=F32)


def _mm_nt(a, b):
    return lax.dot_general(a.astype(BF16), b.astype(BF16), (((1,), (1,)), ((), ())),
                           preferred_element_type=F32)


def _sigmoid(x):
    return 0.5 * jnp.tanh(0.5 * x) + 0.5


def _silu(x):
    h = 0.5 * x
    return h + h * jnp.tanh(h)


def _softplus(x):
    return jnp.maximum(x, 0.0) + jnp.log1p(jnp.exp(-jnp.abs(x)))


def _rms_scale(x):
    return lax.rsqrt(jnp.mean(x * x, axis=-1, keepdims=True) + EPS)


def _const_spec(shape):
    nd = len(shape)
    return pl.BlockSpec(shape, lambda *_: (0,) * nd, pipeline_mode=pl.Buffered(1))


def _even_pre_kernel(x_ref, g_ref, w_ref, cw_ref, alog_ref, dtb_ref, dww_ref, dwb_ref, lng_ref, lnb_ref,
                     qbuf_ref, dbuf_ref,
                     qkv_ref, gcol_ref, z_ref, c_ref, qlast_ref, dlast_ref,
                     qext, uext, ushift, *, ns, nsub, tl, qkv_w, v_w, c_b, n_heads):
    kq = cw_ref.shape[0] // SUBLANES
    kd = dww_ref.shape[0] // SUBLANES
    glu0 = qkv_w + v_w
    q_hist = slice(QBUF_OFF - (kq - 1), QBUF_OFF)
    d_hist = slice(DBUF_OFF - (kd - 1), DBUF_OFF)
    q_tail = slice(QBUF_OFF + tl - (kq - 1), QBUF_OFF + tl)
    d_tail = slice(DBUF_OFF + tl - (kd - 1), DBUF_OFF + tl)

    @pl.when(pl.program_id(1) == 0)
    def _():
        for sq in range(ns):
            qext[sq * nsub, q_hist, :] = qbuf_ref[sq]
            uext[sq * nsub, 0:d_hist.start, :] = jnp.zeros((d_hist.start, c_b), F32)
            uext[sq * nsub, d_hist, :] = dbuf_ref[sq]

    def project(units):
        m = len(units) * tl
        x = jnp.concatenate([x_ref[u // nsub, (u % nsub) * tl:(u % nsub + 1) * tl, :] for u in units], axis=0)
        h = (x * _rms_scale(x) * g_ref[...]).astype(BF16)
        qkv_pre = jnp.dot(h, w_ref[:, 0:qkv_w], preferred_element_type=F32)
        z = jnp.dot(h, w_ref[:, qkv_w:glu0], preferred_element_type=F32)
        ga = jnp.dot(h, w_ref[:, glu0:glu0 + c_b], preferred_element_type=F32)
        gb = jnp.dot(h, w_ref[:, glu0 + c_b:glu0 + 2 * c_b], preferred_element_type=F32)
        ug = ga * _sigmoid(gb)
        ab = jnp.dot(h, w_ref[:, glu0 + 2 * c_b:glu0 + 2 * c_b + LANES], preferred_element_type=F32)

        lane = lax.broadcasted_iota(jnp.int32, (m, LANES), 1)
        g = -jnp.exp(alog_ref[...]) * _softplus(ab + dtb_ref[...])
        g = jnp.where(lane < n_heads, g, 0.0)
        beta = _sigmoid(ab)
        row = lax.broadcasted_iota(jnp.int32, (m, m), 0)
        col = lax.broadcasted_iota(jnp.int32, (m, m), 1)
        tri = (((row ^ col) < CHUNK) & (col <= row)).astype(BF16)
        g_hi = g.astype(BF16)
        g_r1 = g - g_hi.astype(F32)
        g_mid = g_r1.astype(BF16)
        g_lo = (g_r1 - g_mid.astype(F32)).astype(BF16)
        gcum = (jnp.dot(tri, g_hi, preferred_element_type=F32)
                + jnp.dot(tri, g_mid, preferred_element_type=F32)
                + jnp.dot(tri, g_lo, preferred_element_type=F32))
        gcol = jnp.where(lane < n_heads, gcum, jnp.where(lane < 2 * n_heads, beta, 0.0))

        for n, u in enumerate(units):
            sq, rows, part = u // nsub, slice((u % nsub) * tl, (u % nsub + 1) * tl), slice(n * tl, (n + 1) * tl)
            qext[u, QBUF_OFF:QBUF_OFF + tl, :] = qkv_pre[part]
            uext[u, DBUF_OFF:DBUF_OFF + tl, :] = ug[part]
            z_ref[sq, rows, :] = z[part]
            gcol_ref[sq, rows, :] = gcol[part]
            if u % nsub:
                qext[u, q_hist, :] = qext[u - 1, q_tail, :]
                uext[u, d_hist, :] = uext[u - 1, d_tail, :]

    def convolve(u):
        sq, rows0 = u // nsub, (u % nsub) * tl
        for s in range(qkv_w // LANES):
            cols = slice(s * LANES, (s + 1) * LANES)
            y = None
            for j in range(kq):
                r0 = QBUF_OFF - (kq - 1) + j
                wj = cw_ref[j * SUBLANES:(j + 1) * SUBLANES, cols]
                t = wj[None] * qext[u, r0:r0 + tl, cols].reshape(tl // SUBLANES, SUBLANES, LANES)
                y = t if y is None else y + t
            y = _silu(y.reshape(tl, LANES))
            if s < 2 * n_heads:
                y = y * lax.rsqrt(jnp.sum(y * y, axis=-1, keepdims=True) + EPS)
            qkv_ref[sq, rows0:rows0 + tl, cols] = y

        span = tl + DBUF_OFF - SUBLANES
        for b in range(1, SUBLANES):
            ushift[b - 1, 0:span, :] = uext[u, b:b + span, :]
        for r in range(tl // DW_ROWS):
            acc = None
            for j in range(kd):
                a, b = divmod(DBUF_OFF - (kd - 1) + j, SUBLANES)
                r0 = a * SUBLANES + r * DW_ROWS
                win = uext[u, r0:r0 + DW_ROWS, :] if b == 0 else ushift[b - 1, r0:r0 + DW_ROWS, :]
                wj = dww_ref[j * SUBLANES:(j + 1) * SUBLANES, :]
                t = wj[None] * win.reshape(DW_ROWS // SUBLANES, SUBLANES, c_b)
                acc = t if acc is None else acc + t
            cpre = acc.reshape(DW_ROWS, c_b) + dwb_ref[...]
            mu = jnp.mean(cpre, axis=-1, keepdims=True)
            xc = cpre - mu
            var = jnp.mean(xc * xc, axis=-1, keepdims=True)
            y = xc * lax.rsqrt(var + EPS) * lng_ref[...] + lnb_ref[...]
            c_ref[sq, rows0 + r * DW_ROWS:rows0 + (r + 1) * DW_ROWS, :] = _silu(y).astype(c_ref.dtype)

    units = list(range(ns * nsub))
    groups = [units] if nsub == 1 else [[u] for u in units]
    for gi, group in enumerate(groups):
        project(group)
        if gi:
            for u in groups[gi - 1]:
                convolve(u)
    for u in groups[-1]:
        convolve(u)

    for sq in range(ns):
        last = sq * nsub + nsub - 1
        qlast = qext[last, q_tail, :]
        qlast_ref[sq] = qlast
        qext[sq * nsub, q_hist, :] = qlast
        dlast = uext[last, d_tail, :]
        dlast_ref[sq] = dlast
        uext[sq * nsub, d_hist, :] = dlast


def _even_pre(x, gain, w_all, conv_w, alog_pad, dtb_pad, dw_w, dw_b, ln_g, ln_b, qbuf, dbuf, *, n_heads, qkv_w, v_w, c_b):
    n_seq, L, D = x.shape
    tl = min(256, L)
    assert L % tl == 0 and tl % DW_ROWS == 0 and tl % CHUNK == 0
    kq, kd = conv_w.shape[0], dw_w.shape[0]
    assert kq - 1 <= QBUF_OFF and kd - 1 <= DBUF_OFF and tl >= kd - 1
    conv_w = jnp.repeat(conv_w, SUBLANES, axis=0)
    dw_w = jnp.repeat(dw_w, SUBLANES, axis=0)
    nw = w_all.shape[1]
    ns = max(d for d in range(1, max(1, 256 // tl) + 1) if n_seq % d == 0)
    nsub = max(d for d in (EVEN_SUB, 2, 1) if (L // tl) % d == 0) if ns == 1 else 1
    rows = nsub * tl
    kern = functools.partial(_even_pre_kernel, ns=ns, nsub=nsub, tl=tl, qkv_w=qkv_w, v_w=v_w, c_b=c_b,
                             n_heads=n_heads)
    seq_blk = lambda w: pl.BlockSpec((ns, rows, w), lambda s, j: (s, j, 0))
    per_seq = lambda r, w: pl.BlockSpec((ns, r, w), lambda s, j: (s, 0, 0))
    return pl.pallas_call(
        kern,
        grid=(n_seq // ns, L // rows),
        in_specs=[seq_blk(D), _const_spec((1, D)), _const_spec((D, nw)), _const_spec(conv_w.shape),
                  _const_spec((1, LANES)), _const_spec((1, LANES)), _const_spec(dw_w.shape),
                  _const_spec((1, c_b)), _const_spec((1, c_b)), _const_spec((1, c_b)),
                  per_seq(kq - 1, qkv_w), per_seq(kd - 1, c_b)],
        out_specs=[seq_blk(qkv_w), seq_blk(LANES), seq_blk(v_w), seq_blk(c_b),
                   per_seq(kq - 1, qkv_w), per_seq(kd - 1, c_b)],
        out_shape=[jax.ShapeDtypeStruct((n_seq, L, qkv_w), F32),
                   jax.ShapeDtypeStruct((n_seq, L, LANES), F32),
                   jax.ShapeDtypeStruct((n_seq, L, v_w), F32),
                   jax.ShapeDtypeStruct((n_seq, L, c_b), BF16),
                   jax.ShapeDtypeStruct((n_seq, kq - 1, qkv_w), F32),
                   jax.ShapeDtypeStruct((n_seq, kd - 1, c_b), F32)],
        scratch_shapes=[pltpu.VMEM((ns * nsub, QBUF_OFF + tl, qkv_w), F32),
                        pltpu.VMEM((ns * nsub, DBUF_OFF + tl, c_b), F32),
                        pltpu.VMEM((SUBLANES - 1, DBUF_OFF + tl, c_b), F32)],
        compiler_params=pltpu.CompilerParams(dimension_semantics=("arbitrary", "arbitrary"),
                                             vmem_limit_bytes=VMEM_LIMIT),
        name="even_pre",
    )(x, gain, w_all, conv_w, alog_pad, dtb_pad, dw_w, dw_b, ln_g, ln_b, qbuf, dbuf)


def _unit_lower_inverses(a_list, row, col):
    x = row ^ col
    eye = (row == col).astype(F32)
    a8 = [jnp.where(x < 8, a, 0.0) for a in a_list]
    a8_2 = [_mm(t, t) for t in a8]
    p = [eye - t for t in a8]
    a8_4 = [_mm(t, t) for t in a8_2]
    p = [t + _mm(t, sq) for t, sq in zip(p, a8_2)]
    d = [t + _mm(t, sq) for t, sq in zip(p, a8_4)]
    s = 8
    while s < CHUNK:
        off = [jnp.where((x >= s) & (x < 2 * s), a, 0.0) for a in a_list]
        od = [_mm(o, t) for o, t in zip(off, d)]
        d = [t - _mm(t, u) for t, u in zip(d, od)]
        s *= 2
    return d


def _delta_kernel(qkv_ref, gcol_ref, z_ref, onorm_ref, sin_ref, o_ref, sout_ref, *, tb, n_heads, chained):
    dk = dv = LANES
    qk_w = n_heads * dk
    n_pairs = tb // PAIR
    if chained:
        @pl.when(pl.program_id(1) == 0)
        def _():
            sout_ref[...] = sin_ref[...]

    row = lax.broadcasted_iota(jnp.int32, (PAIR, PAIR), 0)
    col = lax.broadcasted_iota(jnp.int32, (PAIR, PAIR), 1)
    same = (row ^ col) < CHUNK
    causal = same & (col <= row)
    strict = same & (col < row)
    in_chunk = [(col >= c * CHUNK) & (col < (c + 1) * CHUNK) for c in range(2)]

    tiles = [(p, h) for p in range(n_pairs) for h in range(n_heads)]
    q, k, kb, vb, gc, a, qk = {}, {}, {}, {}, {}, {}, {}
    for p, h in tiles:
        rows = slice(p * PAIR, (p + 1) * PAIR)
        q[p, h] = qkv_ref[0, rows, h * dk:(h + 1) * dk] * (dk ** -0.5)
        k[p, h] = qkv_ref[0, rows, qk_w + h * dk:qk_w + (h + 1) * dk]
        gc[p, h] = jnp.broadcast_to(gcol_ref[0, rows, h:h + 1], (PAIR, PAIR))
        bt = jnp.broadcast_to(gcol_ref[0, rows, n_heads + h:n_heads + h + 1], (PAIR, PAIR))
        kb[p, h] = k[p, h] * bt
        vb[p, h] = qkv_ref[0, rows, 2 * qk_w + h * dv:2 * qk_w + (h + 1) * dv] * bt
    for t in tiles:
        diff = gc[t] - gc[t].T
        decay = jnp.where(causal, jnp.exp(jnp.where(causal, diff, 0.0)), 0.0)
        a[t] = jnp.where(strict, _mm_nt(kb[t], k[t]) * decay, 0.0)
        qk[t] = jnp.where(causal, _mm_nt(q[t], k[t]) * decay, 0.0)
    inv = dict(zip(tiles, _unit_lower_inverses([a[t] for t in tiles], row, col)))
    uw, kg_t, gl = {}, {}, {}
    for t in tiles:
        eg = jnp.exp(gc[t])
        uw[t] = _mm(inv[t], jnp.concatenate([vb[t], kb[t] * eg], axis=1))
        gl_rows = [jnp.broadcast_to(gc[t][(c + 1) * CHUNK - 1:(c + 1) * CHUNK, :], (PAIR, PAIR)) for c in range(2)]
        gl[t] = [jnp.exp(g) for g in gl_rows]
        gcl = jnp.where(row < CHUNK, gl_rows[0], gl_rows[1])
        kg_t[t] = (k[t] * jnp.exp(gcl - gc[t])).T
        q[t] = q[t] * eg
    qk_uw = {t: _mm(qk[t], uw[t]) for t in tiles}
    kg_uw = {(t, c): _mm(jnp.where(in_chunk[c], kg_t[t], 0.0), uw[t])
             for t in tiles for c in range(2)}

    state = [sout_ref[0, h] for h in range(n_heads)] if chained else None
    for p in range(n_pairs):
        rows = slice(p * PAIR, (p + 1) * PAIR)
        o_parts = {h: [] for h in range(n_heads)}
        for c in range(2):
            r = slice(c * CHUNK, (c + 1) * CHUNK)
            for h in range(n_heads):
                t = (p, h)
                s_old = state[h] if chained else sin_ref[c, h]
                q_eff = q[t][r] - qk_uw[t][r, dv:]
                xs = _mm(jnp.concatenate([kg_uw[t, c][:, dv:], q_eff], axis=0), s_old)
                o_parts[h].append(qk_uw[t][r, :dv] + xs[dk:])
                s_new = s_old * gl[t][c] + (kg_uw[t, c][:, :dv] - xs[:dk])
                if chained:
                    state[h] = s_new
                else:
                    sout_ref[c, h] = s_new
        for h in range(n_heads):
            o = jnp.concatenate(o_parts[h], axis=0)
            zz = z_ref[0, rows, h * dv:(h + 1) * dv].astype(F32)
            y = (o * _rms_scale(o) * onorm_ref[...]) * _silu(zz)
            o_ref[0, rows, h * dv:(h + 1) * dv] = y.astype(o_ref.dtype)
    if chained:
        for h in range(n_heads):
            sout_ref[0, h] = state[h]


def _delta(qkv, gcol, z, onorm, s0, *, n_heads):
    n_seq, L, qkv_w = qkv.shape
    v_w = z.shape[-1]
    assert qkv_w == 3 * n_heads * LANES and v_w == n_heads * LANES
    if L % PAIR == 0:
        chained, n_grp, Lg, per_grp = True, n_seq, L, 1
    else:
        assert L == CHUNK and n_seq % 2 == 0
        chained, n_grp, Lg, per_grp = False, n_seq // 2, PAIR, 2
        qkv, gcol, z = (t.reshape(n_grp, PAIR, t.shape[-1]) for t in (qkv, gcol, z))
    tb = min(512, Lg)
    assert Lg % tb == 0
    kern = functools.partial(_delta_kernel, tb=tb, n_heads=n_heads, chained=chained)
    blk = lambda w: pl.BlockSpec((1, tb, w), lambda s, j: (s, j, 0))
    st = pl.BlockSpec((per_grp, n_heads, LANES, LANES), lambda s, j: (s, 0, 0, 0))
    o, s_new = pl.pallas_call(
        kern,
        grid=(n_grp, Lg // tb),
        in_specs=[blk(qkv_w), blk(LANES), blk(v_w), _const_spec((1, LANES)), st],
        out_specs=[blk(v_w), st],
        out_shape=[jax.ShapeDtypeStruct((n_grp, Lg, v_w), BF16),
                   jax.ShapeDtypeStruct(s0.shape, F32)],
        compiler_params=pltpu.CompilerParams(dimension_semantics=("arbitrary", "arbitrary"),
                                             vmem_limit_bytes=VMEM_LIMIT),
        name="delta_rule",
    )(qkv, gcol, z, onorm, s0)
    return o.reshape(n_seq, L, v_w), s_new


def _out_mlp_kernel(*refs, n_a, steps, tf):
    n_s = len(steps)
    per = 1 + n_a
    w_ref, g_ref, wup_ref, wdn_ref = refs[n_s * per:n_s * per + 4]
    o_refs = refs[n_s * per + 4:]

    def tile(x_ref, a_refs, o_ref):
        a = a_refs[0][...] if n_a == 1 else jnp.concatenate([a_ref[...] for a_ref in a_refs], axis=-1)
        x1 = x_ref[...] + jnp.dot(a, w_ref[...], preferred_element_type=F32)
        h = (x1 * _rms_scale(x1) * g_ref[...]).astype(BF16)
        acc = x1
        for f in range(wup_ref.shape[1] // tf):
            r = jnp.maximum(jnp.dot(h, wup_ref[:, f * tf:(f + 1) * tf], preferred_element_type=F32), 0.0)
            acc = acc + jnp.dot((r * r).astype(BF16), wdn_ref[f * tf:(f + 1) * tf, :],
                                preferred_element_type=F32)
        o_ref[...] = acc

    i = pl.program_id(0)
    first = 0
    for s in range(n_s):
        run = functools.partial(tile, refs[s * per], refs[s * per + 1:(s + 1) * per], o_refs[s])
        if n_s == 1:
            run()
        else:
            pl.when((i >= first) & (i < first + steps[s]))(run)
        first += steps[s]


def _out_mlp(streams, w_out, gain, w_up, w_down):
    D = streams[0][0].shape[-1]
    flat = [(x.reshape(-1, D), [a.reshape(-1, a.shape[-1]) for a in a_list]) for x, a_list in streams]
    tm = min([512] + [x.shape[0] for x, _ in flat])
    assert all(x.shape[0] % tm == 0 for x, _ in flat)
    steps = [x.shape[0] // tm for x, _ in flat]
    n_a = len(flat[0][1])
    F = w_up.shape[1]
    tf = min(1024, F)
    assert F % tf == 0
    kern = functools.partial(_out_mlp_kernel, n_a=n_a, steps=tuple(steps), tf=tf)

    def rows(w, first, n):
        return pl.BlockSpec((tm, w), lambda i: (jnp.minimum(jnp.maximum(i - first, 0), n - 1), 0))

    in_specs, args, out_specs, first = [], [], [], 0
    for (x, a_list), n in zip(flat, steps):
        in_specs += [rows(D, first, n)] + [rows(a.shape[1], first, n) for a in a_list]
        args += [x] + a_list
        out_specs.append(rows(D, first, n))
        first += n
    outs = pl.pallas_call(
        kern,
        grid=(sum(steps),),
        in_specs=in_specs + [_const_spec(w_out.shape), _const_spec((1, D)), _const_spec(w_up.shape),
                             _const_spec(w_down.shape)],
        out_specs=out_specs,
        out_shape=[jax.ShapeDtypeStruct(x.shape, F32) for x, _ in flat],
        compiler_params=pltpu.CompilerParams(dimension_semantics=("arbitrary",), vmem_limit_bytes=VMEM_LIMIT),
        name="out_mlp",
    )(*args, w_out, gain, w_up, w_down)
    return [o.reshape(x.shape) for o, (x, _) in zip(outs, streams)]


def _odd_pre_kernel(x_ref, g_ref, w_ref, qn_ref, kn_ref, q_ref, k_ref, v_ref, *mxu_copies, n_heads):
    dh = LANES
    hd = n_heads * dh
    ns, tl, d_model = x_ref.shape
    x = x_ref[...].reshape(ns * tl, d_model)
    h = (x * _rms_scale(x) * g_ref[...]).astype(BF16)
    q = jnp.dot(h, w_ref[:, 0:hd], preferred_element_type=F32)
    k = jnp.dot(h, w_ref[:, hd:2 * hd], preferred_element_type=F32)
    v = jnp.dot(h, w_ref[:, 2 * hd:3 * hd], preferred_element_type=F32)
    per_seq = lambda t: t.reshape(ns, tl, dh)
    for hh in range(n_heads):
        cols = slice(hh * dh, (hh + 1) * dh)
        qh = q[:, cols]
        kh = k[:, cols]
        q_ref[:, hh] = per_seq((qh * _rms_scale(qh) * qn_ref[...] * (dh ** -0.5 * LOG2E)).astype(q_ref.dtype))
        kh = kh * _rms_scale(kh) * kn_ref[...]
        k_ref[:, hh] = per_seq(kh)
        v_ref[:, hh] = per_seq(v[:, cols])
        if mxu_copies:
            mxu_copies[0][:, hh] = per_seq(kh.astype(BF16))
            mxu_copies[1][:, hh] = per_seq(v[:, cols].astype(BF16))


def _odd_pre(x, gain, w_qkv, qn, kn, *, n_heads, mxu_copies):
    B, L, D = x.shape
    tl = min(256, L)
    assert L % tl == 0 and w_qkv.shape[1] == 3 * n_heads * LANES
    kern = functools.partial(_odd_pre_kernel, n_heads=n_heads)
    ns = max(d for d in range(1, max(1, 256 // tl) + 1) if B % d == 0)
    head_major = pl.BlockSpec((ns, n_heads, tl, LANES), lambda b, j: (b, 0, j, 0))
    hm_shape = (B, n_heads, L, LANES)
    n_copies = 2 if mxu_copies else 0
    return pl.pallas_call(
        kern,
        grid=(B // ns, L // tl),
        in_specs=[pl.BlockSpec((ns, tl, D), lambda b, j: (b, j, 0)), _const_spec((1, D)),
                  _const_spec(w_qkv.shape), _const_spec((1, LANES)), _const_spec((1, LANES))],
        out_specs=[head_major] * (3 + n_copies),
        out_shape=([jax.ShapeDtypeStruct(hm_shape, BF16), jax.ShapeDtypeStruct(hm_shape, F32),
                    jax.ShapeDtypeStruct(hm_shape, F32)] + [jax.ShapeDtypeStruct(hm_shape, BF16)] * n_copies),
        compiler_params=pltpu.CompilerParams(dimension_semantics=("arbitrary", "arbitrary"),
                                             vmem_limit_bytes=VMEM_LIMIT),
        name="odd_pre",
    )(x, gain, w_qkv, qn, kn)


def _suffix_ones(tk):
    j = lax.broadcasted_iota(jnp.int32, (2 * tk, tk), 0) & (tk - 1)
    s = lax.broadcasted_iota(jnp.int32, (2 * tk, tk), 1)
    return (j > s).astype(BF16)


def _sb_logits(q, k_blk, u, below=None):
    zs = [_mm_nt(a, b) for a, b in zip(q, k_blk)] if isinstance(q, (list, tuple)) else [_mm_nt(q, k_blk)]
    z = jnp.concatenate(zs, axis=0) if len(zs) > 1 else zs[0]
    t = jnp.maximum(z, 0.0) + jnp.log(1.0 + jnp.exp2(-jnp.abs(z))) * LOG2E
    if below is not None:
        t = jnp.where(below, t, 0.0)
    t_hi = t.astype(BF16)
    t_lo = (t - t_hi.astype(F32)).astype(BF16)
    later = jnp.dot(jnp.concatenate([t_hi, t_lo], axis=1), u, preferred_element_type=F32)
    return z, t, later


def _sb_weights(z, t, later, off, v_blk, acc):
    wts = jnp.exp2(z - t - (later + off))
    own = later[:, 0:1] + t[:, 0:1]
    if not isinstance(v_blk, (list, tuple)):
        return own, acc + _mm(wts, v_blk)
    rows = z.shape[0] // len(v_blk)
    return own, [a + _mm(wts[n * rows:(n + 1) * rows], v) for n, (a, v) in enumerate(zip(acc, v_blk))]


def _sb_block(q, k_blk, v_blk, u, off, acc, below=None):
    return _sb_weights(*_sb_logits(q, k_blk, u, below), off, v_blk, acc)


def _sb_kernel(*refs, tq, tkp, n_past, n_q, hb, qb, n_prev):
    lazy_past = bool(n_past) and n_q == 1
    if lazy_past:
        q_ref, kn_ref, vn_ref, kl_ref, vl_ref, kp_ref, vp_ref, o_ref, kbuf, vbuf, sem = refs
    elif n_past:
        q_ref, kn_ref, vn_ref, kp_ref, vp_ref, o_ref = refs
    else:
        q_ref, kn_ref, vn_ref, o_ref = refs
    u_self = _suffix_ones(tq)
    u_past = _suffix_ones(tkp) if n_past else None
    row = lax.broadcasted_iota(jnp.int32, (tq, tq), 0)
    col = lax.broadcasted_iota(jnp.int32, (tq, tq), 1)
    below = col < row
    not_below = jnp.where(below, 0.0, MASKED)

    walks = []
    if n_q == 1:
        heads = range(hb)
        qs = [q_ref[0, hh] for hh in heads]
        below_all = jnp.concatenate([below] * hb, axis=0)
        spent, accs = _sb_block(qs, [kn_ref[0, hh] for hh in heads], [vn_ref[0, hh] for hh in heads], u_self,
                                jnp.concatenate([not_below] * hb, axis=0),
                                [jnp.zeros((tq, LANES), F32)] * hb, below_all)
        if lazy_past:
            own, accs = _sb_block(qs, [kl_ref[0, hh] for hh in heads], [vl_ref[0, hh] for hh in heads], u_past,
                                  spent, accs)
            spent = spent + own
        walks = [(hh, 0, qs[hh], spent[hh * tq:(hh + 1) * tq], accs[hh], -1, n_past - 2) for hh in heads]
    chains = [(hh, w) for hh in range(hb) for w in range(qb) if n_q > 1]
    logits = {}
    for c, (hh, w) in enumerate(chains):
        i = pl.program_id(2) * qb + w
        q = q_ref[0, hh, w * tq:(w + 1) * tq, :]
        for d in range(n_prev + 1):
            blk = pl.multiple_of(jnp.maximum(i - d, 0) * tq, tq)
            logits[c, d] = _sb_logits(q, kn_ref[0, hh, pl.ds(blk, tq), :], u_self, below if d == 0 else None)
    for c, (hh, w) in enumerate(chains):
        i = pl.program_id(2) * qb + w
        q = q_ref[0, hh, w * tq:(w + 1) * tq, :]
        start = pl.multiple_of(i * tq, tq)
        spent, acc = _sb_weights(*logits[c, 0], not_below, vn_ref[0, hh, pl.ds(start, tq), :],
                                 jnp.zeros((tq, LANES), F32))
        for d in range(1, n_prev + 1):
            prev = pl.multiple_of(jnp.maximum(i - d, 0) * tq, tq)
            has_prev = jnp.full((tq, 1), i, jnp.int32) >= d
            own, acc = _sb_weights(*logits[c, d], jnp.where(has_prev, spent, MASKED),
                                   vn_ref[0, hh, pl.ds(prev, tq), :], acc)
            spent = spent + jnp.where(has_prev, own, 0.0)
        walks.append((hh, w, q, spent, acc, i - n_prev - 1, n_past - 1))

    def walk(q, k_ref, v_ref, hh, tk, u, first, spent, acc, from_hbm=False):
        def cond(st):
            return (st[0] >= 0) & (jnp.min(st[1]) < -LOG2_W_FLOOR)

        def body(st):
            j, spent, acc = st
            s0 = pl.multiple_of(j * tk, tk)
            if from_hbm:
                head = pl.program_id(1) * hb + hh
                copies = [pltpu.make_async_copy(src.at[pl.program_id(0), head, pl.ds(s0, tk), :], dst, sem.at[n])
                          for n, (src, dst) in enumerate(((k_ref, kbuf), (v_ref, vbuf)))]
                for cp in copies:
                    cp.start()
                for cp in copies:
                    cp.wait()
                k_blk, v_blk = kbuf[...], vbuf[...]
            else:
                k_blk, v_blk = k_ref[0, hh, pl.ds(s0, tk), :], v_ref[0, hh, pl.ds(s0, tk), :]
            own, acc = _sb_block(q, k_blk, v_blk, u, spent, acc)
            return j - 1, spent + own, acc

        _, spent, acc = lax.while_loop(cond, body, (first, spent, acc))
        return spent, acc

    for hh, w, _, _, acc, _, _ in walks:
        o_ref[0, w * tq:(w + 1) * tq, hh * LANES:(hh + 1) * LANES] = acc.astype(o_ref.dtype)
    if n_q == 1 and n_past <= 1:
        return

    least = functools.reduce(jnp.minimum, [wk[3] for wk in walks])

    @pl.when(jnp.min(least) < -LOG2_W_FLOOR)
    def _():
        for hh, w, q, spent, acc, first_self, first_past in walks:
            if n_q > 1:
                spent, acc = walk(q, kn_ref, vn_ref, hh, tq, u_self, first_self, spent, acc)
            if n_past:
                spent, acc = walk(q, kp_ref, vp_ref, hh, tkp, u_past, jnp.int32(first_past), spent, acc,
                                  from_hbm=lazy_past)
            o_ref[0, w * tq:(w + 1) * tq, hh * LANES:(hh + 1) * LANES] = acc.astype(o_ref.dtype)


def _stick_breaking(q, k_new, v_new, k_past, v_past):
    B, H, L, dh = q.shape
    assert dh == LANES
    tq = min(SB_TQ, L)
    assert L % tq == 0
    n_q = L // tq
    assert tq & (tq - 1) == 0
    if n_q == 1:
        hb, qb = H, 1
    else:
        kv_bytes = 2 * 2 * L * LANES * k_new.dtype.itemsize
        hb = max(d for d in (4, 2, 1) if H % d == 0 and d * kv_bytes <= SB_KV_VMEM)
        qb = max(d for d in (SB_CHAINS // hb, 2, 1) if n_q % d == 0)
    n_prev = 0 if n_q == 1 else SB_PREV
    assert H % hb == 0 and n_q % qb == 0
    args = [q, k_new, v_new]
    full = lambda n: pl.BlockSpec((1, hb, n, LANES), lambda b, h, i: (b, h, 0, 0))
    in_specs = [pl.BlockSpec((1, hb, tq * qb, LANES), lambda b, h, i: (b, h, i, 0)), full(L), full(L)]
    n_past, tkp = 0, 0
    scratch = []
    if k_past is not None:
        P = k_past.shape[2]
        tkp = min(256, P)
        assert P % tkp == 0 and tkp & (tkp - 1) == 0
        n_past = P // tkp
        if n_q == 1:
            last = pl.BlockSpec((1, hb, tkp, LANES), lambda b, h, i: (b, h, n_past - 1, 0))
            hbm = pl.BlockSpec(memory_space=pl.ANY)
            args += [k_past, v_past, k_past, v_past]
            in_specs += [last, last, hbm, hbm]
            scratch = [pltpu.VMEM((tkp, LANES), k_past.dtype), pltpu.VMEM((tkp, LANES), v_past.dtype),
                       pltpu.SemaphoreType.DMA((2,))]
        else:
            args += [k_past, v_past]
            in_specs += [full(P), full(P)]
    kern = functools.partial(_sb_kernel, tq=tq, tkp=tkp, n_past=n_past, n_q=n_q, hb=hb, qb=qb, n_prev=n_prev)
    return pl.pallas_call(
        kern,
        grid=(B, H // hb, n_q // qb),
        in_specs=in_specs,
        out_specs=pl.BlockSpec((1, tq * qb, hb * LANES), lambda b, h, i: (b, i, h)),
        scratch_shapes=scratch,
        out_shape=jax.ShapeDtypeStruct((B, L, H * LANES), BF16),
        compiler_params=pltpu.CompilerParams(dimension_semantics=("arbitrary", "arbitrary", "arbitrary"),
                                             vmem_limit_bytes=VMEM_LIMIT),
        name="stick_breaking",
    )(*args)


def _pad_lanes(v):
    return jnp.zeros((1, LANES), F32).at[0, :v.shape[0]].set(v.astype(F32))


def _even_mixers(x, s0, qbuf, dbuf, wts, gain_mix):
    (w_all, conv_w, alog_pad, dtb_pad, onorm, dw_w, dw_b, ln_g, ln_b, n_heads, qkv_w, v_w, c_b) = wts
    qkv, gcol, z, c, qlast, dlast = _even_pre(x, gain_mix, w_all, conv_w, alog_pad, dtb_pad, dw_w, dw_b, ln_g, ln_b,
                                              qbuf, dbuf, n_heads=n_heads, qkv_w=qkv_w, v_w=v_w, c_b=c_b)
    o, s_new = _delta(qkv, gcol, z, onorm, s0, n_heads=n_heads)
    return [o, c], s_new, qlast, dlast


def _odd_mixer(x, k_past, v_past, wts, gain_mix):
    w_qkv, qn, kn, n_heads = wts
    mxu_copies = x.shape[1] > SB_TQ
    q, k, v, *kv_mxu = _odd_pre(x, gain_mix, w_qkv, qn, kn, n_heads=n_heads, mxu_copies=mxu_copies)
    o = _stick_breaking(q, *(kv_mxu or (k, v)), k_past, v_past)
    return [o], k, v


def kernel(x_prompt, x_sample, state_delta, state_qkv_conv, state_dw_conv, cache_k, cache_v, norm_mix, norm_mlp, w_in_e, conv_qkv_e, a_log_e, dt_bias_e, onorm_e, dw_w_e, dw_b_e, ln_g_e, ln_b_e, w_out_e, w_qkv_o, qn_o, kn_o, w_out_o, w_up, w_down):
    depth = norm_mix.shape[0]
    bp = x_prompt.shape[0]
    xp, xs = x_prompt, x_sample
    pd, pq, pw, pk, pv = [], [], [], [], []
    sd, sq, sw, sk, sv = [], [], [], [], []
    row = lambda v: v.astype(F32).reshape(1, -1)
    for i in range(depth):
        gm, gl = row(norm_mix[i]), row(norm_mlp[i])
        wu, wd = w_up[i].astype(BF16), w_down[i].astype(BF16)
        if i % 2 == 0:
            e = i // 2
            n_heads = a_log_e.shape[1]
            qkv_w = conv_qkv_e.shape[2]
            c_b = dw_w_e.shape[2]
            v_w = n_heads * onorm_e.shape[1]
            w_in = w_in_e[e]
            ab0 = qkv_w + v_w
            ab = jnp.zeros((w_in.shape[0], LANES), w_in.dtype).at[:, :2 * n_heads].set(w_in[:, ab0:ab0 + 2 * n_heads])
            w_all = jnp.concatenate([w_in[:, :ab0], w_in[:, ab0 + 2 * n_heads:], ab], axis=1).astype(BF16)
            w_out = w_out_e[e].astype(BF16)
            wts = (w_all, conv_qkv_e[e].astype(F32), _pad_lanes(a_log_e[e]), _pad_lanes(dt_bias_e[e]),
                   row(onorm_e[e]), dw_w_e[e].astype(F32), row(dw_b_e[e]), row(ln_g_e[e]), row(ln_b_e[e]),
                   n_heads, qkv_w, v_w, c_b)
            kq, kd = conv_qkv_e.shape[1], dw_w_e.shape[1]
            s0 = jnp.zeros((bp,) + state_delta.shape[2:], F32)
            qb0 = jnp.zeros((bp, kq - 1, qkv_w), F32)
            db0 = jnp.zeros((bp, kd - 1, c_b), F32)
            ap, d1, q1, c1 = _even_mixers(xp, s0, qb0, db0, wts, gm)
            as_, d2, q2, c2 = _even_mixers(xs, state_delta[e].astype(F32), state_qkv_conv[e].astype(F32),
                                           state_dw_conv[e].astype(F32), wts, gm)
            pd.append(d1); pq.append(q1); pw.append(c1)
            sd.append(d2); sq.append(q2); sw.append(c2)
        else:
            o = i // 2
            n_heads = cache_k.shape[2]
            w_out = w_out_o[o].astype(BF16)
            wts = (w_qkv_o[o].astype(BF16), row(qn_o[o]), row(kn_o[o]), n_heads)
            ap, k1, v1 = _odd_mixer(xp, None, None, wts, gm)
            as_, k2, v2 = _odd_mixer(xs, cache_k[o].astype(F32), cache_v[o].astype(F32), wts, gm)
            pk.append(k1); pv.append(v1)
            sk.append(k2); sv.append(v2)
        xp, xs = _out_mlp([(xp, ap), (xs, as_)], w_out, gl, wu, wd)
    return (xp, xs,
            jnp.stack(pd), jnp.stack(pq), jnp.stack(pw), jnp.stack(pk), jnp.stack(pv),
            jnp.stack(sd), jnp.stack(sq), jnp.stack(sw), jnp.stack(sk), jnp.stack(sv))
```

```python
import functools

import jax
import jax.numpy as jnp
from jax import lax
from jax.experimental import pallas as pl
from jax.experimental.pallas import tpu as pltpu

F32 = jnp.float32
BF16 = jnp.bfloat16
EPS = 1e-6
CHUNK = 64
LANES = 128
PAIR = 2 * CHUNK
QBUF_OFF = 8
DBUF_OFF = 32
DW_ROWS = 32
SUBLANES = 8
EVEN_SUB = 4
VMEM_LIMIT = 56 * 1024 * 1024
LOG2E = 1.4426950408889634
LOG2_W_FLOOR = -151.0
SB_TQ = 256
SB_PREV = 1
SB_KV_VMEM = 36 * 1024 * 1024
SB_CHAINS = 16
MASKED = 1e30


def _mm(a, b):
    return jnp.dot(a.astype(BF16), b.astype(BF16), preferred_element_type=F32)


def _mm_nt(a, b):
    return lax.dot_general(a.astype(BF16), b.astype(BF16), (((1,), (1,)), ((), ())),
                           preferred_element_type=F32)


def _sigmoid(x):
    return 0.5 * jnp.tanh(0.5 * x) + 0.5


def _silu(x):
    h = 0.5 * x
    return h + h * jnp.tanh(h)


def _softplus(x):
    return jnp.maximum(x, 0.0) + jnp.log1p(jnp.exp(-jnp.abs(x)))


def _rms_scale(x):
    return lax.rsqrt(jnp.mean(x * x, axis=-1, keepdims=True) + EPS)


def _const_spec(shape):
    nd = len(shape)
    return pl.BlockSpec(shape, lambda *_: (0,) * nd, pipeline_mode=pl.Buffered(1))


def _even_pre_kernel(x_ref, g_ref, w_ref, cw_ref, alog_ref, dtb_ref, dww_ref, dwb_ref, lng_ref, lnb_ref,
                     qbuf_ref, dbuf_ref,
                     qkv_ref, gcol_ref, z_ref, c_ref, qlast_ref, dlast_ref,
                     qext, uext, ushift, *, ns, nsub, tl, qkv_w, v_w, c_b, n_heads):
    kq = cw_ref.shape[0] // SUBLANES
    kd = dww_ref.shape[0] // SUBLANES
    glu0 = qkv_w + v_w
    q_hist = slice(QBUF_OFF - (kq - 1), QBUF_OFF)
    d_hist = slice(DBUF_OFF - (kd - 1), DBUF_OFF)
    q_tail = slice(QBUF_OFF + tl - (kq - 1), QBUF_OFF + tl)
    d_tail = slice(DBUF_OFF + tl - (kd - 1), DBUF_OFF + tl)

    @pl.when(pl.program_id(1) == 0)
    def _():
        for sq in range(ns):
            qext[sq * nsub, q_hist, :] = qbuf_ref[sq]
            uext[sq * nsub, 0:d_hist.start, :] = jnp.zeros((d_hist.start, c_b), F32)
            uext[sq * nsub, d_hist, :] = dbuf_ref[sq]

    def project(units):
        m = len(units) * tl
        x = jnp.concatenate([x_ref[u // nsub, (u % nsub) * tl:(u % nsub + 1) * tl, :] for u in units], axis=0)
        h = (x * _rms_scale(x) * g_ref[...]).astype(BF16)
        qkv_pre = jnp.dot(h, w_ref[:, 0:qkv_w], preferred_element_type=F32)
        z = jnp.dot(h, w_ref[:, qkv_w:glu0], preferred_element_type=F32)
        ga = jnp.dot(h, w_ref[:, glu0:glu0 + c_b], preferred_element_type=F32)
        gb = jnp.dot(h, w_ref[:, glu0 + c_b:glu0 + 2 * c_b], preferred_element_type=F32)
        ug = ga * _sigmoid(gb)
        ab = jnp.dot(h, w_ref[:, glu0 + 2 * c_b:glu0 + 2 * c_b + LANES], preferred_element_type=F32)

        lane = lax.broadcasted_iota(jnp.int32, (m, LANES), 1)
        g = -jnp.exp(alog_ref[...]) * _softplus(ab + dtb_ref[...])
        g = jnp.where(lane < n_heads, g, 0.0)
        beta = _sigmoid(ab)
        row = lax.broadcasted_iota(jnp.int32, (m, m), 0)
        col = lax.broadcasted_iota(jnp.int32, (m, m), 1)
        tri = (((row ^ col) < CHUNK) & (col <= row)).astype(BF16)
        g_hi = g.astype(BF16)
        g_r1 = g - g_hi.astype(F32)
        g_mid = g_r1.astype(BF16)
        g_lo = (g_r1 - g_mid.astype(F32)).astype(BF16)
        gcum = (jnp.dot(tri, g_hi, preferred_element_type=F32)
                + jnp.dot(tri, g_mid, preferred_element_type=F32)
                + jnp.dot(tri, g_lo, preferred_element_type=F32))
        gcol = jnp.where(lane < n_heads, gcum, jnp.where(lane < 2 * n_heads, beta, 0.0))

        for n, u in enumerate(units):
            sq, rows, part = u // nsub, slice((u % nsub) * tl, (u % nsub + 1) * tl), slice(n * tl, (n + 1) * tl)
            qext[u, QBUF_OFF:QBUF_OFF + tl, :] = qkv_pre[part]
            uext[u, DBUF_OFF:DBUF_OFF + tl, :] = ug[part]
            z_ref[sq, rows, :] = z[part]
            gcol_ref[sq, rows, :] = gcol[part]
            if u % nsub:
                qext[u, q_hist, :] = qext[u - 1, q_tail, :]
                uext[u, d_hist, :] = uext[u - 1, d_tail, :]

    def convolve(u):
        sq, rows0 = u // nsub, (u % nsub) * tl
        for s in range(qkv_w // LANES):
            cols = slice(s * LANES, (s + 1) * LANES)
            y = None
            for j in range(kq):
                r0 = QBUF_OFF - (kq - 1) + j
                wj = cw_ref[j * SUBLANES:(j + 1) * SUBLANES, cols]
                t = wj[None] * qext[u, r0:r0 + tl, cols].reshape(tl // SUBLANES, SUBLANES, LANES)
                y = t if y is None else y + t
            y = _silu(y.reshape(tl, LANES))
            if s < 2 * n_heads:
                y = y * lax.rsqrt(jnp.sum(y * y, axis=-1, keepdims=True) + EPS)
            qkv_ref[sq, rows0:rows0 + tl, cols] = y

        span = tl + DBUF_OFF - SUBLANES
        for b in range(1, SUBLANES):
            ushift[b - 1, 0:span, :] = uext[u, b:b + span, :]
        for r in range(tl // DW_ROWS):
            acc = None
            for j in range(kd):
                a, b = divmod(DBUF_OFF - (kd - 1) + j, SUBLANES)
                r0 = a * SUBLANES + r * DW_ROWS
                win = uext[u, r0:r0 + DW_ROWS, :] if b == 0 else ushift[b - 1, r0:r0 + DW_ROWS, :]
                wj = dww_ref[j * SUBLANES:(j + 1) * SUBLANES, :]
                t = wj[None] * win.reshape(DW_ROWS // SUBLANES, SUBLANES, c_b)
                acc = t if acc is None else acc + t
            cpre = acc.reshape(DW_ROWS, c_b) + dwb_ref[...]
            mu = jnp.mean(cpre, axis=-1, keepdims=True)
            xc = cpre - mu
            var = jnp.mean(xc * xc, axis=-1, keepdims=True)
            y = xc * lax.rsqrt(var + EPS) * lng_ref[...] + lnb_ref[...]
            c_ref[sq, rows0 + r * DW_ROWS:rows0 + (r + 1) * DW_ROWS, :] = _silu(y).astype(c_ref.dtype)

    units = list(range(ns * nsub))
    groups = [units] if nsub == 1 else [[u] for u in units]
    for gi, group in enumerate(groups):
        project(group)
        if gi:
            for u in groups[gi - 1]:
                convolve(u)
    for u in groups[-1]:
        convolve(u)

    for sq in range(ns):
        last = sq * nsub + nsub - 1
        qlast = qext[last, q_tail, :]
        qlast_ref[sq] = qlast
        qext[sq * nsub, q_hist, :] = qlast
        dlast = uext[last, d_tail, :]
        dlast_ref[sq] = dlast
        uext[sq * nsub, d_hist, :] = dlast


def _even_pre(x, gain, w_all, conv_w, alog_pad, dtb_pad, dw_w, dw_b, ln_g, ln_b, qbuf, dbuf, *, n_heads, qkv_w, v_w, c_b):
    n_seq, L, D = x.shape
    tl = min(256, L)
    assert L % tl == 0 and tl % DW_ROWS == 0 and tl % CHUNK == 0
    kq, kd = conv_w.shape[0], dw_w.shape[0]
    assert kq - 1 <= QBUF_OFF and kd - 1 <= DBUF_OFF and tl >= kd - 1
    conv_w = jnp.repeat(conv_w, SUBLANES, axis=0)
    dw_w = jnp.repeat(dw_w, SUBLANES, axis=0)
    nw = w_all.shape[1]
    ns = max(d for d in range(1, max(1, 256 // tl) + 1) if n_seq % d == 0)
    nsub = max(d for d in (EVEN_SUB, 2, 1) if (L // tl) % d == 0) if ns == 1 else 1
    rows = nsub * tl
    kern = functools.partial(_even_pre_kernel, ns=ns, nsub=nsub, tl=tl, qkv_w=qkv_w, v_w=v_w, c_b=c_b,
                             n_heads=n_heads)
    seq_blk = lambda w: pl.BlockSpec((ns, rows, w), lambda s, j: (s, j, 0))
    per_seq = lambda r, w: pl.BlockSpec((ns, r, w), lambda s, j: (s, 0, 0))
    return pl.pallas_call(
        kern,
        grid=(n_seq // ns, L // rows),
        in_specs=[seq_blk(D), _const_spec((1, D)), _const_spec((D, nw)), _const_spec(conv_w.shape),
                  _const_spec((1, LANES)), _const_spec((1, LANES)), _const_spec(dw_w.shape),
                  _const_spec((1, c_b)), _const_spec((1, c_b)), _const_spec((1, c_b)),
                  per_seq(kq - 1, qkv_w), per_seq(kd - 1, c_b)],
        out_specs=[seq_blk(qkv_w), seq_blk(LANES), seq_blk(v_w), seq_blk(c_b),
                   per_seq(kq - 1, qkv_w), per_seq(kd - 1, c_b)],
        out_shape=[jax.ShapeDtypeStruct((n_seq, L, qkv_w), F32),
                   jax.ShapeDtypeStruct((n_seq, L, LANES), F32),
                   jax.ShapeDtypeStruct((n_seq, L, v_w), F32),
                   jax.ShapeDtypeStruct((n_seq, L, c_b), BF16),
                   jax.ShapeDtypeStruct((n_seq, kq - 1, qkv_w), F32),
                   jax.ShapeDtypeStruct((n_seq, kd - 1, c_b), F32)],
        scratch_shapes=[pltpu.VMEM((ns * nsub, QBUF_OFF + tl, qkv_w), F32),
                        pltpu.VMEM((ns * nsub, DBUF_OFF + tl, c_b), F32),
                        pltpu.VMEM((SUBLANES - 1, DBUF_OFF + tl, c_b), F32)],
        compiler_params=pltpu.CompilerParams(dimension_semantics=("arbitrary", "arbitrary"),
                                             vmem_limit_bytes=VMEM_LIMIT),
        name="even_pre",
    )(x, gain, w_all, conv_w, alog_pad, dtb_pad, dw_w, dw_b, ln_g, ln_b, qbuf, dbuf)


def _unit_lower_inverses(a_list, row, col):
    x = row ^ col
    eye = (row == col).astype(F32)
    a8 = [jnp.where(x < 8, a, 0.0) for a in a_list]
    a8_2 = [_mm(t, t) for t in a8]
    p = [eye - t for t in a8]
    a8_4 = [_mm(t, t) for t in a8_2]
    p = [t + _mm(t, sq) for t, sq in zip(p, a8_2)]
    d = [t + _mm(t, sq) for t, sq in zip(p, a8_4)]
    s = 8
    while s < CHUNK:
        off = [jnp.where((x >= s) & (x < 2 * s), a, 0.0) for a in a_list]
        od = [_mm(o, t) for o, t in zip(off, d)]
        d = [t - _mm(t, u) for t, u in zip(d, od)]
        s *= 2
    return d


def _delta_kernel(qkv_ref, gcol_ref, z_ref, onorm_ref, sin_ref, o_ref, sout_ref, *, tb, n_heads, chained):
    dk = dv = LANES
    qk_w = n_heads * dk
    n_pairs = tb // PAIR
    if chained:
        @pl.when(pl.program_id(1) == 0)
        def _():
            sout_ref[...] = sin_ref[...]

    row = lax.broadcasted_iota(jnp.int32, (PAIR, PAIR), 0)
    col = lax.broadcasted_iota(jnp.int32, (PAIR, PAIR), 1)
    same = (row ^ col) < CHUNK
    causal = same & (col <= row)
    strict = same & (col < row)
    in_chunk = [(col >= c * CHUNK) & (col < (c + 1) * CHUNK) for c in range(2)]

    tiles = [(p, h) for p in range(n_pairs) for h in range(n_heads)]
    q, k, kb, vb, gc, a, qk = {}, {}, {}, {}, {}, {}, {}
    for p, h in tiles:
        rows = slice(p * PAIR, (p + 1) * PAIR)
        q[p, h] = qkv_ref[0, rows, h * dk:(h + 1) * dk] * (dk ** -0.5)
        k[p, h] = qkv_ref[0, rows, qk_w + h * dk:qk_w + (h + 1) * dk]
        gc[p, h] = jnp.broadcast_to(gcol_ref[0, rows, h:h + 1], (PAIR, PAIR))
        bt = jnp.broadcast_to(gcol_ref[0, rows, n_heads + h:n_heads + h + 1], (PAIR, PAIR))
        kb[p, h] = k[p, h] * bt
        vb[p, h] = qkv_ref[0, rows, 2 * qk_w + h * dv:2 * qk_w + (h + 1) * dv] * bt
    for t in tiles:
        diff = gc[t] - gc[t].T
        decay = jnp.where(causal, jnp.exp(jnp.where(causal, diff, 0.0)), 0.0)
        a[t] = jnp.where(strict, _mm_nt(kb[t], k[t]) * decay, 0.0)
        qk[t] = jnp.where(causal, _mm_nt(q[t], k[t]) * decay, 0.0)
    inv = dict(zip(tiles, _unit_lower_inverses([a[t] for t in tiles], row, col)))
    uw, kg_t, gl = {}, {}, {}
    for t in tiles:
        eg = jnp.exp(gc[t])
        uw[t] = _mm(inv[t], jnp.concatenate([vb[t], kb[t] * eg], axis=1))
        gl_rows = [jnp.broadcast_to(gc[t][(c + 1) * CHUNK - 1:(c + 1) * CHUNK, :], (PAIR, PAIR)) for c in range(2)]
        gl[t] = [jnp.exp(g) for g in gl_rows]
        gcl = jnp.where(row < CHUNK, gl_rows[0], gl_rows[1])
        kg_t[t] = (k[t] * jnp.exp(gcl - gc[t])).T
        q[t] = q[t] * eg
    qk_uw = {t: _mm(qk[t], uw[t]) for t in tiles}
    kg_uw = {(t, c): _mm(jnp.where(in_chunk[c], kg_t[t], 0.0), uw[t])
             for t in tiles for c in range(2)}

    state = [sout_ref[0, h] for h in range(n_heads)] if chained else None
    for p in range(n_pairs):
        rows = slice(p * PAIR, (p + 1) * PAIR)
        o_parts = {h: [] for h in range(n_heads)}
        for c in range(2):
            r = slice(c * CHUNK, (c + 1) * CHUNK)
            for h in range(n_heads):
                t = (p, h)
                s_old = state[h] if chained else sin_ref[c, h]
                q_eff = q[t][r] - qk_uw[t][r, dv:]
                xs = _mm(jnp.concatenate([kg_uw[t, c][:, dv:], q_eff], axis=0), s_old)
                o_parts[h].append(qk_uw[t][r, :dv] + xs[dk:])
                s_new = s_old * gl[t][c] + (kg_uw[t, c][:, :dv] - xs[:dk])
                if chained:
                    state[h] = s_new
                else:
                    sout_ref[c, h] = s_new
        for h in range(n_heads):
            o = jnp.concatenate(o_parts[h], axis=0)
            zz = z_ref[0, rows, h * dv:(h + 1) * dv].astype(F32)
            y = (o * _rms_scale(o) * onorm_ref[...]) * _silu(zz)
            o_ref[0, rows, h * dv:(h + 1) * dv] = y.astype(o_ref.dtype)
    if chained:
        for h in range(n_heads):
            sout_ref[0, h] = state[h]


def _delta(qkv, gcol, z, onorm, s0, *, n_heads):
    n_seq, L, qkv_w = qkv.shape
    v_w = z.shape[-1]
    assert qkv_w == 3 * n_heads * LANES and v_w == n_heads * LANES
    if L % PAIR == 0:
        chained, n_grp, Lg, per_grp = True, n_seq, L, 1
    else:
        assert L == CHUNK and n_seq % 2 == 0
        chained, n_grp, Lg, per_grp = False, n_seq // 2, PAIR, 2
        qkv, gcol, z = (t.reshape(n_grp, PAIR, t.shape[-1]) for t in (qkv, gcol, z))
    tb = min(512, Lg)
    assert Lg % tb == 0
    kern = functools.partial(_delta_kernel, tb=tb, n_heads=n_heads, chained=chained)
    blk = lambda w: pl.BlockSpec((1, tb, w), lambda s, j: (s, j, 0))
    st = pl.BlockSpec((per_grp, n_heads, LANES, LANES), lambda s, j: (s, 0, 0, 0))
    o, s_new = pl.pallas_call(
        kern,
        grid=(n_grp, Lg // tb),
        in_specs=[blk(qkv_w), blk(LANES), blk(v_w), _const_spec((1, LANES)), st],
        out_specs=[blk(v_w), st],
        out_shape=[jax.ShapeDtypeStruct((n_grp, Lg, v_w), BF16),
                   jax.ShapeDtypeStruct(s0.shape, F32)],
        compiler_params=pltpu.CompilerParams(dimension_semantics=("arbitrary", "arbitrary"),
                                             vmem_limit_bytes=VMEM_LIMIT),
        name="delta_rule",
    )(qkv, gcol, z, onorm, s0)
    return o.reshape(n_seq, L, v_w), s_new


def _out_mlp_kernel(*refs, n_a, steps, tf):
    n_s = len(steps)
    per = 1 + n_a
    w_ref, g_ref, wup_ref, wdn_ref = refs[n_s * per:n_s * per + 4]
    o_refs = refs[n_s * per + 4:]

    def tile(x_ref, a_refs, o_ref):
        a = a_refs[0][...] if n_a == 1 else jnp.concatenate([a_ref[...] for a_ref in a_refs], axis=-1)
        x1 = x_ref[...] + jnp.dot(a, w_ref[...], preferred_element_type=F32)
        h = (x1 * _rms_scale(x1) * g_ref[...]).astype(BF16)
        acc = x1
        for f in range(wup_ref.shape[1] // tf):
            r = jnp.maximum(jnp.dot(h, wup_ref[:, f * tf:(f + 1) * tf], preferred_element_type=F32), 0.0)
            acc = acc + jnp.dot((r * r).astype(BF16), wdn_ref[f * tf:(f + 1) * tf, :],
                                preferred_element_type=F32)
        o_ref[...] = acc

    i = pl.program_id(0)
    first = 0
    for s in range(n_s):
        run = functools.partial(tile, refs[s * per], refs[s * per + 1:(s + 1) * per], o_refs[s])
        if n_s == 1:
            run()
        else:
            pl.when((i >= first) & (i < first + steps[s]))(run)
        first += steps[s]


def _out_mlp(streams, w_out, gain, w_up, w_down):
    D = streams[0][0].shape[-1]
    flat = [(x.reshape(-1, D), [a.reshape(-1, a.shape[-1]) for a in a_list]) for x, a_list in streams]
    tm = min([512] + [x.shape[0] for x, _ in flat])
    assert all(x.shape[0] % tm == 0 for x, _ in flat)
    steps = [x.shape[0] // tm for x, _ in flat]
    n_a = len(flat[0][1])
    F = w_up.shape[1]
    tf = min(1024, F)
    assert F % tf == 0
    kern = functools.partial(_out_mlp_kernel, n_a=n_a, steps=tuple(steps), tf=tf)

    def rows(w, first, n):
        return pl.BlockSpec((tm, w), lambda i: (jnp.minimum(jnp.maximum(i - first, 0), n - 1), 0))

    in_specs, args, out_specs, first = [], [], [], 0
    for (x, a_list), n in zip(flat, steps):
        in_specs += [rows(D, first, n)] + [rows(a.shape[1], first, n) for a in a_list]
        args += [x] + a_list
        out_specs.append(rows(D, first, n))
        first += n
    outs = pl.pallas_call(
        kern,
        grid=(sum(steps),),
        in_specs=in_specs + [_const_spec(w_out.shape), _const_spec((1, D)), _const_spec(w_up.shape),
                             _const_spec(w_down.shape)],
        out_specs=out_specs,
        out_shape=[jax.ShapeDtypeStruct(x.shape, F32) for x, _ in flat],
        compiler_params=pltpu.CompilerParams(dimension_semantics=("arbitrary",), vmem_limit_bytes=VMEM_LIMIT),
        name="out_mlp",
    )(*args, w_out, gain, w_up, w_down)
    return [o.reshape(x.shape) for o, (x, _) in zip(outs, streams)]


def _odd_pre_kernel(x_ref, g_ref, w_ref, qn_ref, kn_ref, q_ref, k_ref, v_ref, *mxu_copies, n_heads):
    dh = LANES
    hd = n_heads * dh
    ns, tl, d_model = x_ref.shape
    x = x_ref[...].reshape(ns * tl, d_model)
    h = (x * _rms_scale(x) * g_ref[...]).astype(BF16)
    q = jnp.dot(h, w_ref[:, 0:hd], preferred_element_type=F32)
    k = jnp.dot(h, w_ref[:, hd:2 * hd], preferred_element_type=F32)
    v = jnp.dot(h, w_ref[:, 2 * hd:3 * hd], preferred_element_type=F32)
    per_seq = lambda t: t.reshape(ns, tl, dh)
    for hh in range(n_heads):
        cols = slice(hh * dh, (hh + 1) * dh)
        qh = q[:, cols]
        kh = k[:, cols]
        q_ref[:, hh] = per_seq((qh * _rms_scale(qh) * qn_ref[...] * (dh ** -0.5 * LOG2E)).astype(q_ref.dtype))
        kh = kh * _rms_scale(kh) * kn_ref[...]
        k_ref[:, hh] = per_seq(kh)
        v_ref[:, hh] = per_seq(v[:, cols])
        if mxu_copies:
            mxu_copies[0][:, hh] = per_seq(kh.astype(BF16))
            mxu_copies[1][:, hh] = per_seq(v[:, cols].astype(BF16))


def _odd_pre(x, gain, w_qkv, qn, kn, *, n_heads, mxu_copies):
    B, L, D = x.shape
    tl = min(256, L)
    assert L % tl == 0 and w_qkv.shape[1] == 3 * n_heads * LANES
    kern = functools.partial(_odd_pre_kernel, n_heads=n_heads)
    ns = max(d for d in range(1, max(1, 256 // tl) + 1) if B % d == 0)
    head_major = pl.BlockSpec((ns, n_heads, tl, LANES), lambda b, j: (b, 0, j, 0))
    hm_shape = (B, n_heads, L, LANES)
    n_copies = 2 if mxu_copies else 0
    return pl.pallas_call(
        kern,
        grid=(B // ns, L // tl),
        in_specs=[pl.BlockSpec((ns, tl, D), lambda b, j: (b, j, 0)), _const_spec((1, D)),
                  _const_spec(w_qkv.shape), _const_spec((1, LANES)), _const_spec((1, LANES))],
        out_specs=[head_major] * (3 + n_copies),
        out_shape=([jax.ShapeDtypeStruct(hm_shape, BF16), jax.ShapeDtypeStruct(hm_shape, F32),
                    jax.ShapeDtypeStruct(hm_shape, F32)] + [jax.ShapeDtypeStruct(hm_shape, BF16)] * n_copies),
        compiler_params=pltpu.CompilerParams(dimension_semantics=("arbitrary", "arbitrary"),
                                             vmem_limit_bytes=VMEM_LIMIT),
        name="odd_pre",
    )(x, gain, w_qkv, qn, kn)


def _suffix_ones(tk):
    j = lax.broadcasted_iota(jnp.int32, (2 * tk, tk), 0) & (tk - 1)
    s = lax.broadcasted_iota(jnp.int32, (2 * tk, tk), 1)
    return (j > s).astype(BF16)


def _sb_logits(q, k_blk, u, below=None):
    zs = [_mm_nt(a, b) for a, b in zip(q, k_blk)] if isinstance(q, (list, tuple)) else [_mm_nt(q, k_blk)]
    z = jnp.concatenate(zs, axis=0) if len(zs) > 1 else zs[0]
    t = jnp.maximum(z, 0.0) + jnp.log(1.0 + jnp.exp2(-jnp.abs(z))) * LOG2E
    if below is not None:
        t = jnp.where(below, t, 0.0)
    t_hi = t.astype(BF16)
    t_lo = (t - t_hi.astype(F32)).astype(BF16)
    later = jnp.dot(jnp.concatenate([t_hi, t_lo], axis=1), u, preferred_element_type=F32)
    return z, t, later


def _sb_weights(z, t, later, off, v_blk, acc):
    wts = jnp.exp2(z - t - (later + off))
    own = later[:, 0:1] + t[:, 0:1]
    if not isinstance(v_blk, (list, tuple)):
        return own, acc + _mm(wts, v_blk)
    rows = z.shape[0] // len(v_blk)
    return own, [a + _mm(wts[n * rows:(n + 1) * rows], v) for n, (a, v) in enumerate(zip(acc, v_blk))]


def _sb_block(q, k_blk, v_blk, u, off, acc, below=None):
    return _sb_weights(*_sb_logits(q, k_blk, u, below), off, v_blk, acc)


def _sb_kernel(*refs, tq, tkp, n_past, n_q, hb, qb, n_prev):
    lazy_past = bool(n_past) and n_q == 1
    if lazy_past:
        q_ref, kn_ref, vn_ref, kl_ref, vl_ref, kp_ref, vp_ref, o_ref, kbuf, vbuf, sem = refs
    elif n_past:
        q_ref, kn_ref, vn_ref, kp_ref, vp_ref, o_ref = refs
    else:
        q_ref, kn_ref, vn_ref, o_ref = refs
    u_self = _suffix_ones(tq)
    u_past = _suffix_ones(tkp) if n_past else None
    row = lax.broadcasted_iota(jnp.int32, (tq, tq), 0)
    col = lax.broadcasted_iota(jnp.int32, (tq, tq), 1)
    below = col < row
    not_below = jnp.where(below, 0.0, MASKED)

    walks = []
    if n_q == 1:
        heads = range(hb)
        qs = [q_ref[0, hh] for hh in heads]
        below_all = jnp.concatenate([below] * hb, axis=0)
        spent, accs = _sb_block(qs, [kn_ref[0, hh] for hh in heads], [vn_ref[0, hh] for hh in heads], u_self,
                                jnp.concatenate([not_below] * hb, axis=0),
                                [jnp.zeros((tq, LANES), F32)] * hb, below_all)
        if lazy_past:
            own, accs = _sb_block(qs, [kl_ref[0, hh] for hh in heads], [vl_ref[0, hh] for hh in heads], u_past,
                                  spent, accs)
            spent = spent + own
        walks = [(hh, 0, qs[hh], spent[hh * tq:(hh + 1) * tq], accs[hh], -1, n_past - 2) for hh in heads]
    chains = [(hh, w) for hh in range(hb) for w in range(qb) if n_q > 1]
    logits = {}
    for c, (hh, w) in enumerate(chains):
        i = pl.program_id(2) * qb + w
        q = q_ref[0, hh, w * tq:(w + 1) * tq, :]
        for d in range(n_prev + 1):
            blk = pl.multiple_of(jnp.maximum(i - d, 0) * tq, tq)
            logits[c, d] = _sb_logits(q, kn_ref[0, hh, pl.ds(blk, tq), :], u_self, below if d == 0 else None)
    for c, (hh, w) in enumerate(chains):
        i = pl.program_id(2) * qb + w
        q = q_ref[0, hh, w * tq:(w + 1) * tq, :]
        start = pl.multiple_of(i * tq, tq)
        spent, acc = _sb_weights(*logits[c, 0], not_below, vn_ref[0, hh, pl.ds(start, tq), :],
                                 jnp.zeros((tq, LANES), F32))
        for d in range(1, n_prev + 1):
            prev = pl.multiple_of(jnp.maximum(i - d, 0) * tq, tq)
            has_prev = jnp.full((tq, 1), i, jnp.int32) >= d
            own, acc = _sb_weights(*logits[c, d], jnp.where(has_prev, spent, MASKED),
                                   vn_ref[0, hh, pl.ds(prev, tq), :], acc)
            spent = spent + jnp.where(has_prev, own, 0.0)
        walks.append((hh, w, q, spent, acc, i - n_prev - 1, n_past - 1))

    def walk(q, k_ref, v_ref, hh, tk, u, first, spent, acc, from_hbm=False):
        def cond(st):
            return (st[0] >= 0) & (jnp.min(st[1]) < -LOG2_W_FLOOR)

        def body(st):
            j, spent, acc = st
            s0 = pl.multiple_of(j * tk, tk)
            if from_hbm:
                head = pl.program_id(1) * hb + hh
                copies = [pltpu.make_async_copy(src.at[pl.program_id(0), head, pl.ds(s0, tk), :], dst, sem.at[n])
                          for n, (src, dst) in enumerate(((k_ref, kbuf), (v_ref, vbuf)))]
                for cp in copies:
                    cp.start()
                for cp in copies:
                    cp.wait()
                k_blk, v_blk = kbuf[...], vbuf[...]
            else:
                k_blk, v_blk = k_ref[0, hh, pl.ds(s0, tk), :], v_ref[0, hh, pl.ds(s0, tk), :]
            own, acc = _sb_block(q, k_blk, v_blk, u, spent, acc)
            return j - 1, spent + own, acc

        _, spent, acc = lax.while_loop(cond, body, (first, spent, acc))
        return spent, acc

    for hh, w, _, _, acc, _, _ in walks:
        o_ref[0, w * tq:(w + 1) * tq, hh * LANES:(hh + 1) * LANES] = acc.astype(o_ref.dtype)
    if n_q == 1 and n_past <= 1:
        return

    least = functools.reduce(jnp.minimum, [wk[3] for wk in walks])

    @pl.when(jnp.min(least) < -LOG2_W_FLOOR)
    def _():
        for hh, w, q, spent, acc, first_self, first_past in walks:
            if n_q > 1:
                spent, acc = walk(q, kn_ref, vn_ref, hh, tq, u_self, first_self, spent, acc)
            if n_past:
                spent, acc = walk(q, kp_ref, vp_ref, hh, tkp, u_past, jnp.int32(first_past), spent, acc,
                                  from_hbm=lazy_past)
            o_ref[0, w * tq:(w + 1) * tq, hh * LANES:(hh + 1) * LANES] = acc.astype(o_ref.dtype)


def _stick_breaking(q, k_new, v_new, k_past, v_past):
    B, H, L, dh = q.shape
    assert dh == LANES
    tq = min(SB_TQ, L)
    assert L % tq == 0
    n_q = L // tq
    assert tq & (tq - 1) == 0
    if n_q == 1:
        hb, qb = H, 1
    else:
        kv_bytes = 2 * 2 * L * LANES * k_new.dtype.itemsize
        hb = max(d for d in (4, 2, 1) if H % d == 0 and d * kv_bytes <= SB_KV_VMEM)
        qb = max(d for d in (SB_CHAINS // hb, 2, 1) if n_q % d == 0)
    n_prev = 0 if n_q == 1 else SB_PREV
    assert H % hb == 0 and n_q % qb == 0
    args = [q, k_new, v_new]
    full = lambda n: pl.BlockSpec((1, hb, n, LANES), lambda b, h, i: (b, h, 0, 0))
    in_specs = [pl.BlockSpec((1, hb, tq * qb, LANES), lambda b, h, i: (b, h, i, 0)), full(L), full(L)]
    n_past, tkp = 0, 0
    scratch = []
    if k_past is not None:
        P = k_past.shape[2]
        tkp = min(256, P)
        assert P % tkp == 0 and tkp & (tkp - 1) == 0
        n_past = P // tkp
        if n_q == 1:
            last = pl.BlockSpec((1, hb, tkp, LANES), lambda b, h, i: (b, h, n_past - 1, 0))
            hbm = pl.BlockSpec(memory_space=pl.ANY)
            args += [k_past, v_past, k_past, v_past]
            in_specs += [last, last, hbm, hbm]
            scratch = [pltpu.VMEM((tkp, LANES), k_past.dtype), pltpu.VMEM((tkp, LANES), v_past.dtype),
                       pltpu.SemaphoreType.DMA((2,))]
        else:
            args += [k_past, v_past]
            in_specs += [full(P), full(P)]
    kern = functools.partial(_sb_kernel, tq=tq, tkp=tkp, n_past=n_past, n_q=n_q, hb=hb, qb=qb, n_prev=n_prev)
    return pl.pallas_call(
        kern,
        grid=(B, H // hb, n_q // qb),
        in_specs=in_specs,
        out_specs=pl.BlockSpec((1, tq * qb, hb * LANES), lambda b, h, i: (b, i, h)),
        scratch_shapes=scratch,
        out_shape=jax.ShapeDtypeStruct((B, L, H * LANES), BF16),
        compiler_params=pltpu.CompilerParams(dimension_semantics=("arbitrary", "arbitrary", "arbitrary"),
                                             vmem_limit_bytes=VMEM_LIMIT),
        name="stick_breaking",
    )(*args)


def _stack(parts):
    return parts[0][None] if len(parts) == 1 else jnp.stack(parts)


def _pad_lanes(v):
    return jnp.zeros((1, LANES), F32).at[0, :v.shape[0]].set(v.astype(F32))


def _even_mixers(x, s0, qbuf, dbuf, wts, gain_mix):
    (w_all, conv_w, alog_pad, dtb_pad, onorm, dw_w, dw_b, ln_g, ln_b, n_heads, qkv_w, v_w, c_b) = wts
    qkv, gcol, z, c, qlast, dlast = _even_pre(x, gain_mix, w_all, conv_w, alog_pad, dtb_pad, dw_w, dw_b, ln_g, ln_b,
                                              qbuf, dbuf, n_heads=n_heads, qkv_w=qkv_w, v_w=v_w, c_b=c_b)
    o, s_new = _delta(qkv, gcol, z, onorm, s0, n_heads=n_heads)
    return [o, c], s_new, qlast, dlast


def _odd_mixer(x, k_past, v_past, wts, gain_mix):
    w_qkv, qn, kn, n_heads = wts
    mxu_copies = x.shape[1] > SB_TQ
    q, k, v, *kv_mxu = _odd_pre(x, gain_mix, w_qkv, qn, kn, n_heads=n_heads, mxu_copies=mxu_copies)
    o = _stick_breaking(q, *(kv_mxu or (k, v)), k_past, v_past)
    return [o], k, v


def kernel(x_prompt, x_sample, state_delta, state_qkv_conv, state_dw_conv, cache_k, cache_v, norm_mix, norm_mlp, w_in_e, conv_qkv_e, a_log_e, dt_bias_e, onorm_e, dw_w_e, dw_b_e, ln_g_e, ln_b_e, w_out_e, w_qkv_o, qn_o, kn_o, w_out_o, w_up, w_down):
    depth = norm_mix.shape[0]
    bp = x_prompt.shape[0]
    xp, xs = x_prompt, x_sample
    pd, pq, pw, pk, pv = [], [], [], [], []
    sd, sq, sw, sk, sv = [], [], [], [], []
    row = lambda v: v.astype(F32).reshape(1, -1)
    for i in range(depth):
        gm, gl = row(norm_mix[i]), row(norm_mlp[i])
        wu, wd = w_up[i].astype(BF16), w_down[i].astype(BF16)
        if i % 2 == 0:
            e = i // 2
            n_heads = a_log_e.shape[1]
            qkv_w = conv_qkv_e.shape[2]
            c_b = dw_w_e.shape[2]
            v_w = n_heads * onorm_e.shape[1]
            w_in = w_in_e[e]
            ab0 = qkv_w + v_w
            ab = jnp.zeros((w_in.shape[0], LANES), w_in.dtype).at[:, :2 * n_heads].set(w_in[:, ab0:ab0 + 2 * n_heads])
            w_all = jnp.concatenate([w_in[:, :ab0], w_in[:, ab0 + 2 * n_heads:], ab], axis=1).astype(BF16)
            w_out = w_out_e[e].astype(BF16)
            wts = (w_all, conv_qkv_e[e].astype(F32), _pad_lanes(a_log_e[e]), _pad_lanes(dt_bias_e[e]),
                   row(onorm_e[e]), dw_w_e[e].astype(F32), row(dw_b_e[e]), row(ln_g_e[e]), row(ln_b_e[e]),
                   n_heads, qkv_w, v_w, c_b)
            kq, kd = conv_qkv_e.shape[1], dw_w_e.shape[1]
            s0 = jnp.zeros((bp,) + state_delta.shape[2:], F32)
            qb0 = jnp.zeros((bp, kq - 1, qkv_w), F32)
            db0 = jnp.zeros((bp, kd - 1, c_b), F32)
            ap, d1, q1, c1 = _even_mixers(xp, s0, qb0, db0, wts, gm)
            as_, d2, q2, c2 = _even_mixers(xs, state_delta[e].astype(F32), state_qkv_conv[e].astype(F32),
                                           state_dw_conv[e].astype(F32), wts, gm)
            pd.append(d1); pq.append(q1); pw.append(c1)
            sd.append(d2); sq.append(q2); sw.append(c2)
        else:
            o = i // 2
            n_heads = cache_k.shape[2]
            w_out = w_out_o[o].astype(BF16)
            wts = (w_qkv_o[o].astype(BF16), row(qn_o[o]), row(kn_o[o]), n_heads)
            ap, k1, v1 = _odd_mixer(xp, None, None, wts, gm)
            as_, k2, v2 = _odd_mixer(xs, cache_k[o].astype(F32), cache_v[o].astype(F32), wts, gm)
            pk.append(k1); pv.append(v1)
            sk.append(k2); sv.append(v2)
        xp, xs = _out_mlp([(xp, ap), (xs, as_)], w_out, gl, wu, wd)
    return (xp, xs,
            *(_stack(t) for t in (pd, pq, pw, pk, pv, sd, sq, sw, sk, sv)))
```

```python
import functools

import jax
import jax.numpy as jnp
from jax import lax
from jax.experimental import pallas as pl
from jax.experimental.pallas import tpu as pltpu

F32 = jnp.float32
BF16 = jnp.bfloat16
EPS = 1e-6
CHUNK = 64
LANES = 128
PAIR = 2 * CHUNK
QBUF_OFF = 8
DBUF_OFF = 32
DW_ROWS = 32
SUBLANES = 8
PROJ_ROWS = 256
EVEN_SUB = 4
DELTA_ROWS = 512
MLP_ROWS = 512
MLP_FF = 1024
VMEM_LIMIT = 56 * 1024 * 1024
LOG2E = 1.4426950408889634
LOG2_W_FLOOR = -151.0
SB_TQ = 256
SB_PREV = 1
SB_KV_VMEM = 36 * 1024 * 1024
SB_CHAINS = 16
MASKED = 1e30


def _mm(a, b):
    return jnp.dot(a.astype(BF16), b.astype(BF16), preferred_element_type=F32)


def _mm_nt(a, b):
    return lax.dot_general(a.astype(BF16), b.astype(BF16), (((1,), (1,)), ((), ())),
                           preferred_element_type=F32)


def _sigmoid(x):
    return 0.5 * jnp.tanh(0.5 * x) + 0.5


def _silu(x):
    h = 0.5 * x
    return h + h * jnp.tanh(h)


def _softplus(x):
    return jnp.maximum(x, 0.0) + jnp.log1p(jnp.exp(-jnp.abs(x)))


def _rms_scale(x):
    return lax.rsqrt(jnp.mean(x * x, axis=-1, keepdims=True) + EPS)


def _const_spec(shape):
    nd = len(shape)
    return pl.BlockSpec(shape, lambda *_: (0,) * nd, pipeline_mode=pl.Buffered(1))


def _even_pre_kernel(x_ref, g_ref, w_ref, cw_ref, alog_ref, dtb_ref, dww_ref, dwb_ref, lng_ref, lnb_ref,
                     qbuf_ref, dbuf_ref,
                     qkv_ref, gcol_ref, z_ref, c_ref, qlast_ref, dlast_ref,
                     qext, uext, ushift, *, ns, nsub, tl, qkv_w, v_w, c_b, n_heads):
    kq = cw_ref.shape[0] // SUBLANES
    kd = dww_ref.shape[0] // SUBLANES
    glu0 = qkv_w + v_w
    q_hist = slice(QBUF_OFF - (kq - 1), QBUF_OFF)
    d_hist = slice(DBUF_OFF - (kd - 1), DBUF_OFF)
    q_tail = slice(QBUF_OFF + tl - (kq - 1), QBUF_OFF + tl)
    d_tail = slice(DBUF_OFF + tl - (kd - 1), DBUF_OFF + tl)

    @pl.when(pl.program_id(1) == 0)
    def _():
        for sq in range(ns):
            qext[sq * nsub, q_hist, :] = qbuf_ref[sq]
            uext[sq * nsub, 0:d_hist.start, :] = jnp.zeros((d_hist.start, c_b), F32)
            uext[sq * nsub, d_hist, :] = dbuf_ref[sq]

    def project(units):
        m = len(units) * tl
        x = jnp.concatenate([x_ref[u // nsub, (u % nsub) * tl:(u % nsub + 1) * tl, :] for u in units], axis=0)
        h = (x * _rms_scale(x) * g_ref[...]).astype(BF16)
        qkv_pre = jnp.dot(h, w_ref[:, 0:qkv_w], preferred_element_type=F32)
        z = jnp.dot(h, w_ref[:, qkv_w:glu0], preferred_element_type=F32)
        ga = jnp.dot(h, w_ref[:, glu0:glu0 + c_b], preferred_element_type=F32)
        gb = jnp.dot(h, w_ref[:, glu0 + c_b:glu0 + 2 * c_b], preferred_element_type=F32)
        ug = ga * _sigmoid(gb)
        ab = jnp.dot(h, w_ref[:, glu0 + 2 * c_b:glu0 + 2 * c_b + LANES], preferred_element_type=F32)

        lane = lax.broadcasted_iota(jnp.int32, (m, LANES), 1)
        g = -jnp.exp(alog_ref[...]) * _softplus(ab + dtb_ref[...])
        g = jnp.where(lane < n_heads, g, 0.0)
        beta = _sigmoid(ab)
        row = lax.broadcasted_iota(jnp.int32, (m, m), 0)
        col = lax.broadcasted_iota(jnp.int32, (m, m), 1)
        tri = (((row ^ col) < CHUNK) & (col <= row)).astype(BF16)
        g_hi = g.astype(BF16)
        g_r1 = g - g_hi.astype(F32)
        g_mid = g_r1.astype(BF16)
        g_lo = (g_r1 - g_mid.astype(F32)).astype(BF16)
        gcum = (jnp.dot(tri, g_hi, preferred_element_type=F32)
                + jnp.dot(tri, g_mid, preferred_element_type=F32)
                + jnp.dot(tri, g_lo, preferred_element_type=F32))
        gcol = jnp.where(lane < n_heads, gcum, jnp.where(lane < 2 * n_heads, beta, 0.0))

        for n, u in enumerate(units):
            sq, rows, part = u // nsub, slice((u % nsub) * tl, (u % nsub + 1) * tl), slice(n * tl, (n + 1) * tl)
            qext[u, QBUF_OFF:QBUF_OFF + tl, :] = qkv_pre[part]
            uext[u, DBUF_OFF:DBUF_OFF + tl, :] = ug[part]
            z_ref[sq, rows, :] = z[part]
            gcol_ref[sq, rows, :] = gcol[part]
            if u % nsub:
                qext[u, q_hist, :] = qext[u - 1, q_tail, :]
                uext[u, d_hist, :] = uext[u - 1, d_tail, :]

    def convolve(u):
        sq, rows0 = u // nsub, (u % nsub) * tl
        for s in range(qkv_w // LANES):
            cols = slice(s * LANES, (s + 1) * LANES)
            y = None
            for j in range(kq):
                r0 = QBUF_OFF - (kq - 1) + j
                wj = cw_ref[j * SUBLANES:(j + 1) * SUBLANES, cols]
                t = wj[None] * qext[u, r0:r0 + tl, cols].reshape(tl // SUBLANES, SUBLANES, LANES)
                y = t if y is None else y + t
            y = _silu(y.reshape(tl, LANES))
            if s < 2 * n_heads:
                y = y * lax.rsqrt(jnp.sum(y * y, axis=-1, keepdims=True) + EPS)
            qkv_ref[sq, rows0:rows0 + tl, cols] = y

        span = tl + DBUF_OFF - SUBLANES
        for b in range(1, SUBLANES):
            ushift[b - 1, 0:span, :] = uext[u, b:b + span, :]
        for r in range(tl // DW_ROWS):
            acc = None
            for j in range(kd):
                a, b = divmod(DBUF_OFF - (kd - 1) + j, SUBLANES)
                r0 = a * SUBLANES + r * DW_ROWS
                win = uext[u, r0:r0 + DW_ROWS, :] if b == 0 else ushift[b - 1, r0:r0 + DW_ROWS, :]
                wj = dww_ref[j * SUBLANES:(j + 1) * SUBLANES, :]
                t = wj[None] * win.reshape(DW_ROWS // SUBLANES, SUBLANES, c_b)
                acc = t if acc is None else acc + t
            cpre = acc.reshape(DW_ROWS, c_b) + dwb_ref[...]
            mu = jnp.mean(cpre, axis=-1, keepdims=True)
            xc = cpre - mu
            var = jnp.mean(xc * xc, axis=-1, keepdims=True)
            y = xc * lax.rsqrt(var + EPS) * lng_ref[...] + lnb_ref[...]
            c_ref[sq, rows0 + r * DW_ROWS:rows0 + (r + 1) * DW_ROWS, :] = _silu(y).astype(c_ref.dtype)

    units = list(range(ns * nsub))
    groups = [units] if nsub == 1 else [[u] for u in units]
    for gi, group in enumerate(groups):
        project(group)
        if gi:
            for u in groups[gi - 1]:
                convolve(u)
    for u in groups[-1]:
        convolve(u)

    for sq in range(ns):
        last = sq * nsub + nsub - 1
        qlast = qext[last, q_tail, :]
        qlast_ref[sq] = qlast
        qext[sq * nsub, q_hist, :] = qlast
        dlast = uext[last, d_tail, :]
        dlast_ref[sq] = dlast
        uext[sq * nsub, d_hist, :] = dlast


def _even_pre(x, gain, w_all, conv_w, alog_pad, dtb_pad, dw_w, dw_b, ln_g, ln_b, qbuf, dbuf, *, n_heads, qkv_w, v_w, c_b):
    n_seq, L, D = x.shape
    tl = min(PROJ_ROWS, L)
    assert L % tl == 0 and tl % DW_ROWS == 0 and tl % CHUNK == 0
    kq, kd = conv_w.shape[0], dw_w.shape[0]
    assert kq - 1 <= QBUF_OFF and kd - 1 <= DBUF_OFF and tl >= kd - 1
    conv_w = jnp.repeat(conv_w, SUBLANES, axis=0)
    dw_w = jnp.repeat(dw_w, SUBLANES, axis=0)
    nw = w_all.shape[1]
    ns = max(d for d in range(1, max(1, PROJ_ROWS // tl) + 1) if n_seq % d == 0)
    nsub = max(d for d in (EVEN_SUB, 2, 1) if (L // tl) % d == 0) if ns == 1 else 1
    rows = nsub * tl
    kern = functools.partial(_even_pre_kernel, ns=ns, nsub=nsub, tl=tl, qkv_w=qkv_w, v_w=v_w, c_b=c_b,
                             n_heads=n_heads)
    seq_blk = lambda w: pl.BlockSpec((ns, rows, w), lambda s, j: (s, j, 0))
    per_seq = lambda r, w: pl.BlockSpec((ns, r, w), lambda s, j: (s, 0, 0))
    return pl.pallas_call(
        kern,
        grid=(n_seq // ns, L // rows),
        in_specs=[seq_blk(D), _const_spec((1, D)), _const_spec((D, nw)), _const_spec(conv_w.shape),
                  _const_spec((1, LANES)), _const_spec((1, LANES)), _const_spec(dw_w.shape),
                  _const_spec((1, c_b)), _const_spec((1, c_b)), _const_spec((1, c_b)),
                  per_seq(kq - 1, qkv_w), per_seq(kd - 1, c_b)],
        out_specs=[seq_blk(qkv_w), seq_blk(LANES), seq_blk(v_w), seq_blk(c_b),
                   per_seq(kq - 1, qkv_w), per_seq(kd - 1, c_b)],
        out_shape=[jax.ShapeDtypeStruct((n_seq, L, qkv_w), F32),
                   jax.ShapeDtypeStruct((n_seq, L, LANES), F32),
                   jax.ShapeDtypeStruct((n_seq, L, v_w), F32),
                   jax.ShapeDtypeStruct((n_seq, L, c_b), BF16),
                   jax.ShapeDtypeStruct((n_seq, kq - 1, qkv_w), F32),
                   jax.ShapeDtypeStruct((n_seq, kd - 1, c_b), F32)],
        scratch_shapes=[pltpu.VMEM((ns * nsub, QBUF_OFF + tl, qkv_w), F32),
                        pltpu.VMEM((ns * nsub, DBUF_OFF + tl, c_b), F32),
                        pltpu.VMEM((SUBLANES - 1, DBUF_OFF + tl, c_b), F32)],
        compiler_params=pltpu.CompilerParams(dimension_semantics=("arbitrary", "arbitrary"),
                                             vmem_limit_bytes=VMEM_LIMIT),
        name="even_pre",
    )(x, gain, w_all, conv_w, alog_pad, dtb_pad, dw_w, dw_b, ln_g, ln_b, qbuf, dbuf)


def _unit_lower_inverses(a_list, row, col):
    x = row ^ col
    eye = (row == col).astype(F32)
    a8 = [jnp.where(x < 8, a, 0.0) for a in a_list]
    a8_2 = [_mm(t, t) for t in a8]
    p = [eye - t for t in a8]
    a8_4 = [_mm(t, t) for t in a8_2]
    p = [t + _mm(t, sq) for t, sq in zip(p, a8_2)]
    d = [t + _mm(t, sq) for t, sq in zip(p, a8_4)]
    s = 8
    while s < CHUNK:
        off = [jnp.where((x >= s) & (x < 2 * s), a, 0.0) for a in a_list]
        od = [_mm(o, t) for o, t in zip(off, d)]
        d = [t - _mm(t, u) for t, u in zip(d, od)]
        s *= 2
    return d


def _delta_kernel(qkv_ref, gcol_ref, z_ref, onorm_ref, sin_ref, o_ref, sout_ref, *, tb, n_heads, chained):
    dk = dv = LANES
    qk_w = n_heads * dk
    n_pairs = tb // PAIR
    if chained:
        @pl.when(pl.program_id(1) == 0)
        def _():
            sout_ref[...] = sin_ref[...]

    row = lax.broadcasted_iota(jnp.int32, (PAIR, PAIR), 0)
    col = lax.broadcasted_iota(jnp.int32, (PAIR, PAIR), 1)
    same = (row ^ col) < CHUNK
    causal = same & (col <= row)
    strict = same & (col < row)
    in_chunk = [(col >= c * CHUNK) & (col < (c + 1) * CHUNK) for c in range(2)]

    tiles = [(p, h) for p in range(n_pairs) for h in range(n_heads)]
    q, k, kb, vb, gc, a, qk = {}, {}, {}, {}, {}, {}, {}
    for p, h in tiles:
        rows = slice(p * PAIR, (p + 1) * PAIR)
        q[p, h] = qkv_ref[0, rows, h * dk:(h + 1) * dk] * (dk ** -0.5)
        k[p, h] = qkv_ref[0, rows, qk_w + h * dk:qk_w + (h + 1) * dk]
        gc[p, h] = jnp.broadcast_to(gcol_ref[0, rows, h:h + 1], (PAIR, PAIR))
        bt = jnp.broadcast_to(gcol_ref[0, rows, n_heads + h:n_heads + h + 1], (PAIR, PAIR))
        kb[p, h] = k[p, h] * bt
        vb[p, h] = qkv_ref[0, rows, 2 * qk_w + h * dv:2 * qk_w + (h + 1) * dv] * bt
    for t in tiles:
        diff = gc[t] - gc[t].T
        decay = jnp.where(causal, jnp.exp(jnp.where(causal, diff, 0.0)), 0.0)
        a[t] = jnp.where(strict, _mm_nt(kb[t], k[t]) * decay, 0.0)
        qk[t] = jnp.where(causal, _mm_nt(q[t], k[t]) * decay, 0.0)
    inv = dict(zip(tiles, _unit_lower_inverses([a[t] for t in tiles], row, col)))
    uw, kg_t, gl = {}, {}, {}
    for t in tiles:
        eg = jnp.exp(gc[t])
        uw[t] = _mm(inv[t], jnp.concatenate([vb[t], kb[t] * eg], axis=1))
        gl_rows = [jnp.broadcast_to(gc[t][(c + 1) * CHUNK - 1:(c + 1) * CHUNK, :], (PAIR, PAIR)) for c in range(2)]
        gl[t] = [jnp.exp(g) for g in gl_rows]
        gcl = jnp.where(row < CHUNK, gl_rows[0], gl_rows[1])
        kg_t[t] = (k[t] * jnp.exp(gcl - gc[t])).T
        q[t] = q[t] * eg
    qk_uw = {t: _mm(qk[t], uw[t]) for t in tiles}
    kg_uw = {(t, c): _mm(jnp.where(in_chunk[c], kg_t[t], 0.0), uw[t])
             for t in tiles for c in range(2)}

    state = [sout_ref[0, h] for h in range(n_heads)] if chained else None
    for p in range(n_pairs):
        rows = slice(p * PAIR, (p + 1) * PAIR)
        o_parts = {h: [] for h in range(n_heads)}
        for c in range(2):
            r = slice(c * CHUNK, (c + 1) * CHUNK)
            for h in range(n_heads):
                t = (p, h)
                s_old = state[h] if chained else sin_ref[c, h]
                q_eff = q[t][r] - qk_uw[t][r, dv:]
                xs = _mm(jnp.concatenate([kg_uw[t, c][:, dv:], q_eff], axis=0), s_old)
                o_parts[h].append(qk_uw[t][r, :dv] + xs[dk:])
                s_new = s_old * gl[t][c] + (kg_uw[t, c][:, :dv] - xs[:dk])
                if chained:
                    state[h] = s_new
                else:
                    sout_ref[c, h] = s_new
        for h in range(n_heads):
            o = jnp.concatenate(o_parts[h], axis=0)
            zz = z_ref[0, rows, h * dv:(h + 1) * dv].astype(F32)
            y = (o * _rms_scale(o) * onorm_ref[...]) * _silu(zz)
            o_ref[0, rows, h * dv:(h + 1) * dv] = y.astype(o_ref.dtype)
    if chained:
        for h in range(n_heads):
            sout_ref[0, h] = state[h]


def _delta(qkv, gcol, z, onorm, s0, *, n_heads):
    n_seq, L, qkv_w = qkv.shape
    v_w = z.shape[-1]
    assert qkv_w == 3 * n_heads * LANES and v_w == n_heads * LANES
    if L % PAIR == 0:
        chained, n_grp, Lg, per_grp = True, n_seq, L, 1
    else:
        assert L == CHUNK and n_seq % 2 == 0
        chained, n_grp, Lg, per_grp = False, n_seq // 2, PAIR, 2
        qkv, gcol, z = (t.reshape(n_grp, PAIR, t.shape[-1]) for t in (qkv, gcol, z))
    tb = min(DELTA_ROWS, Lg)
    assert Lg % tb == 0
    kern = functools.partial(_delta_kernel, tb=tb, n_heads=n_heads, chained=chained)
    blk = lambda w: pl.BlockSpec((1, tb, w), lambda s, j: (s, j, 0))
    st = pl.BlockSpec((per_grp, n_heads, LANES, LANES), lambda s, j: (s, 0, 0, 0))
    o, s_new = pl.pallas_call(
        kern,
        grid=(n_grp, Lg // tb),
        in_specs=[blk(qkv_w), blk(LANES), blk(v_w), _const_spec((1, LANES)), st],
        out_specs=[blk(v_w), st],
        out_shape=[jax.ShapeDtypeStruct((n_grp, Lg, v_w), BF16),
                   jax.ShapeDtypeStruct(s0.shape, F32)],
        compiler_params=pltpu.CompilerParams(dimension_semantics=("arbitrary", "arbitrary"),
                                             vmem_limit_bytes=VMEM_LIMIT),
        name="delta_rule",
    )(qkv, gcol, z, onorm, s0)
    return o.reshape(n_seq, L, v_w), s_new


def _out_mlp_kernel(*refs, n_a, steps, tf):
    n_s = len(steps)
    per = 1 + n_a
    w_ref, g_ref, wup_ref, wdn_ref = refs[n_s * per:n_s * per + 4]
    o_refs = refs[n_s * per + 4:]

    def tile(x_ref, a_refs, o_ref):
        a = a_refs[0][...] if n_a == 1 else jnp.concatenate([a_ref[...] for a_ref in a_refs], axis=-1)
        x1 = x_ref[...] + jnp.dot(a, w_ref[...], preferred_element_type=F32)
        h = (x1 * _rms_scale(x1) * g_ref[...]).astype(BF16)
        acc = x1
        for f in range(wup_ref.shape[-1] // tf):
            r = jnp.maximum(jnp.dot(h, wup_ref[0, :, f * tf:(f + 1) * tf], preferred_element_type=F32), 0.0)
            acc = acc + jnp.dot((r * r).astype(BF16), wdn_ref[0, f * tf:(f + 1) * tf, :],
                                preferred_element_type=F32)
        o_ref[...] = acc

    i = pl.program_id(0)
    first = 0
    for s in range(n_s):
        run = functools.partial(tile, refs[s * per], refs[s * per + 1:(s + 1) * per], o_refs[s])
        if n_s == 1:
            run()
        else:
            pl.when((i >= first) & (i < first + steps[s]))(run)
        first += steps[s]


def _out_mlp(streams, w_out, gain, w_up, w_down, layer):
    D = streams[0][0].shape[-1]
    flat = [(x.reshape(-1, D), [a.reshape(-1, a.shape[-1]) for a in a_list]) for x, a_list in streams]
    tm = min([MLP_ROWS] + [x.shape[0] for x, _ in flat])
    assert all(x.shape[0] % tm == 0 for x, _ in flat)
    steps = [x.shape[0] // tm for x, _ in flat]
    n_a = len(flat[0][1])
    F = w_up.shape[-1]
    tf = min(MLP_FF, F)
    assert F % tf == 0
    kern = functools.partial(_out_mlp_kernel, n_a=n_a, steps=tuple(steps), tf=tf)
    layer_spec = lambda shape: pl.BlockSpec((1,) + shape[1:], lambda i: (layer, 0, 0),
                                            pipeline_mode=pl.Buffered(1))

    def rows(w, first, n):
        return pl.BlockSpec((tm, w), lambda i: (jnp.minimum(jnp.maximum(i - first, 0), n - 1), 0))

    in_specs, args, out_specs, first = [], [], [], 0
    for (x, a_list), n in zip(flat, steps):
        in_specs += [rows(D, first, n)] + [rows(a.shape[1], first, n) for a in a_list]
        args += [x] + a_list
        out_specs.append(rows(D, first, n))
        first += n
    outs = pl.pallas_call(
        kern,
        grid=(sum(steps),),
        in_specs=in_specs + [_const_spec(w_out.shape), _const_spec((1, D)), layer_spec(w_up.shape),
                             layer_spec(w_down.shape)],
        out_specs=out_specs,
        out_shape=[jax.ShapeDtypeStruct(x.shape, F32) for x, _ in flat],
        compiler_params=pltpu.CompilerParams(dimension_semantics=("arbitrary",), vmem_limit_bytes=VMEM_LIMIT),
        name="out_mlp",
    )(*args, w_out, gain, w_up, w_down)
    return [o.reshape(x.shape) for o, (x, _) in zip(outs, streams)]


def _odd_pre_kernel(x_ref, g_ref, w_ref, qn_ref, kn_ref, q_ref, k_ref, v_ref, *mxu_copies, n_heads):
    dh = LANES
    hd = n_heads * dh
    ns, tl, d_model = x_ref.shape
    x = x_ref[...].reshape(ns * tl, d_model)
    h = (x * _rms_scale(x) * g_ref[...]).astype(BF16)
    q = jnp.dot(h, w_ref[:, 0:hd], preferred_element_type=F32)
    k = jnp.dot(h, w_ref[:, hd:2 * hd], preferred_element_type=F32)
    v = jnp.dot(h, w_ref[:, 2 * hd:3 * hd], preferred_element_type=F32)
    per_seq = lambda t: t.reshape(ns, tl, dh)
    for hh in range(n_heads):
        cols = slice(hh * dh, (hh + 1) * dh)
        qh = q[:, cols]
        kh = k[:, cols]
        q_ref[:, hh] = per_seq((qh * _rms_scale(qh) * qn_ref[...] * (dh ** -0.5 * LOG2E)).astype(q_ref.dtype))
        kh = kh * _rms_scale(kh) * kn_ref[...]
        k_ref[:, hh] = per_seq(kh)
        v_ref[:, hh] = per_seq(v[:, cols])
        if mxu_copies:
            mxu_copies[0][:, hh] = per_seq(kh.astype(BF16))
            mxu_copies[1][:, hh] = per_seq(v[:, cols].astype(BF16))


def _odd_pre(x, gain, w_qkv, qn, kn, *, n_heads, mxu_copies):
    B, L, D = x.shape
    tl = min(PROJ_ROWS, L)
    assert L % tl == 0 and w_qkv.shape[1] == 3 * n_heads * LANES
    kern = functools.partial(_odd_pre_kernel, n_heads=n_heads)
    ns = max(d for d in range(1, max(1, PROJ_ROWS // tl) + 1) if B % d == 0)
    head_major = pl.BlockSpec((ns, n_heads, tl, LANES), lambda b, j: (b, 0, j, 0))
    hm_shape = (B, n_heads, L, LANES)
    n_copies = 2 if mxu_copies else 0
    return pl.pallas_call(
        kern,
        grid=(B // ns, L // tl),
        in_specs=[pl.BlockSpec((ns, tl, D), lambda b, j: (b, j, 0)), _const_spec((1, D)),
                  _const_spec(w_qkv.shape), _const_spec((1, LANES)), _const_spec((1, LANES))],
        out_specs=[head_major] * (3 + n_copies),
        out_shape=([jax.ShapeDtypeStruct(hm_shape, BF16), jax.ShapeDtypeStruct(hm_shape, F32),
                    jax.ShapeDtypeStruct(hm_shape, F32)] + [jax.ShapeDtypeStruct(hm_shape, BF16)] * n_copies),
        compiler_params=pltpu.CompilerParams(dimension_semantics=("arbitrary", "arbitrary"),
                                             vmem_limit_bytes=VMEM_LIMIT),
        name="odd_pre",
    )(x, gain, w_qkv, qn, kn)


def _suffix_ones(tk):
    j = lax.broadcasted_iota(jnp.int32, (2 * tk, tk), 0) & (tk - 1)
    s = lax.broadcasted_iota(jnp.int32, (2 * tk, tk), 1)
    return (j > s).astype(BF16)


def _sb_logits(q, k_blk, u, below=None):
    zs = [_mm_nt(a, b) for a, b in zip(q, k_blk)] if isinstance(q, (list, tuple)) else [_mm_nt(q, k_blk)]
    z = jnp.concatenate(zs, axis=0) if len(zs) > 1 else zs[0]
    t = jnp.maximum(z, 0.0) + jnp.log(1.0 + jnp.exp2(-jnp.abs(z))) * LOG2E
    if below is not None:
        t = jnp.where(below, t, 0.0)
    t_hi = t.astype(BF16)
    t_lo = (t - t_hi.astype(F32)).astype(BF16)
    later = jnp.dot(jnp.concatenate([t_hi, t_lo], axis=1), u, preferred_element_type=F32)
    return z, t, later


def _sb_weights(z, t, later, off, v_blk, acc):
    wts = jnp.exp2(z - t - (later + off))
    own = later[:, 0:1] + t[:, 0:1]
    if not isinstance(v_blk, (list, tuple)):
        return own, acc + _mm(wts, v_blk)
    rows = z.shape[0] // len(v_blk)
    return own, [a + _mm(wts[n * rows:(n + 1) * rows], v) for n, (a, v) in enumerate(zip(acc, v_blk))]


def _sb_block(q, k_blk, v_blk, u, off, acc, below=None):
    return _sb_weights(*_sb_logits(q, k_blk, u, below), off, v_blk, acc)


def _sb_kernel(*refs, tq, tkp, n_past, n_q, hb, qb, n_prev):
    lazy_past = bool(n_past) and n_q == 1
    if lazy_past:
        q_ref, kn_ref, vn_ref, kl_ref, vl_ref, kp_ref, vp_ref, o_ref, kbuf, vbuf, sem = refs
    elif n_past:
        q_ref, kn_ref, vn_ref, kp_ref, vp_ref, o_ref = refs
    else:
        q_ref, kn_ref, vn_ref, o_ref = refs
    u_self = _suffix_ones(tq)
    u_past = _suffix_ones(tkp) if n_past else None
    row = lax.broadcasted_iota(jnp.int32, (tq, tq), 0)
    col = lax.broadcasted_iota(jnp.int32, (tq, tq), 1)
    below = col < row
    not_below = jnp.where(below, 0.0, MASKED)

    walks = []
    if n_q == 1:
        heads = range(hb)
        qs = [q_ref[0, hh] for hh in heads]
        below_all = jnp.concatenate([below] * hb, axis=0)
        spent, accs = _sb_block(qs, [kn_ref[0, hh] for hh in heads], [vn_ref[0, hh] for hh in heads], u_self,
                                jnp.concatenate([not_below] * hb, axis=0),
                                [jnp.zeros((tq, LANES), F32)] * hb, below_all)
        if lazy_past:
            own, accs = _sb_block(qs, [kl_ref[0, hh] for hh in heads], [vl_ref[0, hh] for hh in heads], u_past,
                                  spent, accs)
            spent = spent + own
        walks = [(hh, 0, qs[hh], spent[hh * tq:(hh + 1) * tq], accs[hh], -1, n_past - 2) for hh in heads]
    chains = [(hh, w) for hh in range(hb) for w in range(qb) if n_q > 1]
    logits = {}
    for c, (hh, w) in enumerate(chains):
        i = pl.program_id(2) * qb + w
        q = q_ref[0, hh, w * tq:(w + 1) * tq, :]
        for d in range(n_prev + 1):
            blk = pl.multiple_of(jnp.maximum(i - d, 0) * tq, tq)
            logits[c, d] = _sb_logits(q, kn_ref[0, hh, pl.ds(blk, tq), :], u_self, below if d == 0 else None)
    for c, (hh, w) in enumerate(chains):
        i = pl.program_id(2) * qb + w
        q = q_ref[0, hh, w * tq:(w + 1) * tq, :]
        start = pl.multiple_of(i * tq, tq)
        spent, acc = _sb_weights(*logits[c, 0], not_below, vn_ref[0, hh, pl.ds(start, tq), :],
                                 jnp.zeros((tq, LANES), F32))
        for d in range(1, n_prev + 1):
            prev = pl.multiple_of(jnp.maximum(i - d, 0) * tq, tq)
            has_prev = jnp.full((tq, 1), i, jnp.int32) >= d
            own, acc = _sb_weights(*logits[c, d], jnp.where(has_prev, spent, MASKED),
                                   vn_ref[0, hh, pl.ds(prev, tq), :], acc)
            spent = spent + jnp.where(has_prev, own, 0.0)
        walks.append((hh, w, q, spent, acc, i - n_prev - 1, n_past - 1))

    def walk(q, k_ref, v_ref, hh, tk, u, first, spent, acc, from_hbm=False):
        def cond(st):
            return (st[0] >= 0) & (jnp.min(st[1]) < -LOG2_W_FLOOR)

        def body(st):
            j, spent, acc = st
            s0 = pl.multiple_of(j * tk, tk)
            if from_hbm:
                head = pl.program_id(1) * hb + hh
                copies = [pltpu.make_async_copy(src.at[pl.program_id(0), head, pl.ds(s0, tk), :], dst, sem.at[n])
                          for n, (src, dst) in enumerate(((k_ref, kbuf), (v_ref, vbuf)))]
                for cp in copies:
                    cp.start()
                for cp in copies:
                    cp.wait()
                k_blk, v_blk = kbuf[...], vbuf[...]
            else:
                k_blk, v_blk = k_ref[0, hh, pl.ds(s0, tk), :], v_ref[0, hh, pl.ds(s0, tk), :]
            own, acc = _sb_block(q, k_blk, v_blk, u, spent, acc)
            return j - 1, spent + own, acc

        _, spent, acc = lax.while_loop(cond, body, (first, spent, acc))
        return spent, acc

    for hh, w, _, _, acc, _, _ in walks:
        o_ref[0, w * tq:(w + 1) * tq, hh * LANES:(hh + 1) * LANES] = acc.astype(o_ref.dtype)
    if n_q == 1 and n_past <= 1:
        return

    least = functools.reduce(jnp.minimum, [wk[3] for wk in walks])

    @pl.when(jnp.min(least) < -LOG2_W_FLOOR)
    def _():
        for hh, w, q, spent, acc, first_self, first_past in walks:
            if n_q > 1:
                spent, acc = walk(q, kn_ref, vn_ref, hh, tq, u_self, first_self, spent, acc)
            if n_past:
                spent, acc = walk(q, kp_ref, vp_ref, hh, tkp, u_past, jnp.int32(first_past), spent, acc,
                                  from_hbm=lazy_past)
            o_ref[0, w * tq:(w + 1) * tq, hh * LANES:(hh + 1) * LANES] = acc.astype(o_ref.dtype)


def _stick_breaking(q, k_new, v_new, k_past, v_past):
    B, H, L, dh = q.shape
    assert dh == LANES
    tq = min(SB_TQ, L)
    assert L % tq == 0
    n_q = L // tq
    assert tq & (tq - 1) == 0
    if n_q == 1:
        hb, qb = H, 1
    else:
        kv_bytes = 2 * 2 * L * LANES * k_new.dtype.itemsize
        hb = max(d for d in (4, 2, 1) if H % d == 0 and d * kv_bytes <= SB_KV_VMEM)
        qb = max(d for d in (SB_CHAINS // hb, 2, 1) if n_q % d == 0)
    n_prev = 0 if n_q == 1 else SB_PREV
    assert H % hb == 0 and n_q % qb == 0
    args = [q, k_new, v_new]
    full = lambda n: pl.BlockSpec((1, hb, n, LANES), lambda b, h, i: (b, h, 0, 0))
    in_specs = [pl.BlockSpec((1, hb, tq * qb, LANES), lambda b, h, i: (b, h, i, 0)), full(L), full(L)]
    n_past, tkp = 0, 0
    scratch = []
    if k_past is not None:
        P = k_past.shape[2]
        tkp = min(PROJ_ROWS, P)
        assert P % tkp == 0 and tkp & (tkp - 1) == 0
        n_past = P // tkp
        if n_q == 1:
            last = pl.BlockSpec((1, hb, tkp, LANES), lambda b, h, i: (b, h, n_past - 1, 0))
            hbm = pl.BlockSpec(memory_space=pl.ANY)
            args += [k_past, v_past, k_past, v_past]
            in_specs += [last, last, hbm, hbm]
            scratch = [pltpu.VMEM((tkp, LANES), k_past.dtype), pltpu.VMEM((tkp, LANES), v_past.dtype),
                       pltpu.SemaphoreType.DMA((2,))]
        else:
            args += [k_past, v_past]
            in_specs += [full(P), full(P)]
    kern = functools.partial(_sb_kernel, tq=tq, tkp=tkp, n_past=n_past, n_q=n_q, hb=hb, qb=qb, n_prev=n_prev)
    return pl.pallas_call(
        kern,
        grid=(B, H // hb, n_q // qb),
        in_specs=in_specs,
        out_specs=pl.BlockSpec((1, tq * qb, hb * LANES), lambda b, h, i: (b, i, h)),
        scratch_shapes=scratch,
        out_shape=jax.ShapeDtypeStruct((B, L, H * LANES), BF16),
        compiler_params=pltpu.CompilerParams(dimension_semantics=("arbitrary", "arbitrary", "arbitrary"),
                                             vmem_limit_bytes=VMEM_LIMIT),
        name="stick_breaking",
    )(*args)


def _stack(parts):
    return parts[0][None] if len(parts) == 1 else jnp.stack(parts)


def _pad_lanes(v):
    return jnp.zeros((1, LANES), F32).at[0, :v.shape[0]].set(v.astype(F32))


def _even_mixers(x, s0, qbuf, dbuf, wts, gain_mix):
    (w_all, conv_w, alog_pad, dtb_pad, onorm, dw_w, dw_b, ln_g, ln_b, n_heads, qkv_w, v_w, c_b) = wts
    qkv, gcol, z, c, qlast, dlast = _even_pre(x, gain_mix, w_all, conv_w, alog_pad, dtb_pad, dw_w, dw_b, ln_g, ln_b,
                                              qbuf, dbuf, n_heads=n_heads, qkv_w=qkv_w, v_w=v_w, c_b=c_b)
    o, s_new = _delta(qkv, gcol, z, onorm, s0, n_heads=n_heads)
    return [o, c], s_new, qlast, dlast


def _odd_mixer(x, k_past, v_past, wts, gain_mix):
    w_qkv, qn, kn, n_heads = wts
    mxu_copies = x.shape[1] > SB_TQ
    q, k, v, *kv_mxu = _odd_pre(x, gain_mix, w_qkv, qn, kn, n_heads=n_heads, mxu_copies=mxu_copies)
    o = _stick_breaking(q, *(kv_mxu or (k, v)), k_past, v_past)
    return [o], k, v


def kernel(x_prompt, x_sample, state_delta, state_qkv_conv, state_dw_conv, cache_k, cache_v, norm_mix, norm_mlp, w_in_e, conv_qkv_e, a_log_e, dt_bias_e, onorm_e, dw_w_e, dw_b_e, ln_g_e, ln_b_e, w_out_e, w_qkv_o, qn_o, kn_o, w_out_o, w_up, w_down):
    depth = norm_mix.shape[0]
    bp = x_prompt.shape[0]
    xp, xs = x_prompt, x_sample
    pd, pq, pw, pk, pv = [], [], [], [], []
    sd, sq, sw, sk, sv = [], [], [], [], []
    row = lambda v: v.astype(F32).reshape(1, -1)
    wu, wd = w_up.astype(BF16), w_down.astype(BF16)
    for i in range(depth):
        gm, gl = row(norm_mix[i]), row(norm_mlp[i])
        if i % 2 == 0:
            e = i // 2
            n_heads = a_log_e.shape[1]
            qkv_w = conv_qkv_e.shape[2]
            c_b = dw_w_e.shape[2]
            v_w = n_heads * onorm_e.shape[1]
            w_in = w_in_e[e]
            ab0 = qkv_w + v_w
            ab = jnp.zeros((w_in.shape[0], LANES), w_in.dtype).at[:, :2 * n_heads].set(w_in[:, ab0:ab0 + 2 * n_heads])
            w_all = jnp.concatenate([w_in[:, :ab0], w_in[:, ab0 + 2 * n_heads:], ab], axis=1).astype(BF16)
            w_out = w_out_e[e].astype(BF16)
            wts = (w_all, conv_qkv_e[e].astype(F32), _pad_lanes(a_log_e[e]), _pad_lanes(dt_bias_e[e]),
                   row(onorm_e[e]), dw_w_e[e].astype(F32), row(dw_b_e[e]), row(ln_g_e[e]), row(ln_b_e[e]),
                   n_heads, qkv_w, v_w, c_b)
            kq, kd = conv_qkv_e.shape[1], dw_w_e.shape[1]
            s0 = jnp.zeros((bp,) + state_delta.shape[2:], F32)
            qb0 = jnp.zeros((bp, kq - 1, qkv_w), F32)
            db0 = jnp.zeros((bp, kd - 1, c_b), F32)
            ap, d1, q1, c1 = _even_mixers(xp, s0, qb0, db0, wts, gm)
            as_, d2, q2, c2 = _even_mixers(xs, state_delta[e].astype(F32), state_qkv_conv[e].astype(F32),
                                           state_dw_conv[e].astype(F32), wts, gm)
            pd.append(d1); pq.append(q1); pw.append(c1)
            sd.append(d2); sq.append(q2); sw.append(c2)
        else:
            o = i // 2
            n_heads = cache_k.shape[2]
            w_out = w_out_o[o].astype(BF16)
            wts = (w_qkv_o[o].astype(BF16), row(qn_o[o]), row(kn_o[o]), n_heads)
            ap, k1, v1 = _odd_mixer(xp, None, None, wts, gm)
            as_, k2, v2 = _odd_mixer(xs, cache_k[o].astype(F32), cache_v[o].astype(F32), wts, gm)
            pk.append(k1); pv.append(v1)
            sk.append(k2); sv.append(v2)
        xp, xs = _out_mlp([(xp, ap), (xs, as_)], w_out, gl, wu, wd, i)
    return (xp, xs,
            *(_stack(t) for t in (pd, pq, pw, pk, pv, sd, sq, sw, sk, sv)))
```

```python
import functools

import jax
import jax.numpy as jnp
from jax import lax
from jax.experimental import pallas as pl
from jax.experimental.pallas import tpu as pltpu

F32 = jnp.float32
BF16 = jnp.bfloat16
EPS = 1e-6
CHUNK = 64
LANES = 128
PAIR = 2 * CHUNK
INV_BASE = 8
QBUF_OFF = 8
DBUF_OFF = 32
DW_ROWS = 32
SUBLANES = 8
PROJ_ROWS = 256
EVEN_SUB = 4
DELTA_ROWS = 512
MLP_ROWS = 512
MLP_FF = 1024
VMEM_LIMIT = 56 * 1024 * 1024
LOG2E = 1.4426950408889634
LOG2_W_FLOOR = -151.0
SB_TQ = 256
SB_PREV = 1
SB_KV_VMEM = 36 * 1024 * 1024
SB_CHAINS = 16
MASKED = 1e30


def _mm(a, b):
    return jnp.dot(a.astype(BF16), b.astype(BF16), preferred_element_type=F32)


def _mm_nt(a, b):
    return lax.dot_general(a.astype(BF16), b.astype(BF16), (((1,), (1,)), ((), ())),
                           preferred_element_type=F32)


def _sigmoid(x):
    return 0.5 * jnp.tanh(0.5 * x) + 0.5


def _silu(x):
    h = 0.5 * x
    return h + h * jnp.tanh(h)


def _softplus(x):
    return jnp.maximum(x, 0.0) + jnp.log1p(jnp.exp(-jnp.abs(x)))


def _rms_scale(x):
    return lax.rsqrt(jnp.mean(x * x, axis=-1, keepdims=True) + EPS)


def _const_spec(shape):
    nd = len(shape)
    return pl.BlockSpec(shape, lambda *_: (0,) * nd, pipeline_mode=pl.Buffered(1))


def _even_pre_kernel(x_ref, g_ref, w_ref, cw_ref, alog_ref, dtb_ref, dww_ref, dwb_ref, lng_ref, lnb_ref,
                     qbuf_ref, dbuf_ref,
                     qkv_ref, gcol_ref, z_ref, c_ref, qlast_ref, dlast_ref,
                     qext, uext, ushift, *, ns, nsub, tl, qkv_w, v_w, c_b, n_heads):
    kq = cw_ref.shape[0] // SUBLANES
    kd = dww_ref.shape[0] // SUBLANES
    glu0 = qkv_w + v_w
    q_hist = slice(QBUF_OFF - (kq - 1), QBUF_OFF)
    d_hist = slice(DBUF_OFF - (kd - 1), DBUF_OFF)
    q_tail = slice(QBUF_OFF + tl - (kq - 1), QBUF_OFF + tl)
    d_tail = slice(DBUF_OFF + tl - (kd - 1), DBUF_OFF + tl)

    @pl.when(pl.program_id(1) == 0)
    def _():
        for sq in range(ns):
            qext[sq * nsub, q_hist, :] = qbuf_ref[sq]
            uext[sq * nsub, 0:d_hist.start, :] = jnp.zeros((d_hist.start, c_b), F32)
            uext[sq * nsub, d_hist, :] = dbuf_ref[sq]

    def project(units):
        m = len(units) * tl
        x = jnp.concatenate([x_ref[u // nsub, (u % nsub) * tl:(u % nsub + 1) * tl, :] for u in units], axis=0)
        h = (x * _rms_scale(x) * g_ref[...]).astype(BF16)
        qkv_pre = jnp.dot(h, w_ref[:, 0:qkv_w], preferred_element_type=F32)
        z = jnp.dot(h, w_ref[:, qkv_w:glu0], preferred_element_type=F32)
        ga = jnp.dot(h, w_ref[:, glu0:glu0 + c_b], preferred_element_type=F32)
        gb = jnp.dot(h, w_ref[:, glu0 + c_b:glu0 + 2 * c_b], preferred_element_type=F32)
        ug = ga * _sigmoid(gb)
        ab = jnp.dot(h, w_ref[:, glu0 + 2 * c_b:glu0 + 2 * c_b + LANES], preferred_element_type=F32)

        lane = lax.broadcasted_iota(jnp.int32, (m, LANES), 1)
        g = -jnp.exp(alog_ref[...]) * _softplus(ab + dtb_ref[...])
        g = jnp.where(lane < n_heads, g, 0.0)
        beta = _sigmoid(ab)
        row = lax.broadcasted_iota(jnp.int32, (m, m), 0)
        col = lax.broadcasted_iota(jnp.int32, (m, m), 1)
        tri = (((row ^ col) < CHUNK) & (col <= row)).astype(BF16)
        g_hi = g.astype(BF16)
        g_r1 = g - g_hi.astype(F32)
        g_mid = g_r1.astype(BF16)
        g_lo = (g_r1 - g_mid.astype(F32)).astype(BF16)
        gcum = (jnp.dot(tri, g_hi, preferred_element_type=F32)
                + jnp.dot(tri, g_mid, preferred_element_type=F32)
                + jnp.dot(tri, g_lo, preferred_element_type=F32))
        gcol = jnp.where(lane < n_heads, gcum, jnp.where(lane < 2 * n_heads, beta, 0.0))

        for n, u in enumerate(units):
            sq, rows, part = u // nsub, slice((u % nsub) * tl, (u % nsub + 1) * tl), slice(n * tl, (n + 1) * tl)
            qext[u, QBUF_OFF:QBUF_OFF + tl, :] = qkv_pre[part]
            uext[u, DBUF_OFF:DBUF_OFF + tl, :] = ug[part]
            z_ref[sq, rows, :] = z[part]
            gcol_ref[sq, rows, :] = gcol[part]
            if u % nsub:
                qext[u, q_hist, :] = qext[u - 1, q_tail, :]
                uext[u, d_hist, :] = uext[u - 1, d_tail, :]

    def convolve(u):
        sq, rows0 = u // nsub, (u % nsub) * tl
        for s in range(qkv_w // LANES):
            cols = slice(s * LANES, (s + 1) * LANES)
            y = None
            for j in range(kq):
                r0 = QBUF_OFF - (kq - 1) + j
                wj = cw_ref[j * SUBLANES:(j + 1) * SUBLANES, cols]
                t = wj[None] * qext[u, r0:r0 + tl, cols].reshape(tl // SUBLANES, SUBLANES, LANES)
                y = t if y is None else y + t
            y = _silu(y.reshape(tl, LANES))
            if s < 2 * n_heads:
                y = y * lax.rsqrt(jnp.sum(y * y, axis=-1, keepdims=True) + EPS)
            qkv_ref[sq, rows0:rows0 + tl, cols] = y

        span = tl + DBUF_OFF - SUBLANES
        for b in range(1, SUBLANES):
            ushift[b - 1, 0:span, :] = uext[u, b:b + span, :]
        for r in range(tl // DW_ROWS):
            acc = None
            for j in range(kd):
                a, b = divmod(DBUF_OFF - (kd - 1) + j, SUBLANES)
                r0 = a * SUBLANES + r * DW_ROWS
                win = uext[u, r0:r0 + DW_ROWS, :] if b == 0 else ushift[b - 1, r0:r0 + DW_ROWS, :]
                wj = dww_ref[j * SUBLANES:(j + 1) * SUBLANES, :]
                t = wj[None] * win.reshape(DW_ROWS // SUBLANES, SUBLANES, c_b)
                acc = t if acc is None else acc + t
            cpre = acc.reshape(DW_ROWS, c_b) + dwb_ref[...]
            mu = jnp.mean(cpre, axis=-1, keepdims=True)
            xc = cpre - mu
            var = jnp.mean(xc * xc, axis=-1, keepdims=True)
            y = xc * lax.rsqrt(var + EPS) * lng_ref[...] + lnb_ref[...]
            c_ref[sq, rows0 + r * DW_ROWS:rows0 + (r + 1) * DW_ROWS, :] = _silu(y).astype(c_ref.dtype)

    units = list(range(ns * nsub))
    groups = [units] if nsub == 1 else [[u] for u in units]
    for gi, group in enumerate(groups):
        project(group)
        if gi:
            for u in groups[gi - 1]:
                convolve(u)
    for u in groups[-1]:
        convolve(u)

    for sq in range(ns):
        last = sq * nsub + nsub - 1
        qlast = qext[last, q_tail, :]
        qlast_ref[sq] = qlast
        qext[sq * nsub, q_hist, :] = qlast
        dlast = uext[last, d_tail, :]
        dlast_ref[sq] = dlast
        uext[sq * nsub, d_hist, :] = dlast


def _even_pre(x, gain, w_all, conv_w, alog_pad, dtb_pad, dw_w, dw_b, ln_g, ln_b, qbuf, dbuf, *, n_heads, qkv_w, v_w, c_b):
    n_seq, L, D = x.shape
    tl = min(PROJ_ROWS, L)
    assert L % tl == 0 and tl % DW_ROWS == 0 and tl % CHUNK == 0
    kq, kd = conv_w.shape[0], dw_w.shape[0]
    assert kq - 1 <= QBUF_OFF and kd - 1 <= DBUF_OFF and tl >= kd - 1
    conv_w = jnp.repeat(conv_w, SUBLANES, axis=0)
    dw_w = jnp.repeat(dw_w, SUBLANES, axis=0)
    nw = w_all.shape[1]
    ns = max(d for d in range(1, max(1, PROJ_ROWS // tl) + 1) if n_seq % d == 0)
    nsub = max(d for d in (EVEN_SUB, 2, 1) if (L // tl) % d == 0) if ns == 1 else 1
    rows = nsub * tl
    kern = functools.partial(_even_pre_kernel, ns=ns, nsub=nsub, tl=tl, qkv_w=qkv_w, v_w=v_w, c_b=c_b,
                             n_heads=n_heads)
    seq_blk = lambda w: pl.BlockSpec((ns, rows, w), lambda s, j: (s, j, 0))
    per_seq = lambda r, w: pl.BlockSpec((ns, r, w), lambda s, j: (s, 0, 0))
    return pl.pallas_call(
        kern,
        grid=(n_seq // ns, L // rows),
        in_specs=[seq_blk(D), _const_spec((1, D)), _const_spec((D, nw)), _const_spec(conv_w.shape),
                  _const_spec((1, LANES)), _const_spec((1, LANES)), _const_spec(dw_w.shape),
                  _const_spec((1, c_b)), _const_spec((1, c_b)), _const_spec((1, c_b)),
                  per_seq(kq - 1, qkv_w), per_seq(kd - 1, c_b)],
        out_specs=[seq_blk(qkv_w), seq_blk(LANES), seq_blk(v_w), seq_blk(c_b),
                   per_seq(kq - 1, qkv_w), per_seq(kd - 1, c_b)],
        out_shape=[jax.ShapeDtypeStruct((n_seq, L, qkv_w), F32),
                   jax.ShapeDtypeStruct((n_seq, L, LANES), F32),
                   jax.ShapeDtypeStruct((n_seq, L, v_w), F32),
                   jax.ShapeDtypeStruct((n_seq, L, c_b), BF16),
                   jax.ShapeDtypeStruct((n_seq, kq - 1, qkv_w), F32),
                   jax.ShapeDtypeStruct((n_seq, kd - 1, c_b), F32)],
        scratch_shapes=[pltpu.VMEM((ns * nsub, QBUF_OFF + tl, qkv_w), F32),
                        pltpu.VMEM((ns * nsub, DBUF_OFF + tl, c_b), F32),
                        pltpu.VMEM((SUBLANES - 1, DBUF_OFF + tl, c_b), F32)],
        compiler_params=pltpu.CompilerParams(dimension_semantics=("arbitrary", "arbitrary"),
                                             vmem_limit_bytes=VMEM_LIMIT),
        name="even_pre",
    )(x, gain, w_all, conv_w, alog_pad, dtb_pad, dw_w, dw_b, ln_g, ln_b, qbuf, dbuf)


def _unit_lower_inverses(a_list, row, col):
    x = row ^ col
    eye = (row == col).astype(F32)
    a8 = [jnp.where(x < INV_BASE, a, 0.0) for a in a_list]
    a8_2 = [_mm(t, t) for t in a8]
    p = [eye - t for t in a8]
    a8_4 = [_mm(t, t) for t in a8_2]
    p = [t + _mm(t, sq) for t, sq in zip(p, a8_2)]
    d = [t + _mm(t, sq) for t, sq in zip(p, a8_4)]
    s = INV_BASE
    while s < CHUNK:
        off = [jnp.where((x >= s) & (x < 2 * s), a, 0.0) for a in a_list]
        od = [_mm(o, t) for o, t in zip(off, d)]
        d = [t - _mm(t, u) for t, u in zip(d, od)]
        s *= 2
    return d


def _delta_kernel(qkv_ref, gcol_ref, z_ref, onorm_ref, sin_ref, o_ref, sout_ref, *, tb, n_heads, chained):
    dk = dv = LANES
    qk_w = n_heads * dk
    n_pairs = tb // PAIR
    if chained:
        @pl.when(pl.program_id(1) == 0)
        def _():
            sout_ref[...] = sin_ref[...]

    row = lax.broadcasted_iota(jnp.int32, (PAIR, PAIR), 0)
    col = lax.broadcasted_iota(jnp.int32, (PAIR, PAIR), 1)
    same = (row ^ col) < CHUNK
    causal = same & (col <= row)
    strict = same & (col < row)
    in_chunk = [(col >= c * CHUNK) & (col < (c + 1) * CHUNK) for c in range(2)]

    tiles = [(p, h) for p in range(n_pairs) for h in range(n_heads)]
    q, k, kb, vb, gc, a, qk = {}, {}, {}, {}, {}, {}, {}
    for p, h in tiles:
        rows = slice(p * PAIR, (p + 1) * PAIR)
        q[p, h] = qkv_ref[0, rows, h * dk:(h + 1) * dk] * (dk ** -0.5)
        k[p, h] = qkv_ref[0, rows, qk_w + h * dk:qk_w + (h + 1) * dk]
        gc[p, h] = jnp.broadcast_to(gcol_ref[0, rows, h:h + 1], (PAIR, PAIR))
        bt = jnp.broadcast_to(gcol_ref[0, rows, n_heads + h:n_heads + h + 1], (PAIR, PAIR))
        kb[p, h] = k[p, h] * bt
        vb[p, h] = qkv_ref[0, rows, 2 * qk_w + h * dv:2 * qk_w + (h + 1) * dv] * bt
    for t in tiles:
        diff = gc[t] - gc[t].T
        decay = jnp.where(causal, jnp.exp(jnp.where(causal, diff, 0.0)), 0.0)
        a[t] = jnp.where(strict, _mm_nt(kb[t], k[t]) * decay, 0.0)
        qk[t] = jnp.where(causal, _mm_nt(q[t], k[t]) * decay, 0.0)
    inv = dict(zip(tiles, _unit_lower_inverses([a[t] for t in tiles], row, col)))
    uw, kg_t, gl = {}, {}, {}
    for t in tiles:
        eg = jnp.exp(gc[t])
        uw[t] = _mm(inv[t], jnp.concatenate([vb[t], kb[t] * eg], axis=1))
        gl_rows = [jnp.broadcast_to(gc[t][(c + 1) * CHUNK - 1:(c + 1) * CHUNK, :], (PAIR, PAIR)) for c in range(2)]
        gl[t] = [jnp.exp(g) for g in gl_rows]
        gcl = jnp.where(row < CHUNK, gl_rows[0], gl_rows[1])
        kg_t[t] = (k[t] * jnp.exp(gcl - gc[t])).T
        q[t] = q[t] * eg
    qk_uw = {t: _mm(qk[t], uw[t]) for t in tiles}
    kg_uw = {(t, c): _mm(jnp.where(in_chunk[c], kg_t[t], 0.0), uw[t])
             for t in tiles for c in range(2)}

    state = [sout_ref[0, h] for h in range(n_heads)] if chained else None
    for p in range(n_pairs):
        rows = slice(p * PAIR, (p + 1) * PAIR)
        o_parts = {h: [] for h in range(n_heads)}
        for c in range(2):
            r = slice(c * CHUNK, (c + 1) * CHUNK)
            for h in range(n_heads):
                t = (p, h)
                s_old = state[h] if chained else sin_ref[c, h]
                q_eff = q[t][r] - qk_uw[t][r, dv:]
                xs = _mm(jnp.concatenate([kg_uw[t, c][:, dv:], q_eff], axis=0), s_old)
                o_parts[h].append(qk_uw[t][r, :dv] + xs[dk:])
                s_new = s_old * gl[t][c] + (kg_uw[t, c][:, :dv] - xs[:dk])
                if chained:
                    state[h] = s_new
                else:
                    sout_ref[c, h] = s_new
        for h in range(n_heads):
            o = jnp.concatenate(o_parts[h], axis=0)
            zz = z_ref[0, rows, h * dv:(h + 1) * dv].astype(F32)
            y = (o * _rms_scale(o) * onorm_ref[...]) * _silu(zz)
            o_ref[0, rows, h * dv:(h + 1) * dv] = y.astype(o_ref.dtype)
    if chained:
        for h in range(n_heads):
            sout_ref[0, h] = state[h]


def _delta(qkv, gcol, z, onorm, s0, *, n_heads):
    n_seq, L, qkv_w = qkv.shape
    v_w = z.shape[-1]
    assert qkv_w == 3 * n_heads * LANES and v_w == n_heads * LANES
    if L % PAIR == 0:
        chained, n_grp, Lg, per_grp = True, n_seq, L, 1
    else:
        assert L == CHUNK and n_seq % 2 == 0
        chained, n_grp, Lg, per_grp = False, n_seq // 2, PAIR, 2
        qkv, gcol, z = (t.reshape(n_grp, PAIR, t.shape[-1]) for t in (qkv, gcol, z))
    tb = min(DELTA_ROWS, Lg)
    assert Lg % tb == 0
    kern = functools.partial(_delta_kernel, tb=tb, n_heads=n_heads, chained=chained)
    blk = lambda w: pl.BlockSpec((1, tb, w), lambda s, j: (s, j, 0))
    st = pl.BlockSpec((per_grp, n_heads, LANES, LANES), lambda s, j: (s, 0, 0, 0))
    o, s_new = pl.pallas_call(
        kern,
        grid=(n_grp, Lg // tb),
        in_specs=[blk(qkv_w), blk(LANES), blk(v_w), _const_spec((1, LANES)), st],
        out_specs=[blk(v_w), st],
        out_shape=[jax.ShapeDtypeStruct((n_grp, Lg, v_w), BF16),
                   jax.ShapeDtypeStruct(s0.shape, F32)],
        compiler_params=pltpu.CompilerParams(dimension_semantics=("arbitrary", "arbitrary"),
                                             vmem_limit_bytes=VMEM_LIMIT),
        name="delta_rule",
    )(qkv, gcol, z, onorm, s0)
    return o.reshape(n_seq, L, v_w), s_new


def _out_mlp_kernel(*refs, n_a, steps, tf):
    n_s = len(steps)
    per = 1 + n_a
    w_ref, g_ref, wup_ref, wdn_ref = refs[n_s * per:n_s * per + 4]
    o_refs = refs[n_s * per + 4:]

    def tile(x_ref, a_refs, o_ref):
        a = a_refs[0][...] if n_a == 1 else jnp.concatenate([a_ref[...] for a_ref in a_refs], axis=-1)
        x1 = x_ref[...] + jnp.dot(a, w_ref[...], preferred_element_type=F32)
        h = (x1 * _rms_scale(x1) * g_ref[...]).astype(BF16)
        acc = x1
        for f in range(wup_ref.shape[-1] // tf):
            r = jnp.maximum(jnp.dot(h, wup_ref[0, :, f * tf:(f + 1) * tf], preferred_element_type=F32), 0.0)
            acc = acc + jnp.dot((r * r).astype(BF16), wdn_ref[0, f * tf:(f + 1) * tf, :],
                                preferred_element_type=F32)
        o_ref[...] = acc

    i = pl.program_id(0)
    first = 0
    for s in range(n_s):
        run = functools.partial(tile, refs[s * per], refs[s * per + 1:(s + 1) * per], o_refs[s])
        if n_s == 1:
            run()
        else:
            pl.when((i >= first) & (i < first + steps[s]))(run)
        first += steps[s]


def _out_mlp(streams, w_out, gain, w_up, w_down, layer):
    D = streams[0][0].shape[-1]
    flat = [(x.reshape(-1, D), [a.reshape(-1, a.shape[-1]) for a in a_list]) for x, a_list in streams]
    tm = min([MLP_ROWS] + [x.shape[0] for x, _ in flat])
    assert all(x.shape[0] % tm == 0 for x, _ in flat)
    steps = [x.shape[0] // tm for x, _ in flat]
    n_a = len(flat[0][1])
    F = w_up.shape[-1]
    tf = min(MLP_FF, F)
    assert F % tf == 0
    kern = functools.partial(_out_mlp_kernel, n_a=n_a, steps=tuple(steps), tf=tf)
    layer_spec = lambda shape: pl.BlockSpec((1,) + shape[1:], lambda i: (layer, 0, 0),
                                            pipeline_mode=pl.Buffered(1))

    def rows(w, first, n):
        return pl.BlockSpec((tm, w), lambda i: (jnp.minimum(jnp.maximum(i - first, 0), n - 1), 0))

    in_specs, args, out_specs, first = [], [], [], 0
    for (x, a_list), n in zip(flat, steps):
        in_specs += [rows(D, first, n)] + [rows(a.shape[1], first, n) for a in a_list]
        args += [x] + a_list
        out_specs.append(rows(D, first, n))
        first += n
    outs = pl.pallas_call(
        kern,
        grid=(sum(steps),),
        in_specs=in_specs + [_const_spec(w_out.shape), _const_spec((1, D)), layer_spec(w_up.shape),
                             layer_spec(w_down.shape)],
        out_specs=out_specs,
        out_shape=[jax.ShapeDtypeStruct(x.shape, F32) for x, _ in flat],
        compiler_params=pltpu.CompilerParams(dimension_semantics=("arbitrary",), vmem_limit_bytes=VMEM_LIMIT),
        name="out_mlp",
    )(*args, w_out, gain, w_up, w_down)
    return [o.reshape(x.shape) for o, (x, _) in zip(outs, streams)]


def _odd_pre_kernel(x_ref, g_ref, w_ref, qn_ref, kn_ref, q_ref, k_ref, v_ref, *mxu_copies, n_heads):
    dh = LANES
    hd = n_heads * dh
    ns, tl, d_model = x_ref.shape
    x = x_ref[...].reshape(ns * tl, d_model)
    h = (x * _rms_scale(x) * g_ref[...]).astype(BF16)
    q = jnp.dot(h, w_ref[:, 0:hd], preferred_element_type=F32)
    k = jnp.dot(h, w_ref[:, hd:2 * hd], preferred_element_type=F32)
    v = jnp.dot(h, w_ref[:, 2 * hd:3 * hd], preferred_element_type=F32)
    per_seq = lambda t: t.reshape(ns, tl, dh)
    for hh in range(n_heads):
        cols = slice(hh * dh, (hh + 1) * dh)
        qh = q[:, cols]
        kh = k[:, cols]
        q_ref[:, hh] = per_seq((qh * _rms_scale(qh) * qn_ref[...] * (dh ** -0.5 * LOG2E)).astype(q_ref.dtype))
        kh = kh * _rms_scale(kh) * kn_ref[...]
        k_ref[:, hh] = per_seq(kh)
        v_ref[:, hh] = per_seq(v[:, cols])
        if mxu_copies:
            mxu_copies[0][:, hh] = per_seq(kh.astype(BF16))
            mxu_copies[1][:, hh] = per_seq(v[:, cols].astype(BF16))


def _odd_pre(x, gain, w_qkv, qn, kn, *, n_heads, mxu_copies):
    B, L, D = x.shape
    tl = min(PROJ_ROWS, L)
    assert L % tl == 0 and w_qkv.shape[1] == 3 * n_heads * LANES
    kern = functools.partial(_odd_pre_kernel, n_heads=n_heads)
    ns = max(d for d in range(1, max(1, PROJ_ROWS // tl) + 1) if B % d == 0)
    head_major = pl.BlockSpec((ns, n_heads, tl, LANES), lambda b, j: (b, 0, j, 0))
    hm_shape = (B, n_heads, L, LANES)
    n_copies = 2 if mxu_copies else 0
    return pl.pallas_call(
        kern,
        grid=(B // ns, L // tl),
        in_specs=[pl.BlockSpec((ns, tl, D), lambda b, j: (b, j, 0)), _const_spec((1, D)),
                  _const_spec(w_qkv.shape), _const_spec((1, LANES)), _const_spec((1, LANES))],
        out_specs=[head_major] * (3 + n_copies),
        out_shape=([jax.ShapeDtypeStruct(hm_shape, BF16), jax.ShapeDtypeStruct(hm_shape, F32),
                    jax.ShapeDtypeStruct(hm_shape, F32)] + [jax.ShapeDtypeStruct(hm_shape, BF16)] * n_copies),
        compiler_params=pltpu.CompilerParams(dimension_semantics=("arbitrary", "arbitrary"),
                                             vmem_limit_bytes=VMEM_LIMIT),
        name="odd_pre",
    )(x, gain, w_qkv, qn, kn)


def _suffix_ones(tk):
    j = lax.broadcasted_iota(jnp.int32, (2 * tk, tk), 0) & (tk - 1)
    s = lax.broadcasted_iota(jnp.int32, (2 * tk, tk), 1)
    return (j > s).astype(BF16)


def _sb_logits(q, k_blk, u, below=None):
    zs = [_mm_nt(a, b) for a, b in zip(q, k_blk)] if isinstance(q, (list, tuple)) else [_mm_nt(q, k_blk)]
    z = jnp.concatenate(zs, axis=0) if len(zs) > 1 else zs[0]
    t = jnp.maximum(z, 0.0) + jnp.log(1.0 + jnp.exp2(-jnp.abs(z))) * LOG2E
    if below is not None:
        t = jnp.where(below, t, 0.0)
    t_hi = t.astype(BF16)
    t_lo = (t - t_hi.astype(F32)).astype(BF16)
    later = jnp.dot(jnp.concatenate([t_hi, t_lo], axis=1), u, preferred_element_type=F32)
    return z, t, later


def _sb_weights(z, t, later, off, v_blk, acc):
    wts = jnp.exp2(z - t - (later + off))
    own = later[:, 0:1] + t[:, 0:1]
    if not isinstance(v_blk, (list, tuple)):
        return own, acc + _mm(wts, v_blk)
    rows = z.shape[0] // len(v_blk)
    return own, [a + _mm(wts[n * rows:(n + 1) * rows], v) for n, (a, v) in enumerate(zip(acc, v_blk))]


def _sb_block(q, k_blk, v_blk, u, off, acc, below=None):
    return _sb_weights(*_sb_logits(q, k_blk, u, below), off, v_blk, acc)


def _sb_kernel(*refs, tq, tkp, n_past, n_q, hb, qb, n_prev):
    lazy_past = bool(n_past) and n_q == 1
    if lazy_past:
        q_ref, kn_ref, vn_ref, kl_ref, vl_ref, kp_ref, vp_ref, o_ref, kbuf, vbuf, sem = refs
    elif n_past:
        q_ref, kn_ref, vn_ref, kp_ref, vp_ref, o_ref = refs
    else:
        q_ref, kn_ref, vn_ref, o_ref = refs
    u_self = _suffix_ones(tq)
    u_past = _suffix_ones(tkp) if n_past else None
    row = lax.broadcasted_iota(jnp.int32, (tq, tq), 0)
    col = lax.broadcasted_iota(jnp.int32, (tq, tq), 1)
    below = col < row
    not_below = jnp.where(below, 0.0, MASKED)

    walks = []
    if n_q == 1:
        heads = range(hb)
        qs = [q_ref[0, hh] for hh in heads]
        below_all = jnp.concatenate([below] * hb, axis=0)
        spent, accs = _sb_block(qs, [kn_ref[0, hh] for hh in heads], [vn_ref[0, hh] for hh in heads], u_self,
                                jnp.concatenate([not_below] * hb, axis=0),
                                [jnp.zeros((tq, LANES), F32)] * hb, below_all)
        if lazy_past:
            own, accs = _sb_block(qs, [kl_ref[0, hh] for hh in heads], [vl_ref[0, hh] for hh in heads], u_past,
                                  spent, accs)
            spent = spent + own
        walks = [(hh, 0, qs[hh], spent[hh * tq:(hh + 1) * tq], accs[hh], -1, n_past - 2) for hh in heads]
    chains = [(hh, w) for hh in range(hb) for w in range(qb) if n_q > 1]
    logits = {}
    for c, (hh, w) in enumerate(chains):
        i = pl.program_id(2) * qb + w
        q = q_ref[0, hh, w * tq:(w + 1) * tq, :]
        for d in range(n_prev + 1):
            blk = pl.multiple_of(jnp.maximum(i - d, 0) * tq, tq)
            logits[c, d] = _sb_logits(q, kn_ref[0, hh, pl.ds(blk, tq), :], u_self, below if d == 0 else None)
    for c, (hh, w) in enumerate(chains):
        i = pl.program_id(2) * qb + w
        q = q_ref[0, hh, w * tq:(w + 1) * tq, :]
        start = pl.multiple_of(i * tq, tq)
        spent, acc = _sb_weights(*logits[c, 0], not_below, vn_ref[0, hh, pl.ds(start, tq), :],
                                 jnp.zeros((tq, LANES), F32))
        for d in range(1, n_prev + 1):
            prev = pl.multiple_of(jnp.maximum(i - d, 0) * tq, tq)
            has_prev = jnp.full((tq, 1), i, jnp.int32) >= d
            own, acc = _sb_weights(*logits[c, d], jnp.where(has_prev, spent, MASKED),
                                   vn_ref[0, hh, pl.ds(prev, tq), :], acc)
            spent = spent + jnp.where(has_prev, own, 0.0)
        walks.append((hh, w, q, spent, acc, i - n_prev - 1, n_past - 1))

    def walk(q, k_ref, v_ref, hh, tk, u, first, spent, acc, from_hbm=False):
        def cond(st):
            return (st[0] >= 0) & (jnp.min(st[1]) < -LOG2_W_FLOOR)

        def body(st):
            j, spent, acc = st
            s0 = pl.multiple_of(j * tk, tk)
            if from_hbm:
                head = pl.program_id(1) * hb + hh
                copies = [pltpu.make_async_copy(src.at[pl.program_id(0), head, pl.ds(s0, tk), :], dst, sem.at[n])
                          for n, (src, dst) in enumerate(((k_ref, kbuf), (v_ref, vbuf)))]
                for cp in copies:
                    cp.start()
                for cp in copies:
                    cp.wait()
                k_blk, v_blk = kbuf[...], vbuf[...]
            else:
                k_blk, v_blk = k_ref[0, hh, pl.ds(s0, tk), :], v_ref[0, hh, pl.ds(s0, tk), :]
            own, acc = _sb_block(q, k_blk, v_blk, u, spent, acc)
            return j - 1, spent + own, acc

        _, spent, acc = lax.while_loop(cond, body, (first, spent, acc))
        return spent, acc

    for hh, w, _, _, acc, _, _ in walks:
        o_ref[0, w * tq:(w + 1) * tq, hh * LANES:(hh + 1) * LANES] = acc.astype(o_ref.dtype)
    if n_q == 1 and n_past <= 1:
        return

    least = functools.reduce(jnp.minimum, [wk[3] for wk in walks])

    @pl.when(jnp.min(least) < -LOG2_W_FLOOR)
    def _():
        for hh, w, q, spent, acc, first_self, first_past in walks:
            if n_q > 1:
                spent, acc = walk(q, kn_ref, vn_ref, hh, tq, u_self, first_self, spent, acc)
            if n_past:
                spent, acc = walk(q, kp_ref, vp_ref, hh, tkp, u_past, jnp.int32(first_past), spent, acc,
                                  from_hbm=lazy_past)
            o_ref[0, w * tq:(w + 1) * tq, hh * LANES:(hh + 1) * LANES] = acc.astype(o_ref.dtype)


def _stick_breaking(q, k_new, v_new, k_past, v_past):
    B, H, L, dh = q.shape
    assert dh == LANES
    tq = min(SB_TQ, L)
    assert L % tq == 0
    n_q = L // tq
    assert tq & (tq - 1) == 0
    if n_q == 1:
        hb, qb = H, 1
    else:
        kv_bytes = 2 * 2 * L * LANES * k_new.dtype.itemsize
        hb = max(d for d in (4, 2, 1) if H % d == 0 and d * kv_bytes <= SB_KV_VMEM)
        qb = max(d for d in (SB_CHAINS // hb, 2, 1) if n_q % d == 0)
    n_prev = 0 if n_q == 1 else SB_PREV
    assert H % hb == 0 and n_q % qb == 0
    args = [q, k_new, v_new]
    full = lambda n: pl.BlockSpec((1, hb, n, LANES), lambda b, h, i: (b, h, 0, 0))
    in_specs = [pl.BlockSpec((1, hb, tq * qb, LANES), lambda b, h, i: (b, h, i, 0)), full(L), full(L)]
    n_past, tkp = 0, 0
    scratch = []
    if k_past is not None:
        P = k_past.shape[2]
        tkp = min(PROJ_ROWS, P)
        assert P % tkp == 0 and tkp & (tkp - 1) == 0
        n_past = P // tkp
        if n_q == 1:
            last = pl.BlockSpec((1, hb, tkp, LANES), lambda b, h, i: (b, h, n_past - 1, 0))
            hbm = pl.BlockSpec(memory_space=pl.ANY)
            args += [k_past, v_past, k_past, v_past]
            in_specs += [last, last, hbm, hbm]
            scratch = [pltpu.VMEM((tkp, LANES), k_past.dtype), pltpu.VMEM((tkp, LANES), v_past.dtype),
                       pltpu.SemaphoreType.DMA((2,))]
        else:
            args += [k_past, v_past]
            in_specs += [full(P), full(P)]
    kern = functools.partial(_sb_kernel, tq=tq, tkp=tkp, n_past=n_past, n_q=n_q, hb=hb, qb=qb, n_prev=n_prev)
    return pl.pallas_call(
        kern,
        grid=(B, H // hb, n_q // qb),
        in_specs=in_specs,
        out_specs=pl.BlockSpec((1, tq * qb, hb * LANES), lambda b, h, i: (b, i, h)),
        scratch_shapes=scratch,
        out_shape=jax.ShapeDtypeStruct((B, L, H * LANES), BF16),
        compiler_params=pltpu.CompilerParams(dimension_semantics=("arbitrary", "arbitrary", "arbitrary"),
                                             vmem_limit_bytes=VMEM_LIMIT),
        name="stick_breaking",
    )(*args)


def _stack(parts):
    return parts[0][None] if len(parts) == 1 else jnp.stack(parts)


def _pad_lanes(v):
    return jnp.zeros((1, LANES), F32).at[0, :v.shape[0]].set(v.astype(F32))


def _even_mixers(x, s0, qbuf, dbuf, wts, gain_mix):
    (w_all, conv_w, alog_pad, dtb_pad, onorm, dw_w, dw_b, ln_g, ln_b, n_heads, qkv_w, v_w, c_b) = wts
    qkv, gcol, z, c, qlast, dlast = _even_pre(x, gain_mix, w_all, conv_w, alog_pad, dtb_pad, dw_w, dw_b, ln_g, ln_b,
                                              qbuf, dbuf, n_heads=n_heads, qkv_w=qkv_w, v_w=v_w, c_b=c_b)
    o, s_new = _delta(qkv, gcol, z, onorm, s0, n_heads=n_heads)
    return [o, c], s_new, qlast, dlast


def _odd_mixer(x, k_past, v_past, wts, gain_mix):
    w_qkv, qn, kn, n_heads = wts
    mxu_copies = x.shape[1] > SB_TQ
    q, k, v, *kv_mxu = _odd_pre(x, gain_mix, w_qkv, qn, kn, n_heads=n_heads, mxu_copies=mxu_copies)
    o = _stick_breaking(q, *(kv_mxu or (k, v)), k_past, v_past)
    return [o], k, v


def kernel(x_prompt, x_sample, state_delta, state_qkv_conv, state_dw_conv, cache_k, cache_v, norm_mix, norm_mlp, w_in_e, conv_qkv_e, a_log_e, dt_bias_e, onorm_e, dw_w_e, dw_b_e, ln_g_e, ln_b_e, w_out_e, w_qkv_o, qn_o, kn_o, w_out_o, w_up, w_down):
    depth = norm_mix.shape[0]
    bp = x_prompt.shape[0]
    xp, xs = x_prompt, x_sample
    pd, pq, pw, pk, pv = [], [], [], [], []
    sd, sq, sw, sk, sv = [], [], [], [], []
    row = lambda v: v.astype(F32).reshape(1, -1)
    wu, wd = w_up.astype(BF16), w_down.astype(BF16)
    for i in range(depth):
        gm, gl = row(norm_mix[i]), row(norm_mlp[i])
        if i % 2 == 0:
            e = i // 2
            n_heads = a_log_e.shape[1]
            qkv_w = conv_qkv_e.shape[2]
            c_b = dw_w_e.shape[2]
            v_w = n_heads * onorm_e.shape[1]
            w_in = w_in_e[e]
            ab0 = qkv_w + v_w
            ab = jnp.zeros((w_in.shape[0], LANES), w_in.dtype).at[:, :2 * n_heads].set(w_in[:, ab0:ab0 + 2 * n_heads])
            w_all = jnp.concatenate([w_in[:, :ab0], w_in[:, ab0 + 2 * n_heads:], ab], axis=1).astype(BF16)
            w_out = w_out_e[e].astype(BF16)
            wts = (w_all, conv_qkv_e[e].astype(F32), _pad_lanes(a_log_e[e]), _pad_lanes(dt_bias_e[e]),
                   row(onorm_e[e]), dw_w_e[e].astype(F32), row(dw_b_e[e]), row(ln_g_e[e]), row(ln_b_e[e]),
                   n_heads, qkv_w, v_w, c_b)
            kq, kd = conv_qkv_e.shape[1], dw_w_e.shape[1]
            s0 = jnp.zeros((bp,) + state_delta.shape[2:], F32)
            qb0 = jnp.zeros((bp, kq - 1, qkv_w), F32)
            db0 = jnp.zeros((bp, kd - 1, c_b), F32)
            ap, d1, q1, c1 = _even_mixers(xp, s0, qb0, db0, wts, gm)
            as_, d2, q2, c2 = _even_mixers(xs, state_delta[e].astype(F32), state_qkv_conv[e].astype(F32),
                                           state_dw_conv[e].astype(F32), wts, gm)
            pd.append(d1); pq.append(q1); pw.append(c1)
            sd.append(d2); sq.append(q2); sw.append(c2)
        else:
            o = i // 2
            n_heads = cache_k.shape[2]
            w_out = w_out_o[o].astype(BF16)
            wts = (w_qkv_o[o].astype(BF16), row(qn_o[o]), row(kn_o[o]), n_heads)
            ap, k1, v1 = _odd_mixer(xp, None, None, wts, gm)
            as_, k2, v2 = _odd_mixer(xs, cache_k[o].astype(F32), cache_v[o].astype(F32), wts, gm)
            pk.append(k1); pv.append(v1)
            sk.append(k2); sv.append(v2)
        xp, xs = _out_mlp([(xp, ap), (xs, as_)], w_out, gl, wu, wd, i)
    return (xp, xs,
            *(_stack(t) for t in (pd, pq, pw, pk, pv, sd, sq, sw, sk, sv)))
```

```python
import functools

import jax
import jax.numpy as jnp
from jax import lax
from jax.experimental import pallas as pl
from jax.experimental.pallas import tpu as pltpu

F32 = jnp.float32
BF16 = jnp.bfloat16
EPS = 1e-6
CHUNK = 64
LANES = 128
PAIR = 2 * CHUNK
QBUF_OFF = 8
DBUF_OFF = 32
DW_ROWS = 32
SUBLANES = 8
PROJ_ROWS = 256
EVEN_SUB = 4
DELTA_ROWS = 512
MLP_ROWS = 512
MLP_FF = 1024
VMEM_LIMIT = 56 * 1024 * 1024
LOG2E = 1.4426950408889634
LOG2_W_FLOOR = -151.0
SB_TQ = 256
SB_PREV = 1
SB_KV_VMEM = 36 * 1024 * 1024
SB_CHAINS = 16
MASKED = 1e30


def _mm(a, b):
    return jnp.dot(a.astype(BF16), b.astype(BF16), preferred_element_type=F32)


def _mm_nt(a, b):
    return lax.dot_general(a.astype(BF16), b.astype(BF16), (((1,), (1,)), ((), ())),
                           preferred_element_type=F32)


def _sigmoid(x):
    return 0.5 * jnp.tanh(0.5 * x) + 0.5


def _silu(x):
    h = 0.5 * x
    return h + h * jnp.tanh(h)


def _softplus(x):
    return jnp.maximum(x, 0.0) + jnp.log1p(jnp.exp(-jnp.abs(x)))


def _rms_scale(x):
    return lax.rsqrt(jnp.mean(x * x, axis=-1, keepdims=True) + EPS)


def _const_spec(shape):
    nd = len(shape)
    return pl.BlockSpec(shape, lambda *_: (0,) * nd, pipeline_mode=pl.Buffered(1))


def _even_pre_kernel(x_ref, g_ref, w_ref, cw_ref, alog_ref, dtb_ref, dww_ref, dwb_ref, lng_ref, lnb_ref,
                     qbuf_ref, dbuf_ref,
                     qkv_ref, gcol_ref, z_ref, c_ref, qlast_ref, dlast_ref,
                     qext, uext, ushift, *, ns, nsub, tl, qkv_w, v_w, c_b, n_heads):
    kq = cw_ref.shape[0] // SUBLANES
    kd = dww_ref.shape[0] // SUBLANES
    glu0 = qkv_w + v_w
    q_hist = slice(QBUF_OFF - (kq - 1), QBUF_OFF)
    d_hist = slice(DBUF_OFF - (kd - 1), DBUF_OFF)
    q_tail = slice(QBUF_OFF + tl - (kq - 1), QBUF_OFF + tl)
    d_tail = slice(DBUF_OFF + tl - (kd - 1), DBUF_OFF + tl)

    @pl.when(pl.program_id(1) == 0)
    def _():
        for sq in range(ns):
            qext[sq * nsub, q_hist, :] = qbuf_ref[sq]
            uext[sq * nsub, 0:d_hist.start, :] = jnp.zeros((d_hist.start, c_b), F32)
            uext[sq * nsub, d_hist, :] = dbuf_ref[sq]

    def project(units):
        m = len(units) * tl
        x = jnp.concatenate([x_ref[u // nsub, (u % nsub) * tl:(u % nsub + 1) * tl, :] for u in units], axis=0)
        h = (x * _rms_scale(x) * g_ref[...]).astype(BF16)
        qkv_pre = jnp.dot(h, w_ref[:, 0:qkv_w], preferred_element_type=F32)
        z = jnp.dot(h, w_ref[:, qkv_w:glu0], preferred_element_type=F32)
        ga = jnp.dot(h, w_ref[:, glu0:glu0 + c_b], preferred_element_type=F32)
        gb = jnp.dot(h, w_ref[:, glu0 + c_b:glu0 + 2 * c_b], preferred_element_type=F32)
        ug = ga * _sigmoid(gb)
        ab = jnp.dot(h, w_ref[:, glu0 + 2 * c_b:glu0 + 2 * c_b + LANES], preferred_element_type=F32)

        lane = lax.broadcasted_iota(jnp.int32, (m, LANES), 1)
        g = -jnp.exp(alog_ref[...]) * _softplus(ab + dtb_ref[...])
        g = jnp.where(lane < n_heads, g, 0.0)
        beta = _sigmoid(ab)
        row = lax.broadcasted_iota(jnp.int32, (m, m), 0)
        col = lax.broadcasted_iota(jnp.int32, (m, m), 1)
        tri = (((row ^ col) < CHUNK) & (col <= row)).astype(BF16)
        g_hi = g.astype(BF16)
        g_r1 = g - g_hi.astype(F32)
        g_mid = g_r1.astype(BF16)
        g_lo = (g_r1 - g_mid.astype(F32)).astype(BF16)
        gcum = (jnp.dot(tri, g_hi, preferred_element_type=F32)
                + jnp.dot(tri, g_mid, preferred_element_type=F32)
                + jnp.dot(tri, g_lo, preferred_element_type=F32))
        gcol = jnp.where(lane < n_heads, gcum, jnp.where(lane < 2 * n_heads, beta, 0.0))

        for n, u in enumerate(units):
            sq, rows, part = u // nsub, slice((u % nsub) * tl, (u % nsub + 1) * tl), slice(n * tl, (n + 1) * tl)
            qext[u, QBUF_OFF:QBUF_OFF + tl, :] = qkv_pre[part]
            uext[u, DBUF_OFF:DBUF_OFF + tl, :] = ug[part]
            z_ref[sq, rows, :] = z[part]
            gcol_ref[sq, rows, :] = gcol[part]
            if u % nsub:
                qext[u, q_hist, :] = qext[u - 1, q_tail, :]
                uext[u, d_hist, :] = uext[u - 1, d_tail, :]

    def convolve(u):
        sq, rows0 = u // nsub, (u % nsub) * tl
        for s in range(qkv_w // LANES):
            cols = slice(s * LANES, (s + 1) * LANES)
            y = None
            for j in range(kq):
                r0 = QBUF_OFF - (kq - 1) + j
                wj = cw_ref[j * SUBLANES:(j + 1) * SUBLANES, cols]
                t = wj[None] * qext[u, r0:r0 + tl, cols].reshape(tl // SUBLANES, SUBLANES, LANES)
                y = t if y is None else y + t
            y = _silu(y.reshape(tl, LANES))
            if s < 2 * n_heads:
                y = y * lax.rsqrt(jnp.sum(y * y, axis=-1, keepdims=True) + EPS)
            qkv_ref[sq, rows0:rows0 + tl, cols] = y

        span = tl + DBUF_OFF - SUBLANES
        for b in range(1, SUBLANES):
            ushift[b - 1, 0:span, :] = uext[u, b:b + span, :]
        for r in range(tl // DW_ROWS):
            acc = None
            for j in range(kd):
                a, b = divmod(DBUF_OFF - (kd - 1) + j, SUBLANES)
                r0 = a * SUBLANES + r * DW_ROWS
                win = uext[u, r0:r0 + DW_ROWS, :] if b == 0 else ushift[b - 1, r0:r0 + DW_ROWS, :]
                wj = dww_ref[j * SUBLANES:(j + 1) * SUBLANES, :]
                t = wj[None] * win.reshape(DW_ROWS // SUBLANES, SUBLANES, c_b)
                acc = t if acc is None else acc + t
            cpre = acc.reshape(DW_ROWS, c_b) + dwb_ref[...]
            mu = jnp.mean(cpre, axis=-1, keepdims=True)
            xc = cpre - mu
            var = jnp.mean(xc * xc, axis=-1, keepdims=True)
            y = xc * lax.rsqrt(var + EPS) * lng_ref[...] + lnb_ref[...]
            c_ref[sq, rows0 + r * DW_ROWS:rows0 + (r + 1) * DW_ROWS, :] = _silu(y).astype(c_ref.dtype)

    units = list(range(ns * nsub))
    groups = [units] if nsub == 1 else [[u] for u in units]
    for gi, group in enumerate(groups):
        project(group)
        if gi:
            for u in groups[gi - 1]:
                convolve(u)
    for u in groups[-1]:
        convolve(u)

    for sq in range(ns):
        last = sq * nsub + nsub - 1
        qlast = qext[last, q_tail, :]
        qlast_ref[sq] = qlast
        qext[sq * nsub, q_hist, :] = qlast
        dlast = uext[last, d_tail, :]
        dlast_ref[sq] = dlast
        uext[sq * nsub, d_hist, :] = dlast


def _even_pre(x, gain, w_all, conv_w, alog_pad, dtb_pad, dw_w, dw_b, ln_g, ln_b, qbuf, dbuf, *, n_heads, qkv_w, v_w, c_b):
    n_seq, L, D = x.shape
    tl = min(PROJ_ROWS, L)
    assert L % tl == 0 and tl % DW_ROWS == 0 and tl % CHUNK == 0
    kq, kd = conv_w.shape[0], dw_w.shape[0]
    assert kq - 1 <= QBUF_OFF and kd - 1 <= DBUF_OFF and tl >= kd - 1
    conv_w = jnp.repeat(conv_w, SUBLANES, axis=0)
    dw_w = jnp.repeat(dw_w, SUBLANES, axis=0)
    nw = w_all.shape[1]
    ns = max(d for d in range(1, max(1, PROJ_ROWS // tl) + 1) if n_seq % d == 0)
    nsub = max(d for d in (EVEN_SUB, 2, 1) if (L // tl) % d == 0) if ns == 1 else 1
    rows = nsub * tl
    kern = functools.partial(_even_pre_kernel, ns=ns, nsub=nsub, tl=tl, qkv_w=qkv_w, v_w=v_w, c_b=c_b,
                             n_heads=n_heads)
    seq_blk = lambda w: pl.BlockSpec((ns, rows, w), lambda s, j: (s, j, 0))
    per_seq = lambda r, w: pl.BlockSpec((ns, r, w), lambda s, j: (s, 0, 0))
    return pl.pallas_call(
        kern,
        grid=(n_seq // ns, L // rows),
        in_specs=[seq_blk(D), _const_spec((1, D)), _const_spec((D, nw)), _const_spec(conv_w.shape),
                  _const_spec((1, LANES)), _const_spec((1, LANES)), _const_spec(dw_w.shape),
                  _const_spec((1, c_b)), _const_spec((1, c_b)), _const_spec((1, c_b)),
                  per_seq(kq - 1, qkv_w), per_seq(kd - 1, c_b)],
        out_specs=[seq_blk(qkv_w), seq_blk(LANES), seq_blk(v_w), seq_blk(c_b),
                   per_seq(kq - 1, qkv_w), per_seq(kd - 1, c_b)],
        out_shape=[jax.ShapeDtypeStruct((n_seq, L, qkv_w), F32),
                   jax.ShapeDtypeStruct((n_seq, L, LANES), F32),
                   jax.ShapeDtypeStruct((n_seq, L, v_w), F32),
                   jax.ShapeDtypeStruct((n_seq, L, c_b), BF16),
                   jax.ShapeDtypeStruct((n_seq, kq - 1, qkv_w), F32),
                   jax.ShapeDtypeStruct((n_seq, kd - 1, c_b), F32)],
        scratch_shapes=[pltpu.VMEM((ns * nsub, QBUF_OFF + tl, qkv_w), F32),
                        pltpu.VMEM((ns * nsub, DBUF_OFF + tl, c_b), F32),
                        pltpu.VMEM((SUBLANES - 1, DBUF_OFF + tl, c_b), F32)],
        compiler_params=pltpu.CompilerParams(dimension_semantics=("arbitrary", "arbitrary"),
                                             vmem_limit_bytes=VMEM_LIMIT),
        name="even_pre",
    )(x, gain, w_all, conv_w, alog_pad, dtb_pad, dw_w, dw_b, ln_g, ln_b, qbuf, dbuf)


def _unit_lower_inverses(a_list, row, col):
    x = row ^ col
    eye = (row == col).astype(F32)
    a8 = [jnp.where(x < 8, a, 0.0) for a in a_list]
    a8_2 = [_mm(t, t) for t in a8]
    p = [eye - t for t in a8]
    a8_4 = [_mm(t, t) for t in a8_2]
    p = [t + _mm(t, sq) for t, sq in zip(p, a8_2)]
    d = [t + _mm(t, sq) for t, sq in zip(p, a8_4)]
    s = 8
    while s < CHUNK:
        off = [jnp.where((x >= s) & (x < 2 * s), a, 0.0) for a in a_list]
        od = [_mm(o, t) for o, t in zip(off, d)]
        d = [t - _mm(t, u) for t, u in zip(d, od)]
        s *= 2
    return d


def _delta_kernel(qkv_ref, gcol_ref, z_ref, onorm_ref, sin_ref, o_ref, sout_ref, *, tb, n_heads, chained):
    dk = dv = LANES
    qk_w = n_heads * dk
    n_pairs = tb // PAIR
    if chained:
        @pl.when(pl.program_id(1) == 0)
        def _():
            sout_ref[...] = sin_ref[...]

    row = lax.broadcasted_iota(jnp.int32, (PAIR, PAIR), 0)
    col = lax.broadcasted_iota(jnp.int32, (PAIR, PAIR), 1)
    same = (row ^ col) < CHUNK
    causal = same & (col <= row)
    strict = same & (col < row)
    in_chunk = [(col >= c * CHUNK) & (col < (c + 1) * CHUNK) for c in range(2)]

    tiles = [(p, h) for p in range(n_pairs) for h in range(n_heads)]
    q, k, kb, vb, gc, a, qk = {}, {}, {}, {}, {}, {}, {}
    for p, h in tiles:
        rows = slice(p * PAIR, (p + 1) * PAIR)
        q[p, h] = qkv_ref[0, rows, h * dk:(h + 1) * dk] * (dk ** -0.5)
        k[p, h] = qkv_ref[0, rows, qk_w + h * dk:qk_w + (h + 1) * dk]
        gc[p, h] = jnp.broadcast_to(gcol_ref[0, rows, h:h + 1], (PAIR, PAIR))
        bt = jnp.broadcast_to(gcol_ref[0, rows, n_heads + h:n_heads + h + 1], (PAIR, PAIR))
        kb[p, h] = k[p, h] * bt
        vb[p, h] = qkv_ref[0, rows, 2 * qk_w + h * dv:2 * qk_w + (h + 1) * dv] * bt
    for t in tiles:
        diff = gc[t] - gc[t].T
        decay = jnp.where(causal, jnp.exp(jnp.where(causal, diff, 0.0)), 0.0)
        a[t] = jnp.where(strict, _mm_nt(kb[t], k[t]) * decay, 0.0)
        qk[t] = jnp.where(causal, _mm_nt(q[t], k[t]) * decay, 0.0)
    inv = dict(zip(tiles, _unit_lower_inverses([a[t] for t in tiles], row, col)))
    uw, kg_t, gl = {}, {}, {}
    for t in tiles:
        eg = jnp.exp(gc[t])
        uw[t] = _mm(inv[t], jnp.concatenate([vb[t], kb[t] * eg], axis=1))
        gl_rows = [jnp.broadcast_to(gc[t][(c + 1) * CHUNK - 1:(c + 1) * CHUNK, :], (PAIR, PAIR)) for c in range(2)]
        gl[t] = [jnp.exp(g) for g in gl_rows]
        gcl = jnp.where(row < CHUNK, gl_rows[0], gl_rows[1])
        kg_t[t] = (k[t] * jnp.exp(gcl - gc[t])).T
        q[t] = q[t] * eg
    qk_uw = {t: _mm(qk[t], uw[t]) for t in tiles}
    kg_uw = {(t, c): _mm(jnp.where(in_chunk[c], kg_t[t], 0.0), uw[t])
             for t in tiles for c in range(2)}

    state = [sout_ref[0, h] for h in range(n_heads)] if chained else None
    for p in range(n_pairs):
        rows = slice(p * PAIR, (p + 1) * PAIR)
        o_parts = {h: [] for h in range(n_heads)}
        for c in range(2):
            r = slice(c * CHUNK, (c + 1) * CHUNK)
            for h in range(n_heads):
                t = (p, h)
                s_old = state[h] if chained else sin_ref[2 * p + c, h]
                q_eff = q[t][r] - qk_uw[t][r, dv:]
                xs = _mm(jnp.concatenate([kg_uw[t, c][:, dv:], q_eff], axis=0), s_old)
                o_parts[h].append(qk_uw[t][r, :dv] + xs[dk:])
                s_new = s_old * gl[t][c] + (kg_uw[t, c][:, :dv] - xs[:dk])
                if chained:
                    state[h] = s_new
                else:
                    sout_ref[2 * p + c, h] = s_new
        for h in range(n_heads):
            o = jnp.concatenate(o_parts[h], axis=0)
            zz = z_ref[0, rows, h * dv:(h + 1) * dv].astype(F32)
            y = (o * _rms_scale(o) * onorm_ref[...]) * _silu(zz)
            o_ref[0, rows, h * dv:(h + 1) * dv] = y.astype(o_ref.dtype)
    if chained:
        for h in range(n_heads):
            sout_ref[0, h] = state[h]


def _delta(qkv, gcol, z, onorm, s0, *, n_heads):
    n_seq, L, qkv_w = qkv.shape
    v_w = z.shape[-1]
    assert qkv_w == 3 * n_heads * LANES and v_w == n_heads * LANES
    if L % PAIR == 0:
        chained, n_grp, Lg, per_grp = True, n_seq, L, 1
    else:
        assert L == CHUNK and n_seq % 2 == 0
        per_grp = max(d for d in range(2, DELTA_ROWS // CHUNK + 1, 2) if n_seq % d == 0)
        chained, n_grp, Lg = False, n_seq // per_grp, per_grp * CHUNK
        qkv, gcol, z = (t.reshape(n_grp, Lg, t.shape[-1]) for t in (qkv, gcol, z))
    tb = min(DELTA_ROWS, Lg)
    assert Lg % tb == 0
    kern = functools.partial(_delta_kernel, tb=tb, n_heads=n_heads, chained=chained)
    blk = lambda w: pl.BlockSpec((1, tb, w), lambda s, j: (s, j, 0))
    st = pl.BlockSpec((per_grp, n_heads, LANES, LANES), lambda s, j: (s, 0, 0, 0))
    o, s_new = pl.pallas_call(
        kern,
        grid=(n_grp, Lg // tb),
        in_specs=[blk(qkv_w), blk(LANES), blk(v_w), _const_spec((1, LANES)), st],
        out_specs=[blk(v_w), st],
        out_shape=[jax.ShapeDtypeStruct((n_grp, Lg, v_w), BF16),
                   jax.ShapeDtypeStruct(s0.shape, F32)],
        compiler_params=pltpu.CompilerParams(dimension_semantics=("arbitrary", "arbitrary"),
                                             vmem_limit_bytes=VMEM_LIMIT),
        name="delta_rule",
    )(qkv, gcol, z, onorm, s0)
    return o.reshape(n_seq, L, v_w), s_new


def _out_mlp_kernel(*refs, n_a, steps, tf):
    n_s = len(steps)
    per = 1 + n_a
    w_ref, g_ref, wup_ref, wdn_ref = refs[n_s * per:n_s * per + 4]
    o_refs = refs[n_s * per + 4:]

    def tile(x_ref, a_refs, o_ref):
        a = a_refs[0][...] if n_a == 1 else jnp.concatenate([a_ref[...] for a_ref in a_refs], axis=-1)
        x1 = x_ref[...] + jnp.dot(a, w_ref[...], preferred_element_type=F32)
        h = (x1 * _rms_scale(x1) * g_ref[...]).astype(BF16)
        acc = x1
        for f in range(wup_ref.shape[-1] // tf):
            r = jnp.maximum(jnp.dot(h, wup_ref[0, :, f * tf:(f + 1) * tf], preferred_element_type=F32), 0.0)
            acc = acc + jnp.dot((r * r).astype(BF16), wdn_ref[0, f * tf:(f + 1) * tf, :],
                                preferred_element_type=F32)
        o_ref[...] = acc

    i = pl.program_id(0)
    first = 0
    for s in range(n_s):
        run = functools.partial(tile, refs[s * per], refs[s * per + 1:(s + 1) * per], o_refs[s])
        if n_s == 1:
            run()
        else:
            pl.when((i >= first) & (i < first + steps[s]))(run)
        first += steps[s]


def _out_mlp(streams, w_out, gain, w_up, w_down, layer):
    D = streams[0][0].shape[-1]
    flat = [(x.reshape(-1, D), [a.reshape(-1, a.shape[-1]) for a in a_list]) for x, a_list in streams]
    tm = min([MLP_ROWS] + [x.shape[0] for x, _ in flat])
    assert all(x.shape[0] % tm == 0 for x, _ in flat)
    steps = [x.shape[0] // tm for x, _ in flat]
    n_a = len(flat[0][1])
    F = w_up.shape[-1]
    tf = min(MLP_FF, F)
    assert F % tf == 0
    kern = functools.partial(_out_mlp_kernel, n_a=n_a, steps=tuple(steps), tf=tf)
    layer_spec = lambda shape: pl.BlockSpec((1,) + shape[1:], lambda i: (layer, 0, 0),
                                            pipeline_mode=pl.Buffered(1))

    def rows(w, first, n):
        return pl.BlockSpec((tm, w), lambda i: (jnp.minimum(jnp.maximum(i - first, 0), n - 1), 0))

    in_specs, args, out_specs, first = [], [], [], 0
    for (x, a_list), n in zip(flat, steps):
        in_specs += [rows(D, first, n)] + [rows(a.shape[1], first, n) for a in a_list]
        args += [x] + a_list
        out_specs.append(rows(D, first, n))
        first += n
    outs = pl.pallas_call(
        kern,
        grid=(sum(steps),),
        in_specs=in_specs + [_const_spec(w_out.shape), _const_spec((1, D)), layer_spec(w_up.shape),
                             layer_spec(w_down.shape)],
        out_specs=out_specs,
        out_shape=[jax.ShapeDtypeStruct(x.shape, F32) for x, _ in flat],
        compiler_params=pltpu.CompilerParams(dimension_semantics=("arbitrary",), vmem_limit_bytes=VMEM_LIMIT),
        name="out_mlp",
    )(*args, w_out, gain, w_up, w_down)
    return [o.reshape(x.shape) for o, (x, _) in zip(outs, streams)]


def _odd_pre_kernel(x_ref, g_ref, w_ref, qn_ref, kn_ref, q_ref, k_ref, v_ref, *mxu_copies, n_heads):
    dh = LANES
    hd = n_heads * dh
    ns, tl, d_model = x_ref.shape
    x = x_ref[...].reshape(ns * tl, d_model)
    h = (x * _rms_scale(x) * g_ref[...]).astype(BF16)
    q = jnp.dot(h, w_ref[:, 0:hd], preferred_element_type=F32)
    k = jnp.dot(h, w_ref[:, hd:2 * hd], preferred_element_type=F32)
    v = jnp.dot(h, w_ref[:, 2 * hd:3 * hd], preferred_element_type=F32)
    per_seq = lambda t: t.reshape(ns, tl, dh)
    for hh in range(n_heads):
        cols = slice(hh * dh, (hh + 1) * dh)
        qh = q[:, cols]
        kh = k[:, cols]
        q_ref[:, hh] = per_seq((qh * _rms_scale(qh) * qn_ref[...] * (dh ** -0.5 * LOG2E)).astype(q_ref.dtype))
        kh = kh * _rms_scale(kh) * kn_ref[...]
        k_ref[:, hh] = per_seq(kh)
        v_ref[:, hh] = per_seq(v[:, cols])
        if mxu_copies:
            mxu_copies[0][:, hh] = per_seq(kh.astype(BF16))
            mxu_copies[1][:, hh] = per_seq(v[:, cols].astype(BF16))


def _odd_pre(x, gain, w_qkv, qn, kn, *, n_heads, mxu_copies):
    B, L, D = x.shape
    tl = min(PROJ_ROWS, L)
    assert L % tl == 0 and w_qkv.shape[1] == 3 * n_heads * LANES
    kern = functools.partial(_odd_pre_kernel, n_heads=n_heads)
    ns = max(d for d in range(1, max(1, PROJ_ROWS // tl) + 1) if B % d == 0)
    head_major = pl.BlockSpec((ns, n_heads, tl, LANES), lambda b, j: (b, 0, j, 0))
    hm_shape = (B, n_heads, L, LANES)
    n_copies = 2 if mxu_copies else 0
    return pl.pallas_call(
        kern,
        grid=(B // ns, L // tl),
        in_specs=[pl.BlockSpec((ns, tl, D), lambda b, j: (b, j, 0)), _const_spec((1, D)),
                  _const_spec(w_qkv.shape), _const_spec((1, LANES)), _const_spec((1, LANES))],
        out_specs=[head_major] * (3 + n_copies),
        out_shape=([jax.ShapeDtypeStruct(hm_shape, BF16), jax.ShapeDtypeStruct(hm_shape, F32),
                    jax.ShapeDtypeStruct(hm_shape, F32)] + [jax.ShapeDtypeStruct(hm_shape, BF16)] * n_copies),
        compiler_params=pltpu.CompilerParams(dimension_semantics=("arbitrary", "arbitrary"),
                                             vmem_limit_bytes=VMEM_LIMIT),
        name="odd_pre",
    )(x, gain, w_qkv, qn, kn)


def _suffix_ones(tk):
    j = lax.broadcasted_iota(jnp.int32, (2 * tk, tk), 0) & (tk - 1)
    s = lax.broadcasted_iota(jnp.int32, (2 * tk, tk), 1)
    return (j > s).astype(BF16)


def _sb_logits(q, k_blk, u, below=None):
    zs = [_mm_nt(a, b) for a, b in zip(q, k_blk)] if isinstance(q, (list, tuple)) else [_mm_nt(q, k_blk)]
    z = jnp.concatenate(zs, axis=0) if len(zs) > 1 else zs[0]
    t = jnp.maximum(z, 0.0) + jnp.log(1.0 + jnp.exp2(-jnp.abs(z))) * LOG2E
    if below is not None:
        t = jnp.where(below, t, 0.0)
    t_hi = t.astype(BF16)
    t_lo = (t - t_hi.astype(F32)).astype(BF16)
    later = jnp.dot(jnp.concatenate([t_hi, t_lo], axis=1), u, preferred_element_type=F32)
    return z, t, later


def _sb_weights(z, t, later, off, v_blk, acc):
    wts = jnp.exp2(z - t - (later + off))
    own = later[:, 0:1] + t[:, 0:1]
    if not isinstance(v_blk, (list, tuple)):
        return own, acc + _mm(wts, v_blk)
    rows = z.shape[0] // len(v_blk)
    return own, [a + _mm(wts[n * rows:(n + 1) * rows], v) for n, (a, v) in enumerate(zip(acc, v_blk))]


def _sb_block(q, k_blk, v_blk, u, off, acc, below=None):
    return _sb_weights(*_sb_logits(q, k_blk, u, below), off, v_blk, acc)


def _sb_kernel(*refs, tq, tkp, n_past, n_q, hb, qb, n_prev):
    lazy_past = bool(n_past) and n_q == 1
    if lazy_past:
        q_ref, kn_ref, vn_ref, kl_ref, vl_ref, kp_ref, vp_ref, o_ref, kbuf, vbuf, sem = refs
    elif n_past:
        q_ref, kn_ref, vn_ref, kp_ref, vp_ref, o_ref = refs
    else:
        q_ref, kn_ref, vn_ref, o_ref = refs
    u_self = _suffix_ones(tq)
    u_past = _suffix_ones(tkp) if n_past else None
    row = lax.broadcasted_iota(jnp.int32, (tq, tq), 0)
    col = lax.broadcasted_iota(jnp.int32, (tq, tq), 1)
    below = col < row
    not_below = jnp.where(below, 0.0, MASKED)

    walks = []
    if n_q == 1:
        heads = range(hb)
        qs = [q_ref[0, hh] for hh in heads]
        below_all = jnp.concatenate([below] * hb, axis=0)
        spent, accs = _sb_block(qs, [kn_ref[0, hh] for hh in heads], [vn_ref[0, hh] for hh in heads], u_self,
                                jnp.concatenate([not_below] * hb, axis=0),
                                [jnp.zeros((tq, LANES), F32)] * hb, below_all)
        if lazy_past:
            own, accs = _sb_block(qs, [kl_ref[0, hh] for hh in heads], [vl_ref[0, hh] for hh in heads], u_past,
                                  spent, accs)
            spent = spent + own
        walks = [(hh, 0, qs[hh], spent[hh * tq:(hh + 1) * tq], accs[hh], -1, n_past - 2) for hh in heads]
    chains = [(hh, w) for hh in range(hb) for w in range(qb) if n_q > 1]
    logits = {}
    for c, (hh, w) in enumerate(chains):
        i = pl.program_id(2) * qb + w
        q = q_ref[0, hh, w * tq:(w + 1) * tq, :]
        for d in range(n_prev + 1):
            blk = pl.multiple_of(jnp.maximum(i - d, 0) * tq, tq)
            logits[c, d] = _sb_logits(q, kn_ref[0, hh, pl.ds(blk, tq), :], u_self, below if d == 0 else None)
    for c, (hh, w) in enumerate(chains):
        i = pl.program_id(2) * qb + w
        q = q_ref[0, hh, w * tq:(w + 1) * tq, :]
        start = pl.multiple_of(i * tq, tq)
        spent, acc = _sb_weights(*logits[c, 0], not_below, vn_ref[0, hh, pl.ds(start, tq), :],
                                 jnp.zeros((tq, LANES), F32))
        for d in range(1, n_prev + 1):
            prev = pl.multiple_of(jnp.maximum(i - d, 0) * tq, tq)
            has_prev = jnp.full((tq, 1), i, jnp.int32) >= d
            own, acc = _sb_weights(*logits[c, d], jnp.where(has_prev, spent, MASKED),
                                   vn_ref[0, hh, pl.ds(prev, tq), :], acc)
            spent = spent + jnp.where(has_prev, own, 0.0)
        walks.append((hh, w, q, spent, acc, i - n_prev - 1, n_past - 1))

    def walk(q, k_ref, v_ref, hh, tk, u, first, spent, acc, from_hbm=False):
        def cond(st):
            return (st[0] >= 0) & (jnp.min(st[1]) < -LOG2_W_FLOOR)

        def body(st):
            j, spent, acc = st
            s0 = pl.multiple_of(j * tk, tk)
            if from_hbm:
                head = pl.program_id(1) * hb + hh
                copies = [pltpu.make_async_copy(src.at[pl.program_id(0), head, pl.ds(s0, tk), :], dst, sem.at[n])
                          for n, (src, dst) in enumerate(((k_ref, kbuf), (v_ref, vbuf)))]
                for cp in copies:
                    cp.start()
                for cp in copies:
                    cp.wait()
                k_blk, v_blk = kbuf[...], vbuf[...]
            else:
                k_blk, v_blk = k_ref[0, hh, pl.ds(s0, tk), :], v_ref[0, hh, pl.ds(s0, tk), :]
            own, acc = _sb_block(q, k_blk, v_blk, u, spent, acc)
            return j - 1, spent + own, acc

        _, spent, acc = lax.while_loop(cond, body, (first, spent, acc))
        return spent, acc

    for hh, w, _, _, acc, _, _ in walks:
        o_ref[0, w * tq:(w + 1) * tq, hh * LANES:(hh + 1) * LANES] = acc.astype(o_ref.dtype)
    if n_q == 1 and n_past <= 1:
        return

    least = functools.reduce(jnp.minimum, [wk[3] for wk in walks])

    @pl.when(jnp.min(least) < -LOG2_W_FLOOR)
    def _():
        for hh, w, q, spent, acc, first_self, first_past in walks:
            if n_q > 1:
                spent, acc = walk(q, kn_ref, vn_ref, hh, tq, u_self, first_self, spent, acc)
            if n_past:
                spent, acc = walk(q, kp_ref, vp_ref, hh, tkp, u_past, jnp.int32(first_past), spent, acc,
                                  from_hbm=lazy_past)
            o_ref[0, w * tq:(w + 1) * tq, hh * LANES:(hh + 1) * LANES] = acc.astype(o_ref.dtype)


def _stick_breaking(q, k_new, v_new, k_past, v_past):
    B, H, L, dh = q.shape
    assert dh == LANES
    tq = min(SB_TQ, L)
    assert L % tq == 0
    n_q = L // tq
    assert tq & (tq - 1) == 0
    if n_q == 1:
        hb, qb = H, 1
    else:
        kv_bytes = 2 * 2 * L * LANES * k_new.dtype.itemsize
        hb = max(d for d in (4, 2, 1) if H % d == 0 and d * kv_bytes <= SB_KV_VMEM)
        qb = max(d for d in (SB_CHAINS // hb, 2, 1) if n_q % d == 0)
    n_prev = 0 if n_q == 1 else SB_PREV
    assert H % hb == 0 and n_q % qb == 0
    args = [q, k_new, v_new]
    full = lambda n: pl.BlockSpec((1, hb, n, LANES), lambda b, h, i: (b, h, 0, 0))
    in_specs = [pl.BlockSpec((1, hb, tq * qb, LANES), lambda b, h, i: (b, h, i, 0)), full(L), full(L)]
    n_past, tkp = 0, 0
    scratch = []
    if k_past is not None:
        P = k_past.shape[2]
        tkp = min(PROJ_ROWS, P)
        assert P % tkp == 0 and tkp & (tkp - 1) == 0
        n_past = P // tkp
        if n_q == 1:
            last = pl.BlockSpec((1, hb, tkp, LANES), lambda b, h, i: (b, h, n_past - 1, 0))
            hbm = pl.BlockSpec(memory_space=pl.ANY)
            args += [k_past, v_past, k_past, v_past]
            in_specs += [last, last, hbm, hbm]
            scratch = [pltpu.VMEM((tkp, LANES), k_past.dtype), pltpu.VMEM((tkp, LANES), v_past.dtype),
                       pltpu.SemaphoreType.DMA((2,))]
        else:
            args += [k_past, v_past]
            in_specs += [full(P), full(P)]
    kern = functools.partial(_sb_kernel, tq=tq, tkp=tkp, n_past=n_past, n_q=n_q, hb=hb, qb=qb, n_prev=n_prev)
    return pl.pallas_call(
        kern,
        grid=(B, H // hb, n_q // qb),
        in_specs=in_specs,
        out_specs=pl.BlockSpec((1, tq * qb, hb * LANES), lambda b, h, i: (b, i, h)),
        scratch_shapes=scratch,
        out_shape=jax.ShapeDtypeStruct((B, L, H * LANES), BF16),
        compiler_params=pltpu.CompilerParams(dimension_semantics=("arbitrary", "arbitrary", "arbitrary"),
                                             vmem_limit_bytes=VMEM_LIMIT),
        name="stick_breaking",
    )(*args)


def _stack(parts):
    return parts[0][None] if len(parts) == 1 else jnp.stack(parts)


def _pad_lanes(v):
    return jnp.zeros((1, LANES), F32).at[0, :v.shape[0]].set(v.astype(F32))


def _even_mixers(x, s0, qbuf, dbuf, wts, gain_mix):
    (w_all, conv_w, alog_pad, dtb_pad, onorm, dw_w, dw_b, ln_g, ln_b, n_heads, qkv_w, v_w, c_b) = wts
    qkv, gcol, z, c, qlast, dlast = _even_pre(x, gain_mix, w_all, conv_w, alog_pad, dtb_pad, dw_w, dw_b, ln_g, ln_b,
                                              qbuf, dbuf, n_heads=n_heads, qkv_w=qkv_w, v_w=v_w, c_b=c_b)
    o, s_new = _delta(qkv, gcol, z, onorm, s0, n_heads=n_heads)
    return [o, c], s_new, qlast, dlast


def _odd_mixer(x, k_past, v_past, wts, gain_mix):
    w_qkv, qn, kn, n_heads = wts
    mxu_copies = x.shape[1] > SB_TQ
    q, k, v, *kv_mxu = _odd_pre(x, gain_mix, w_qkv, qn, kn, n_heads=n_heads, mxu_copies=mxu_copies)
    o = _stick_breaking(q, *(kv_mxu or (k, v)), k_past, v_past)
    return [o], k, v


def kernel(x_prompt, x_sample, state_delta, state_qkv_conv, state_dw_conv, cache_k, cache_v, norm_mix, norm_mlp, w_in_e, conv_qkv_e, a_log_e, dt_bias_e, onorm_e, dw_w_e, dw_b_e, ln_g_e, ln_b_e, w_out_e, w_qkv_o, qn_o, kn_o, w_out_o, w_up, w_down):
    depth = norm_mix.shape[0]
    bp = x_prompt.shape[0]
    xp, xs = x_prompt, x_sample
    pd, pq, pw, pk, pv = [], [], [], [], []
    sd, sq, sw, sk, sv = [], [], [], [], []
    row = lambda v: v.astype(F32).reshape(1, -1)
    wu, wd = w_up.astype(BF16), w_down.astype(BF16)
    for i in range(depth):
        gm, gl = row(norm_mix[i]), row(norm_mlp[i])
        if i % 2 == 0:
            e = i // 2
            n_heads = a_log_e.shape[1]
            qkv_w = conv_qkv_e.shape[2]
            c_b = dw_w_e.shape[2]
            v_w = n_heads * onorm_e.shape[1]
            w_in = w_in_e[e]
            ab0 = qkv_w + v_w
            ab = jnp.zeros((w_in.shape[0], LANES), w_in.dtype).at[:, :2 * n_heads].set(w_in[:, ab0:ab0 + 2 * n_heads])
            w_all = jnp.concatenate([w_in[:, :ab0], w_in[:, ab0 + 2 * n_heads:], ab], axis=1).astype(BF16)
            w_out = w_out_e[e].astype(BF16)
            wts = (w_all, conv_qkv_e[e].astype(F32), _pad_lanes(a_log_e[e]), _pad_lanes(dt_bias_e[e]),
                   row(onorm_e[e]), dw_w_e[e].astype(F32), row(dw_b_e[e]), row(ln_g_e[e]), row(ln_b_e[e]),
                   n_heads, qkv_w, v_w, c_b)
            kq, kd = conv_qkv_e.shape[1], dw_w_e.shape[1]
            s0 = jnp.zeros((bp,) + state_delta.shape[2:], F32)
            qb0 = jnp.zeros((bp, kq - 1, qkv_w), F32)
            db0 = jnp.zeros((bp, kd - 1, c_b), F32)
            ap, d1, q1, c1 = _even_mixers(xp, s0, qb0, db0, wts, gm)
            as_, d2, q2, c2 = _even_mixers(xs, state_delta[e].astype(F32), state_qkv_conv[e].astype(F32),
                                           state_dw_conv[e].astype(F32), wts, gm)
            pd.append(d1); pq.append(q1); pw.append(c1)
            sd.append(d2); sq.append(q2); sw.append(c2)
        else:
            o = i // 2
            n_heads = cache_k.shape[2]
            w_out = w_out_o[o].astype(BF16)
            wts = (w_qkv_o[o].astype(BF16), row(qn_o[o]), row(kn_o[o]), n_heads)
            ap, k1, v1 = _odd_mixer(xp, None, None, wts, gm)
            as_, k2, v2 = _odd_mixer(xs, cache_k[o].astype(F32), cache_v[o].astype(F32), wts, gm)
            pk.append(k1); pv.append(v1)
            sk.append(k2); sv.append(v2)
        xp, xs = _out_mlp([(xp, ap), (xs, as_)], w_out, gl, wu, wd, i)
    return (xp, xs,
            *(_stack(t) for t in (pd, pq, pw, pk, pv, sd, sq, sw, sk, sv)))
```

```python
import functools

import jax
import jax.numpy as jnp
from jax import lax
from jax.experimental import pallas as pl
from jax.experimental.pallas import tpu as pltpu

F32 = jnp.float32
BF16 = jnp.bfloat16
EPS = 1e-6
CHUNK = 64
LANES = 128
PAIR = 2 * CHUNK
QBUF_OFF = 8
DBUF_OFF = 32
DW_ROWS = 32
SUBLANES = 8
PROJ_ROWS = 256
EVEN_SUB = 2
DELTA_ROWS = 512
MLP_ROWS = 512
MLP_FF = 1024
VMEM_LIMIT = 56 * 1024 * 1024
LOG2E = 1.4426950408889634
LOG2_W_FLOOR = -151.0
SB_TQ = 256
SB_PREV = 1
SB_KV_VMEM = 36 * 1024 * 1024
SB_CHAINS = 16
MASKED = 1e30


def _mm(a, b):
    return jnp.dot(a.astype(BF16), b.astype(BF16), preferred_element_type=F32)


def _mm_nt(a, b):
    return lax.dot_general(a.astype(BF16), b.astype(BF16), (((1,), (1,)), ((), ())),
                           preferred_element_type=F32)


def _sigmoid(x):
    return 0.5 * jnp.tanh(0.5 * x) + 0.5


def _silu(x):
    h = 0.5 * x
    return h + h * jnp.tanh(h)


def _softplus(x):
    return jnp.maximum(x, 0.0) + jnp.log1p(jnp.exp(-jnp.abs(x)))


def _rms_scale(x):
    return lax.rsqrt(jnp.mean(x * x, axis=-1, keepdims=True) + EPS)


def _const_spec(shape):
    nd = len(shape)
    return pl.BlockSpec(shape, lambda *_: (0,) * nd, pipeline_mode=pl.Buffered(1))


def _even_pre_kernel(x_ref, g_ref, w_ref, cw_ref, alog_ref, dtb_ref, dww_ref, dwb_ref, lng_ref, lnb_ref,
                     qbuf_ref, dbuf_ref,
                     qkv_ref, gcol_ref, z_ref, c_ref, qlast_ref, dlast_ref,
                     qext, uext, ushift, *, ns, nsub, tl, qkv_w, v_w, c_b, n_heads):
    kq = cw_ref.shape[0] // SUBLANES
    kd = dww_ref.shape[0] // SUBLANES
    glu0 = qkv_w + v_w
    q_hist = slice(QBUF_OFF - (kq - 1), QBUF_OFF)
    d_hist = slice(DBUF_OFF - (kd - 1), DBUF_OFF)
    q_tail = slice(QBUF_OFF + tl - (kq - 1), QBUF_OFF + tl)
    d_tail = slice(DBUF_OFF + tl - (kd - 1), DBUF_OFF + tl)

    @pl.when(pl.program_id(1) == 0)
    def _():
        for sq in range(ns):
            qext[sq * nsub, q_hist, :] = qbuf_ref[sq]
            uext[sq * nsub, 0:d_hist.start, :] = jnp.zeros((d_hist.start, c_b), F32)
            uext[sq * nsub, d_hist, :] = dbuf_ref[sq]

    def project(units):
        m = len(units) * tl
        x = jnp.concatenate([x_ref[u // nsub, (u % nsub) * tl:(u % nsub + 1) * tl, :] for u in units], axis=0)
        h = (x * _rms_scale(x) * g_ref[...]).astype(BF16)
        qkv_pre = jnp.dot(h, w_ref[:, 0:qkv_w], preferred_element_type=F32)
        z = jnp.dot(h, w_ref[:, qkv_w:glu0], preferred_element_type=F32)
        ga = jnp.dot(h, w_ref[:, glu0:glu0 + c_b], preferred_element_type=F32)
        gb = jnp.dot(h, w_ref[:, glu0 + c_b:glu0 + 2 * c_b], preferred_element_type=F32)
        ug = ga * _sigmoid(gb)
        ab = jnp.dot(h, w_ref[:, glu0 + 2 * c_b:glu0 + 2 * c_b + LANES], preferred_element_type=F32)

        lane = lax.broadcasted_iota(jnp.int32, (m, LANES), 1)
        g = -jnp.exp(alog_ref[...]) * _softplus(ab + dtb_ref[...])
        g = jnp.where(lane < n_heads, g, 0.0)
        beta = _sigmoid(ab)
        row = lax.broadcasted_iota(jnp.int32, (m, m), 0)
        col = lax.broadcasted_iota(jnp.int32, (m, m), 1)
        tri = (((row ^ col) < CHUNK) & (col <= row)).astype(BF16)
        g_hi = g.astype(BF16)
        g_r1 = g - g_hi.astype(F32)
        g_mid = g_r1.astype(BF16)
        g_lo = (g_r1 - g_mid.astype(F32)).astype(BF16)
        gcum = (jnp.dot(tri, g_hi, preferred_element_type=F32)
                + jnp.dot(tri, g_mid, preferred_element_type=F32)
                + jnp.dot(tri, g_lo, preferred_element_type=F32))
        gcol = jnp.where(lane < n_heads, gcum, jnp.where(lane < 2 * n_heads, beta, 0.0))

        for n, u in enumerate(units):
            sq, rows, part = u // nsub, slice((u % nsub) * tl, (u % nsub + 1) * tl), slice(n * tl, (n + 1) * tl)
            qext[u, QBUF_OFF:QBUF_OFF + tl, :] = qkv_pre[part]
            uext[u, DBUF_OFF:DBUF_OFF + tl, :] = ug[part]
            z_ref[sq, rows, :] = z[part]
            gcol_ref[sq, rows, :] = gcol[part]
            if u % nsub:
                qext[u, q_hist, :] = qext[u - 1, q_tail, :]
                uext[u, d_hist, :] = uext[u - 1, d_tail, :]

    def convolve(u):
        sq, rows0 = u // nsub, (u % nsub) * tl
        for s in range(qkv_w // LANES):
            cols = slice(s * LANES, (s + 1) * LANES)
            y = None
            for j in range(kq):
                r0 = QBUF_OFF - (kq - 1) + j
                wj = cw_ref[j * SUBLANES:(j + 1) * SUBLANES, cols]
                t = wj[None] * qext[u, r0:r0 + tl, cols].reshape(tl // SUBLANES, SUBLANES, LANES)
                y = t if y is None else y + t
            y = _silu(y.reshape(tl, LANES))
            if s < 2 * n_heads:
                y = y * lax.rsqrt(jnp.sum(y * y, axis=-1, keepdims=True) + EPS)
            qkv_ref[sq, rows0:rows0 + tl, cols] = y

        span = tl + DBUF_OFF - SUBLANES
        for b in range(1, SUBLANES):
            ushift[b - 1, 0:span, :] = uext[u, b:b + span, :]
        for r in range(tl // DW_ROWS):
            acc = None
            for j in range(kd):
                a, b = divmod(DBUF_OFF - (kd - 1) + j, SUBLANES)
                r0 = a * SUBLANES + r * DW_ROWS
                win = uext[u, r0:r0 + DW_ROWS, :] if b == 0 else ushift[b - 1, r0:r0 + DW_ROWS, :]
                wj = dww_ref[j * SUBLANES:(j + 1) * SUBLANES, :]
                t = wj[None] * win.reshape(DW_ROWS // SUBLANES, SUBLANES, c_b)
                acc = t if acc is None else acc + t
            cpre = acc.reshape(DW_ROWS, c_b) + dwb_ref[...]
            mu = jnp.mean(cpre, axis=-1, keepdims=True)
            xc = cpre - mu
            var = jnp.mean(xc * xc, axis=-1, keepdims=True)
            y = xc * lax.rsqrt(var + EPS) * lng_ref[...] + lnb_ref[...]
            c_ref[sq, rows0 + r * DW_ROWS:rows0 + (r + 1) * DW_ROWS, :] = _silu(y).astype(c_ref.dtype)

    units = list(range(ns * nsub))
    groups = [units] if nsub == 1 else [[u] for u in units]
    for gi, group in enumerate(groups):
        project(group)
        if gi:
            for u in groups[gi - 1]:
                convolve(u)
    for u in groups[-1]:
        convolve(u)

    for sq in range(ns):
        last = sq * nsub + nsub - 1
        qlast = qext[last, q_tail, :]
        qlast_ref[sq] = qlast
        qext[sq * nsub, q_hist, :] = qlast
        dlast = uext[last, d_tail, :]
        dlast_ref[sq] = dlast
        uext[sq * nsub, d_hist, :] = dlast


def _even_pre(x, gain, w_all, conv_w, alog_pad, dtb_pad, dw_w, dw_b, ln_g, ln_b, qbuf, dbuf, *, n_heads, qkv_w, v_w, c_b):
    n_seq, L, D = x.shape
    tl = min(PROJ_ROWS, L)
    assert L % tl == 0 and tl % DW_ROWS == 0 and tl % CHUNK == 0
    kq, kd = conv_w.shape[0], dw_w.shape[0]
    assert kq - 1 <= QBUF_OFF and kd - 1 <= DBUF_OFF and tl >= kd - 1
    conv_w = jnp.repeat(conv_w, SUBLANES, axis=0)
    dw_w = jnp.repeat(dw_w, SUBLANES, axis=0)
    nw = w_all.shape[1]
    ns = max(d for d in range(1, max(1, PROJ_ROWS // tl) + 1) if n_seq % d == 0)
    nsub = max(d for d in (EVEN_SUB, 2, 1) if (L // tl) % d == 0) if ns == 1 else 1
    rows = nsub * tl
    kern = functools.partial(_even_pre_kernel, ns=ns, nsub=nsub, tl=tl, qkv_w=qkv_w, v_w=v_w, c_b=c_b,
                             n_heads=n_heads)
    seq_blk = lambda w: pl.BlockSpec((ns, rows, w), lambda s, j: (s, j, 0))
    per_seq = lambda r, w: pl.BlockSpec((ns, r, w), lambda s, j: (s, 0, 0))
    return pl.pallas_call(
        kern,
        grid=(n_seq // ns, L // rows),
        in_specs=[seq_blk(D), _const_spec((1, D)), _const_spec((D, nw)), _const_spec(conv_w.shape),
                  _const_spec((1, LANES)), _const_spec((1, LANES)), _const_spec(dw_w.shape),
                  _const_spec((1, c_b)), _const_spec((1, c_b)), _const_spec((1, c_b)),
                  per_seq(kq - 1, qkv_w), per_seq(kd - 1, c_b)],
        out_specs=[seq_blk(qkv_w), seq_blk(LANES), seq_blk(v_w), seq_blk(c_b),
                   per_seq(kq - 1, qkv_w), per_seq(kd - 1, c_b)],
        out_shape=[jax.ShapeDtypeStruct((n_seq, L, qkv_w), F32),
                   jax.ShapeDtypeStruct((n_seq, L, LANES), F32),
                   jax.ShapeDtypeStruct((n_seq, L, v_w), F32),
                   jax.ShapeDtypeStruct((n_seq, L, c_b), BF16),
                   jax.ShapeDtypeStruct((n_seq, kq - 1, qkv_w), F32),
                   jax.ShapeDtypeStruct((n_seq, kd - 1, c_b), F32)],
        scratch_shapes=[pltpu.VMEM((ns * nsub, QBUF_OFF + tl, qkv_w), F32),
                        pltpu.VMEM((ns * nsub, DBUF_OFF + tl, c_b), F32),
                        pltpu.VMEM((SUBLANES - 1, DBUF_OFF + tl, c_b), F32)],
        compiler_params=pltpu.CompilerParams(dimension_semantics=("arbitrary", "arbitrary"),
                                             vmem_limit_bytes=VMEM_LIMIT),
        name="even_pre",
    )(x, gain, w_all, conv_w, alog_pad, dtb_pad, dw_w, dw_b, ln_g, ln_b, qbuf, dbuf)


def _unit_lower_inverses(a_list, row, col):
    x = row ^ col
    eye = (row == col).astype(F32)
    a8 = [jnp.where(x < 8, a, 0.0) for a in a_list]
    a8_2 = [_mm(t, t) for t in a8]
    p = [eye - t for t in a8]
    a8_4 = [_mm(t, t) for t in a8_2]
    p = [t + _mm(t, sq) for t, sq in zip(p, a8_2)]
    d = [t + _mm(t, sq) for t, sq in zip(p, a8_4)]
    s = 8
    while s < CHUNK:
        off = [jnp.where((x >= s) & (x < 2 * s), a, 0.0) for a in a_list]
        od = [_mm(o, t) for o, t in zip(off, d)]
        d = [t - _mm(t, u) for t, u in zip(d, od)]
        s *= 2
    return d


def _delta_kernel(qkv_ref, gcol_ref, z_ref, onorm_ref, sin_ref, o_ref, sout_ref, *, tb, n_heads, chained):
    dk = dv = LANES
    qk_w = n_heads * dk
    n_pairs = tb // PAIR
    if chained:
        @pl.when(pl.program_id(1) == 0)
        def _():
            sout_ref[...] = sin_ref[...]

    row = lax.broadcasted_iota(jnp.int32, (PAIR, PAIR), 0)
    col = lax.broadcasted_iota(jnp.int32, (PAIR, PAIR), 1)
    same = (row ^ col) < CHUNK
    causal = same & (col <= row)
    strict = same & (col < row)
    in_chunk = [(col >= c * CHUNK) & (col < (c + 1) * CHUNK) for c in range(2)]

    tiles = [(p, h) for p in range(n_pairs) for h in range(n_heads)]
    q, k, kb, vb, gc, a, qk = {}, {}, {}, {}, {}, {}, {}
    for p, h in tiles:
        rows = slice(p * PAIR, (p + 1) * PAIR)
        q[p, h] = qkv_ref[0, rows, h * dk:(h + 1) * dk] * (dk ** -0.5)
        k[p, h] = qkv_ref[0, rows, qk_w + h * dk:qk_w + (h + 1) * dk]
        gc[p, h] = jnp.broadcast_to(gcol_ref[0, rows, h:h + 1], (PAIR, PAIR))
        bt = jnp.broadcast_to(gcol_ref[0, rows, n_heads + h:n_heads + h + 1], (PAIR, PAIR))
        kb[p, h] = k[p, h] * bt
        vb[p, h] = qkv_ref[0, rows, 2 * qk_w + h * dv:2 * qk_w + (h + 1) * dv] * bt
    for t in tiles:
        diff = gc[t] - gc[t].T
        decay = jnp.where(causal, jnp.exp(jnp.where(causal, diff, 0.0)), 0.0)
        a[t] = jnp.where(strict, _mm_nt(kb[t], k[t]) * decay, 0.0)
        qk[t] = jnp.where(causal, _mm_nt(q[t], k[t]) * decay, 0.0)
    inv = dict(zip(tiles, _unit_lower_inverses([a[t] for t in tiles], row, col)))
    uw, kg_t, gl = {}, {}, {}
    for t in tiles:
        eg = jnp.exp(gc[t])
        uw[t] = _mm(inv[t], jnp.concatenate([vb[t], kb[t] * eg], axis=1))
        gl_rows = [jnp.broadcast_to(gc[t][(c + 1) * CHUNK - 1:(c + 1) * CHUNK, :], (PAIR, PAIR)) for c in range(2)]
        gl[t] = [jnp.exp(g) for g in gl_rows]
        gcl = jnp.where(row < CHUNK, gl_rows[0], gl_rows[1])
        kg_t[t] = (k[t] * jnp.exp(gcl - gc[t])).T
        q[t] = q[t] * eg
    qk_uw = {t: _mm(qk[t], uw[t]) for t in tiles}
    kg_uw = {(t, c): _mm(jnp.where(in_chunk[c], kg_t[t], 0.0), uw[t])
             for t in tiles for c in range(2)}

    state = [sout_ref[0, h] for h in range(n_heads)] if chained else None
    for p in range(n_pairs):
        rows = slice(p * PAIR, (p + 1) * PAIR)
        o_parts = {h: [] for h in range(n_heads)}
        for c in range(2):
            r = slice(c * CHUNK, (c + 1) * CHUNK)
            for h in range(n_heads):
                t = (p, h)
                s_old = state[h] if chained else sin_ref[2 * p + c, h]
                q_eff = q[t][r] - qk_uw[t][r, dv:]
                xs = _mm(jnp.concatenate([kg_uw[t, c][:, dv:], q_eff], axis=0), s_old)
                o_parts[h].append(qk_uw[t][r, :dv] + xs[dk:])
                s_new = s_old * gl[t][c] + (kg_uw[t, c][:, :dv] - xs[:dk])
                if chained:
                    state[h] = s_new
                else:
                    sout_ref[2 * p + c, h] = s_new
        for h in range(n_heads):
            o = jnp.concatenate(o_parts[h], axis=0)
            zz = z_ref[0, rows, h * dv:(h + 1) * dv].astype(F32)
            y = (o * _rms_scale(o) * onorm_ref[...]) * _silu(zz)
            o_ref[0, rows, h * dv:(h + 1) * dv] = y.astype(o_ref.dtype)
    if chained:
        for h in range(n_heads):
            sout_ref[0, h] = state[h]


def _delta(qkv, gcol, z, onorm, s0, *, n_heads):
    n_seq, L, qkv_w = qkv.shape
    v_w = z.shape[-1]
    assert qkv_w == 3 * n_heads * LANES and v_w == n_heads * LANES
    if L % PAIR == 0:
        chained, n_grp, Lg, per_grp = True, n_seq, L, 1
    else:
        assert L == CHUNK and n_seq % 2 == 0
        per_grp = max(d for d in range(2, DELTA_ROWS // CHUNK + 1, 2) if n_seq % d == 0)
        chained, n_grp, Lg = False, n_seq // per_grp, per_grp * CHUNK
        qkv, gcol, z = (t.reshape(n_grp, Lg, t.shape[-1]) for t in (qkv, gcol, z))
    tb = min(DELTA_ROWS, Lg)
    assert Lg % tb == 0
    kern = functools.partial(_delta_kernel, tb=tb, n_heads=n_heads, chained=chained)
    blk = lambda w: pl.BlockSpec((1, tb, w), lambda s, j: (s, j, 0))
    st = pl.BlockSpec((per_grp, n_heads, LANES, LANES), lambda s, j: (s, 0, 0, 0))
    o, s_new = pl.pallas_call(
        kern,
        grid=(n_grp, Lg // tb),
        in_specs=[blk(qkv_w), blk(LANES), blk(v_w), _const_spec((1, LANES)), st],
        out_specs=[blk(v_w), st],
        out_shape=[jax.ShapeDtypeStruct((n_grp, Lg, v_w), BF16),
                   jax.ShapeDtypeStruct(s0.shape, F32)],
        compiler_params=pltpu.CompilerParams(dimension_semantics=("arbitrary", "arbitrary"),
                                             vmem_limit_bytes=VMEM_LIMIT),
        name="delta_rule",
    )(qkv, gcol, z, onorm, s0)
    return o.reshape(n_seq, L, v_w), s_new


def _out_mlp_kernel(*refs, n_a, steps, tf):
    n_s = len(steps)
    per = 1 + n_a
    w_ref, g_ref, wup_ref, wdn_ref = refs[n_s * per:n_s * per + 4]
    o_refs = refs[n_s * per + 4:]

    def tile(x_ref, a_refs, o_ref):
        a = a_refs[0][...] if n_a == 1 else jnp.concatenate([a_ref[...] for a_ref in a_refs], axis=-1)
        x1 = x_ref[...] + jnp.dot(a, w_ref[...], preferred_element_type=F32)
        h = (x1 * _rms_scale(x1) * g_ref[...]).astype(BF16)
        acc = x1
        for f in range(wup_ref.shape[-1] // tf):
            r = jnp.maximum(jnp.dot(h, wup_ref[0, :, f * tf:(f + 1) * tf], preferred_element_type=F32), 0.0)
            acc = acc + jnp.dot((r * r).astype(BF16), wdn_ref[0, f * tf:(f + 1) * tf, :],
                                preferred_element_type=F32)
        o_ref[...] = acc

    i = pl.program_id(0)
    first = 0
    for s in range(n_s):
        run = functools.partial(tile, refs[s * per], refs[s * per + 1:(s + 1) * per], o_refs[s])
        if n_s == 1:
            run()
        else:
            pl.when((i >= first) & (i < first + steps[s]))(run)
        first += steps[s]


def _out_mlp(streams, w_out, gain, w_up, w_down, layer):
    D = streams[0][0].shape[-1]
    flat = [(x.reshape(-1, D), [a.reshape(-1, a.shape[-1]) for a in a_list]) for x, a_list in streams]
    tm = min([MLP_ROWS] + [x.shape[0] for x, _ in flat])
    assert all(x.shape[0] % tm == 0 for x, _ in flat)
    steps = [x.shape[0] // tm for x, _ in flat]
    n_a = len(flat[0][1])
    F = w_up.shape[-1]
    tf = min(MLP_FF, F)
    assert F % tf == 0
    kern = functools.partial(_out_mlp_kernel, n_a=n_a, steps=tuple(steps), tf=tf)
    layer_spec = lambda shape: pl.BlockSpec((1,) + shape[1:], lambda i: (layer, 0, 0),
                                            pipeline_mode=pl.Buffered(1))

    def rows(w, first, n):
        return pl.BlockSpec((tm, w), lambda i: (jnp.minimum(jnp.maximum(i - first, 0), n - 1), 0))

    in_specs, args, out_specs, first = [], [], [], 0
    for (x, a_list), n in zip(flat, steps):
        in_specs += [rows(D, first, n)] + [rows(a.shape[1], first, n) for a in a_list]
        args += [x] + a_list
        out_specs.append(rows(D, first, n))
        first += n
    outs = pl.pallas_call(
        kern,
        grid=(sum(steps),),
        in_specs=in_specs + [_const_spec(w_out.shape), _const_spec((1, D)), layer_spec(w_up.shape),
                             layer_spec(w_down.shape)],
        out_specs=out_specs,
        out_shape=[jax.ShapeDtypeStruct(x.shape, F32) for x, _ in flat],
        compiler_params=pltpu.CompilerParams(dimension_semantics=("arbitrary",), vmem_limit_bytes=VMEM_LIMIT),
        name="out_mlp",
    )(*args, w_out, gain, w_up, w_down)
    return [o.reshape(x.shape) for o, (x, _) in zip(outs, streams)]


def _odd_pre_kernel(x_ref, g_ref, w_ref, qn_ref, kn_ref, q_ref, k_ref, v_ref, *mxu_copies, n_heads):
    dh = LANES
    hd = n_heads * dh
    ns, tl, d_model = x_ref.shape
    x = x_ref[...].reshape(ns * tl, d_model)
    h = (x * _rms_scale(x) * g_ref[...]).astype(BF16)
    q = jnp.dot(h, w_ref[:, 0:hd], preferred_element_type=F32)
    k = jnp.dot(h, w_ref[:, hd:2 * hd], preferred_element_type=F32)
    v = jnp.dot(h, w_ref[:, 2 * hd:3 * hd], preferred_element_type=F32)
    per_seq = lambda t: t.reshape(ns, tl, dh)
    for hh in range(n_heads):
        cols = slice(hh * dh, (hh + 1) * dh)
        qh = q[:, cols]
        kh = k[:, cols]
        q_ref[:, hh] = per_seq((qh * _rms_scale(qh) * qn_ref[...] * (dh ** -0.5 * LOG2E)).astype(q_ref.dtype))
        kh = kh * _rms_scale(kh) * kn_ref[...]
        k_ref[:, hh] = per_seq(kh)
        v_ref[:, hh] = per_seq(v[:, cols])
        if mxu_copies:
            mxu_copies[0][:, hh] = per_seq(kh.astype(BF16))
            mxu_copies[1][:, hh] = per_seq(v[:, cols].astype(BF16))


def _odd_pre(x, gain, w_qkv, qn, kn, *, n_heads, mxu_copies):
    B, L, D = x.shape
    tl = min(PROJ_ROWS, L)
    assert L % tl == 0 and w_qkv.shape[1] == 3 * n_heads * LANES
    kern = functools.partial(_odd_pre_kernel, n_heads=n_heads)
    ns = max(d for d in range(1, max(1, PROJ_ROWS // tl) + 1) if B % d == 0)
    head_major = pl.BlockSpec((ns, n_heads, tl, LANES), lambda b, j: (b, 0, j, 0))
    hm_shape = (B, n_heads, L, LANES)
    n_copies = 2 if mxu_copies else 0
    return pl.pallas_call(
        kern,
        grid=(B // ns, L // tl),
        in_specs=[pl.BlockSpec((ns, tl, D), lambda b, j: (b, j, 0)), _const_spec((1, D)),
                  _const_spec(w_qkv.shape), _const_spec((1, LANES)), _const_spec((1, LANES))],
        out_specs=[head_major] * (3 + n_copies),
        out_shape=([jax.ShapeDtypeStruct(hm_shape, BF16), jax.ShapeDtypeStruct(hm_shape, F32),
                    jax.ShapeDtypeStruct(hm_shape, F32)] + [jax.ShapeDtypeStruct(hm_shape, BF16)] * n_copies),
        compiler_params=pltpu.CompilerParams(dimension_semantics=("arbitrary", "arbitrary"),
                                             vmem_limit_bytes=VMEM_LIMIT),
        name="odd_pre",
    )(x, gain, w_qkv, qn, kn)


def _suffix_ones(tk):
    j = lax.broadcasted_iota(jnp.int32, (2 * tk, tk), 0) & (tk - 1)
    s = lax.broadcasted_iota(jnp.int32, (2 * tk, tk), 1)
    return (j > s).astype(BF16)


def _sb_logits(q, k_blk, u, below=None):
    zs = [_mm_nt(a, b) for a, b in zip(q, k_blk)] if isinstance(q, (list, tuple)) else [_mm_nt(q, k_blk)]
    z = jnp.concatenate(zs, axis=0) if len(zs) > 1 else zs[0]
    t = jnp.maximum(z, 0.0) + jnp.log(1.0 + jnp.exp2(-jnp.abs(z))) * LOG2E
    if below is not None:
        t = jnp.where(below, t, 0.0)
    t_hi = t.astype(BF16)
    t_lo = (t - t_hi.astype(F32)).astype(BF16)
    later = jnp.dot(jnp.concatenate([t_hi, t_lo], axis=1), u, preferred_element_type=F32)
    return z, t, later


def _sb_weights(z, t, later, off, v_blk, acc):
    wts = jnp.exp2(z - t - (later + off))
    own = later[:, 0:1] + t[:, 0:1]
    if not isinstance(v_blk, (list, tuple)):
        return own, acc + _mm(wts, v_blk)
    rows = z.shape[0] // len(v_blk)
    return own, [a + _mm(wts[n * rows:(n + 1) * rows], v) for n, (a, v) in enumerate(zip(acc, v_blk))]


def _sb_block(q, k_blk, v_blk, u, off, acc, below=None):
    return _sb_weights(*_sb_logits(q, k_blk, u, below), off, v_blk, acc)


def _sb_kernel(*refs, tq, tkp, n_past, n_q, hb, qb, n_prev):
    lazy_past = bool(n_past) and n_q == 1
    if lazy_past:
        q_ref, kn_ref, vn_ref, kl_ref, vl_ref, kp_ref, vp_ref, o_ref, kbuf, vbuf, sem = refs
    elif n_past:
        q_ref, kn_ref, vn_ref, kp_ref, vp_ref, o_ref = refs
    else:
        q_ref, kn_ref, vn_ref, o_ref = refs
    u_self = _suffix_ones(tq)
    u_past = _suffix_ones(tkp) if n_past else None
    row = lax.broadcasted_iota(jnp.int32, (tq, tq), 0)
    col = lax.broadcasted_iota(jnp.int32, (tq, tq), 1)
    below = col < row
    not_below = jnp.where(below, 0.0, MASKED)

    walks = []
    if n_q == 1:
        heads = range(hb)
        qs = [q_ref[0, hh] for hh in heads]
        below_all = jnp.concatenate([below] * hb, axis=0)
        spent, accs = _sb_block(qs, [kn_ref[0, hh] for hh in heads], [vn_ref[0, hh] for hh in heads], u_self,
                                jnp.concatenate([not_below] * hb, axis=0),
                                [jnp.zeros((tq, LANES), F32)] * hb, below_all)
        if lazy_past:
            own, accs = _sb_block(qs, [kl_ref[0, hh] for hh in heads], [vl_ref[0, hh] for hh in heads], u_past,
                                  spent, accs)
            spent = spent + own
        walks = [(hh, 0, qs[hh], spent[hh * tq:(hh + 1) * tq], accs[hh], -1, n_past - 2) for hh in heads]
    chains = [(hh, w) for hh in range(hb) for w in range(qb) if n_q > 1]
    logits = {}
    for c, (hh, w) in enumerate(chains):
        i = pl.program_id(2) * qb + w
        q = q_ref[0, hh, w * tq:(w + 1) * tq, :]
        for d in range(n_prev + 1):
            blk = pl.multiple_of(jnp.maximum(i - d, 0) * tq, tq)
            logits[c, d] = _sb_logits(q, kn_ref[0, hh, pl.ds(blk, tq), :], u_self, below if d == 0 else None)
    for c, (hh, w) in enumerate(chains):
        i = pl.program_id(2) * qb + w
        q = q_ref[0, hh, w * tq:(w + 1) * tq, :]
        start = pl.multiple_of(i * tq, tq)
        spent, acc = _sb_weights(*logits[c, 0], not_below, vn_ref[0, hh, pl.ds(start, tq), :],
                                 jnp.zeros((tq, LANES), F32))
        for d in range(1, n_prev + 1):
            prev = pl.multiple_of(jnp.maximum(i - d, 0) * tq, tq)
            has_prev = jnp.full((tq, 1), i, jnp.int32) >= d
            own, acc = _sb_weights(*logits[c, d], jnp.where(has_prev, spent, MASKED),
                                   vn_ref[0, hh, pl.ds(prev, tq), :], acc)
            spent = spent + jnp.where(has_prev, own, 0.0)
        walks.append((hh, w, q, spent, acc, i - n_prev - 1, n_past - 1))

    def walk(q, k_ref, v_ref, hh, tk, u, first, spent, acc, from_hbm=False):
        def cond(st):
            return (st[0] >= 0) & (jnp.min(st[1]) < -LOG2_W_FLOOR)

        def body(st):
            j, spent, acc = st
            s0 = pl.multiple_of(j * tk, tk)
            if from_hbm:
                head = pl.program_id(1) * hb + hh
                copies = [pltpu.make_async_copy(src.at[pl.program_id(0), head, pl.ds(s0, tk), :], dst, sem.at[n])
                          for n, (src, dst) in enumerate(((k_ref, kbuf), (v_ref, vbuf)))]
                for cp in copies:
                    cp.start()
                for cp in copies:
                    cp.wait()
                k_blk, v_blk = kbuf[...], vbuf[...]
            else:
                k_blk, v_blk = k_ref[0, hh, pl.ds(s0, tk), :], v_ref[0, hh, pl.ds(s0, tk), :]
            own, acc = _sb_block(q, k_blk, v_blk, u, spent, acc)
            return j - 1, spent + own, acc

        _, spent, acc = lax.while_loop(cond, body, (first, spent, acc))
        return spent, acc

    for hh, w, _, _, acc, _, _ in walks:
        o_ref[0, w * tq:(w + 1) * tq, hh * LANES:(hh + 1) * LANES] = acc.astype(o_ref.dtype)
    if n_q == 1 and n_past <= 1:
        return

    least = functools.reduce(jnp.minimum, [wk[3] for wk in walks])

    @pl.when(jnp.min(least) < -LOG2_W_FLOOR)
    def _():
        for hh, w, q, spent, acc, first_self, first_past in walks:
            if n_q > 1:
                spent, acc = walk(q, kn_ref, vn_ref, hh, tq, u_self, first_self, spent, acc)
            if n_past:
                spent, acc = walk(q, kp_ref, vp_ref, hh, tkp, u_past, jnp.int32(first_past), spent, acc,
                                  from_hbm=lazy_past)
            o_ref[0, w * tq:(w + 1) * tq, hh * LANES:(hh + 1) * LANES] = acc.astype(o_ref.dtype)


def _stick_breaking(q, k_new, v_new, k_past, v_past):
    B, H, L, dh = q.shape
    assert dh == LANES
    tq = min(SB_TQ, L)
    assert L % tq == 0
    n_q = L // tq
    assert tq & (tq - 1) == 0
    if n_q == 1:
        hb, qb = H, 1
    else:
        kv_bytes = 2 * 2 * L * LANES * k_new.dtype.itemsize
        hb = max(d for d in (4, 2, 1) if H % d == 0 and d * kv_bytes <= SB_KV_VMEM)
        qb = max(d for d in (SB_CHAINS // hb, 2, 1) if n_q % d == 0)
    n_prev = 0 if n_q == 1 else SB_PREV
    assert H % hb == 0 and n_q % qb == 0
    args = [q, k_new, v_new]
    full = lambda n: pl.BlockSpec((1, hb, n, LANES), lambda b, h, i: (b, h, 0, 0))
    in_specs = [pl.BlockSpec((1, hb, tq * qb, LANES), lambda b, h, i: (b, h, i, 0)), full(L), full(L)]
    n_past, tkp = 0, 0
    scratch = []
    if k_past is not None:
        P = k_past.shape[2]
        tkp = min(PROJ_ROWS, P)
        assert P % tkp == 0 and tkp & (tkp - 1) == 0
        n_past = P // tkp
        if n_q == 1:
            last = pl.BlockSpec((1, hb, tkp, LANES), lambda b, h, i: (b, h, n_past - 1, 0))
            hbm = pl.BlockSpec(memory_space=pl.ANY)
            args += [k_past, v_past, k_past, v_past]
            in_specs += [last, last, hbm, hbm]
            scratch = [pltpu.VMEM((tkp, LANES), k_past.dtype), pltpu.VMEM((tkp, LANES), v_past.dtype),
                       pltpu.SemaphoreType.DMA((2,))]
        else:
            args += [k_past, v_past]
            in_specs += [full(P), full(P)]
    kern = functools.partial(_sb_kernel, tq=tq, tkp=tkp, n_past=n_past, n_q=n_q, hb=hb, qb=qb, n_prev=n_prev)
    return pl.pallas_call(
        kern,
        grid=(B, H // hb, n_q // qb),
        in_specs=in_specs,
        out_specs=pl.BlockSpec((1, tq * qb, hb * LANES), lambda b, h, i: (b, i, h)),
        scratch_shapes=scratch,
        out_shape=jax.ShapeDtypeStruct((B, L, H * LANES), BF16),
        compiler_params=pltpu.CompilerParams(dimension_semantics=("arbitrary", "arbitrary", "arbitrary"),
                                             vmem_limit_bytes=VMEM_LIMIT),
        name="stick_breaking",
    )(*args)


def _stack(parts):
    return parts[0][None] if len(parts) == 1 else jnp.stack(parts)


def _pad_lanes(v):
    return jnp.zeros((1, LANES), F32).at[0, :v.shape[0]].set(v.astype(F32))


def _even_mixers(x, s0, qbuf, dbuf, wts, gain_mix):
    (w_all, conv_w, alog_pad, dtb_pad, onorm, dw_w, dw_b, ln_g, ln_b, n_heads, qkv_w, v_w, c_b) = wts
    qkv, gcol, z, c, qlast, dlast = _even_pre(x, gain_mix, w_all, conv_w, alog_pad, dtb_pad, dw_w, dw_b, ln_g, ln_b,
                                              qbuf, dbuf, n_heads=n_heads, qkv_w=qkv_w, v_w=v_w, c_b=c_b)
    o, s_new = _delta(qkv, gcol, z, onorm, s0, n_heads=n_heads)
    return [o, c], s_new, qlast, dlast


def _odd_mixer(x, k_past, v_past, wts, gain_mix):
    w_qkv, qn, kn, n_heads = wts
    mxu_copies = x.shape[1] > SB_TQ
    q, k, v, *kv_mxu = _odd_pre(x, gain_mix, w_qkv, qn, kn, n_heads=n_heads, mxu_copies=mxu_copies)
    o = _stick_breaking(q, *(kv_mxu or (k, v)), k_past, v_past)
    return [o], k, v


def kernel(x_prompt, x_sample, state_delta, state_qkv_conv, state_dw_conv, cache_k, cache_v, norm_mix, norm_mlp, w_in_e, conv_qkv_e, a_log_e, dt_bias_e, onorm_e, dw_w_e, dw_b_e, ln_g_e, ln_b_e, w_out_e, w_qkv_o, qn_o, kn_o, w_out_o, w_up, w_down):
    depth = norm_mix.shape[0]
    bp = x_prompt.shape[0]
    xp, xs = x_prompt, x_sample
    pd, pq, pw, pk, pv = [], [], [], [], []
    sd, sq, sw, sk, sv = [], [], [], [], []
    row = lambda v: v.astype(F32).reshape(1, -1)
    wu, wd = w_up.astype(BF16), w_down.astype(BF16)
    for i in range(depth):
        gm, gl = row(norm_mix[i]), row(norm_mlp[i])
        if i % 2 == 0:
            e = i // 2
            n_heads = a_log_e.shape[1]
            qkv_w = conv_qkv_e.shape[2]
            c_b = dw_w_e.shape[2]
            v_w = n_heads * onorm_e.shape[1]
            w_in = w_in_e[e]
            ab0 = qkv_w + v_w
            ab = jnp.zeros((w_in.shape[0], LANES), w_in.dtype).at[:, :2 * n_heads].set(w_in[:, ab0:ab0 + 2 * n_heads])
            w_all = jnp.concatenate([w_in[:, :ab0], w_in[:, ab0 + 2 * n_heads:], ab], axis=1).astype(BF16)
            w_out = w_out_e[e].astype(BF16)
            wts = (w_all, conv_qkv_e[e].astype(F32), _pad_lanes(a_log_e[e]), _pad_lanes(dt_bias_e[e]),
                   row(onorm_e[e]), dw_w_e[e].astype(F32), row(dw_b_e[e]), row(ln_g_e[e]), row(ln_b_e[e]),
                   n_heads, qkv_w, v_w, c_b)
            kq, kd = conv_qkv_e.shape[1], dw_w_e.shape[1]
            s0 = jnp.zeros((bp,) + state_delta.shape[2:], F32)
            qb0 = jnp.zeros((bp, kq - 1, qkv_w), F32)
            db0 = jnp.zeros((bp, kd - 1, c_b), F32)
            ap, d1, q1, c1 = _even_mixers(xp, s0, qb0, db0, wts, gm)
            as_, d2, q2, c2 = _even_mixers(xs, state_delta[e].astype(F32), state_qkv_conv[e].astype(F32),
                                           state_dw_conv[e].astype(F32), wts, gm)
            pd.append(d1); pq.append(q1); pw.append(c1)
            sd.append(d2); sq.append(q2); sw.append(c2)
        else:
            o = i // 2
            n_heads = cache_k.shape[2]
            w_out = w_out_o[o].astype(BF16)
            wts = (w_qkv_o[o].astype(BF16), row(qn_o[o]), row(kn_o[o]), n_heads)
            ap, k1, v1 = _odd_mixer(xp, None, None, wts, gm)
            as_, k2, v2 = _odd_mixer(xs, cache_k[o].astype(F32), cache_v[o].astype(F32), wts, gm)
            pk.append(k1); pv.append(v1)
            sk.append(k2); sv.append(v2)
        xp, xs = _out_mlp([(xp, ap), (xs, as_)], w_out, gl, wu, wd, i)
    return (xp, xs,
            *(_stack(t) for t in (pd, pq, pw, pk, pv, sd, sq, sw, sk, sv)))
```

```python
import functools

import jax
import jax.numpy as jnp
from jax import lax
from jax.experimental import pallas as pl
from jax.experimental.pallas import tpu as pltpu

F32 = jnp.float32
BF16 = jnp.bfloat16
EPS = 1e-6
CHUNK = 64
LANES = 128
PAIR = 2 * CHUNK
QBUF_OFF = 8
DBUF_OFF = 32
DW_ROWS = 32
SUBLANES = 8
PROJ_ROWS = 256
EVEN_SUB = 4
DELTA_ROWS = 512
MLP_ROWS = 512
MLP_FF = 1024
VMEM_LIMIT = 56 * 1024 * 1024
LOG2E = 1.4426950408889634
LOG2_W_FLOOR = -151.0
SB_TQ = 256
SB_PREV = 1
SB_KV_VMEM = 36 * 1024 * 1024
SB_CHAINS = 16
MASKED = 1e30


def _mm(a, b):
    return jnp.dot(a.astype(BF16), b.astype(BF16), preferred_element_type=F32)


def _mm_nt(a, b):
    return lax.dot_general(a.astype(BF16), b.astype(BF16), (((1,), (1,)), ((), ())),
                           preferred_element_type=F32)


def _sigmoid(x):
    return 0.5 * jnp.tanh(0.5 * x) + 0.5


def _silu(x):
    h = 0.5 * x
    return h + h * jnp.tanh(h)


def _softplus(x):
    return jnp.maximum(x, 0.0) + jnp.log1p(jnp.exp(-jnp.abs(x)))


def _rms_scale(x):
    return lax.rsqrt(jnp.mean(x * x, axis=-1, keepdims=True) + EPS)


def _const_spec(shape):
    nd = len(shape)
    return pl.BlockSpec(shape, lambda *_: (0,) * nd, pipeline_mode=pl.Buffered(1))


def _even_pre_kernel(x_ref, g_ref, w_ref, cw_ref, alog_ref, dtb_ref, dww_ref, dwb_ref, lng_ref, lnb_ref,
                     qbuf_ref, dbuf_ref,
                     qkv_ref, gcol_ref, z_ref, c_ref, qlast_ref, dlast_ref,
                     qext, uext, ushift, *, ns, nsub, tl, qkv_w, v_w, c_b, n_heads):
    kq = cw_ref.shape[0] // SUBLANES
    kd = dww_ref.shape[0] // SUBLANES
    glu0 = qkv_w + v_w
    q_hist = slice(QBUF_OFF - (kq - 1), QBUF_OFF)
    d_hist = slice(DBUF_OFF - (kd - 1), DBUF_OFF)
    q_tail = slice(QBUF_OFF + tl - (kq - 1), QBUF_OFF + tl)
    d_tail = slice(DBUF_OFF + tl - (kd - 1), DBUF_OFF + tl)

    @pl.when(pl.program_id(1) == 0)
    def _():
        for sq in range(ns):
            qext[sq * nsub, q_hist, :] = qbuf_ref[sq]
            uext[sq * nsub, 0:d_hist.start, :] = jnp.zeros((d_hist.start, c_b), F32)
            uext[sq * nsub, d_hist, :] = dbuf_ref[sq]

    def project(units):
        m = len(units) * tl
        x = jnp.concatenate([x_ref[u // nsub, (u % nsub) * tl:(u % nsub + 1) * tl, :] for u in units], axis=0)
        h = (x * _rms_scale(x) * g_ref[...]).astype(BF16)
        qkv_pre = jnp.dot(h, w_ref[:, 0:qkv_w], preferred_element_type=F32)
        z = jnp.dot(h, w_ref[:, qkv_w:glu0], preferred_element_type=F32)
        ga = jnp.dot(h, w_ref[:, glu0:glu0 + c_b], preferred_element_type=F32)
        gb = jnp.dot(h, w_ref[:, glu0 + c_b:glu0 + 2 * c_b], preferred_element_type=F32)
        ug = ga * _sigmoid(gb)
        ab = jnp.dot(h, w_ref[:, glu0 + 2 * c_b:glu0 + 2 * c_b + LANES], preferred_element_type=F32)

        lane = lax.broadcasted_iota(jnp.int32, (m, LANES), 1)
        g = -jnp.exp(alog_ref[...]) * _softplus(ab + dtb_ref[...])
        g = jnp.where(lane < n_heads, g, 0.0)
        beta = _sigmoid(ab)
        row = lax.broadcasted_iota(jnp.int32, (m, m), 0)
        col = lax.broadcasted_iota(jnp.int32, (m, m), 1)
        tri = (((row ^ col) < CHUNK) & (col <= row)).astype(BF16)
        g_hi = g.astype(BF16)
        g_r1 = g - g_hi.astype(F32)
        g_mid = g_r1.astype(BF16)
        g_lo = (g_r1 - g_mid.astype(F32)).astype(BF16)
        gcum = (jnp.dot(tri, g_hi, preferred_element_type=F32)
                + jnp.dot(tri, g_mid, preferred_element_type=F32)
                + jnp.dot(tri, g_lo, preferred_element_type=F32))
        gcol = jnp.where(lane < n_heads, gcum, jnp.where(lane < 2 * n_heads, beta, 0.0))

        for n, u in enumerate(units):
            sq, rows, part = u // nsub, slice((u % nsub) * tl, (u % nsub + 1) * tl), slice(n * tl, (n + 1) * tl)
            qext[u, QBUF_OFF:QBUF_OFF + tl, :] = qkv_pre[part]
            uext[u, DBUF_OFF:DBUF_OFF + tl, :] = ug[part]
            z_ref[sq, rows, :] = z[part]
            gcol_ref[sq, rows, :] = gcol[part]
            if u % nsub:
                qext[u, q_hist, :] = qext[u - 1, q_tail, :]
                uext[u, d_hist, :] = uext[u - 1, d_tail, :]

    def convolve(u):
        sq, rows0 = u // nsub, (u % nsub) * tl
        for s in range(qkv_w // LANES):
            cols = slice(s * LANES, (s + 1) * LANES)
            y = None
            for j in range(kq):
                r0 = QBUF_OFF - (kq - 1) + j
                wj = cw_ref[j * SUBLANES:(j + 1) * SUBLANES, cols]
                t = wj[None] * qext[u, r0:r0 + tl, cols].reshape(tl // SUBLANES, SUBLANES, LANES)
                y = t if y is None else y + t
            y = _silu(y.reshape(tl, LANES))
            if s < 2 * n_heads:
                y = y * lax.rsqrt(jnp.sum(y * y, axis=-1, keepdims=True) + EPS)
            qkv_ref[sq, rows0:rows0 + tl, cols] = y

        span = tl + DBUF_OFF - SUBLANES
        for b in range(1, SUBLANES):
            ushift[b - 1, 0:span, :] = uext[u, b:b + span, :]
        for r in range(tl // DW_ROWS):
            acc = None
            for j in range(kd):
                a, b = divmod(DBUF_OFF - (kd - 1) + j, SUBLANES)
                r0 = a * SUBLANES + r * DW_ROWS
                win = uext[u, r0:r0 + DW_ROWS, :] if b == 0 else ushift[b - 1, r0:r0 + DW_ROWS, :]
                wj = dww_ref[j * SUBLANES:(j + 1) * SUBLANES, :]
                t = wj[None] * win.reshape(DW_ROWS // SUBLANES, SUBLANES, c_b)
                acc = t if acc is None else acc + t
            cpre = acc.reshape(DW_ROWS, c_b) + dwb_ref[...]
            mu = jnp.mean(cpre, axis=-1, keepdims=True)
            xc = cpre - mu
            var = jnp.mean(xc * xc, axis=-1, keepdims=True)
            y = xc * lax.rsqrt(var + EPS) * lng_ref[...] + lnb_ref[...]
            c_ref[sq, rows0 + r * DW_ROWS:rows0 + (r + 1) * DW_ROWS, :] = _silu(y).astype(c_ref.dtype)

    units = list(range(ns * nsub))
    groups = [units] if nsub == 1 else [[u] for u in units]
    for gi, group in enumerate(groups):
        project(group)
        if gi:
            for u in groups[gi - 1]:
                convolve(u)
    for u in groups[-1]:
        convolve(u)

    for sq in range(ns):
        last = sq * nsub + nsub - 1
        qlast = qext[last, q_tail, :]
        qlast_ref[sq] = qlast
        qext[sq * nsub, q_hist, :] = qlast
        dlast = uext[last, d_tail, :]
        dlast_ref[sq] = dlast
        uext[sq * nsub, d_hist, :] = dlast


def _even_pre(x, gain, w_all, conv_w, alog_pad, dtb_pad, dw_w, dw_b, ln_g, ln_b, qbuf, dbuf, *, n_heads, qkv_w, v_w, c_b):
    n_seq, L, D = x.shape
    tl = min(PROJ_ROWS, L)
    assert L % tl == 0 and tl % DW_ROWS == 0 and tl % CHUNK == 0
    kq, kd = conv_w.shape[0], dw_w.shape[0]
    assert kq - 1 <= QBUF_OFF and kd - 1 <= DBUF_OFF and tl >= kd - 1
    conv_w = jnp.repeat(conv_w, SUBLANES, axis=0)
    dw_w = jnp.repeat(dw_w, SUBLANES, axis=0)
    nw = w_all.shape[1]
    ns = max(d for d in range(1, max(1, PROJ_ROWS // tl) + 1) if n_seq % d == 0)
    nsub = max(d for d in (EVEN_SUB, 2, 1) if (L // tl) % d == 0) if ns == 1 else 1
    rows = nsub * tl
    kern = functools.partial(_even_pre_kernel, ns=ns, nsub=nsub, tl=tl, qkv_w=qkv_w, v_w=v_w, c_b=c_b,
                             n_heads=n_heads)
    seq_blk = lambda w: pl.BlockSpec((ns, rows, w), lambda s, j: (s, j, 0))
    per_seq = lambda r, w: pl.BlockSpec((ns, r, w), lambda s, j: (s, 0, 0))
    return pl.pallas_call(
        kern,
        grid=(n_seq // ns, L // rows),
        in_specs=[seq_blk(D), _const_spec((1, D)), _const_spec((D, nw)), _const_spec(conv_w.shape),
                  _const_spec((1, LANES)), _const_spec((1, LANES)), _const_spec(dw_w.shape),
                  _const_spec((1, c_b)), _const_spec((1, c_b)), _const_spec((1, c_b)),
                  per_seq(kq - 1, qkv_w), per_seq(kd - 1, c_b)],
        out_specs=[seq_blk(qkv_w), seq_blk(LANES), seq_blk(v_w), seq_blk(c_b),
                   per_seq(kq - 1, qkv_w), per_seq(kd - 1, c_b)],
        out_shape=[jax.ShapeDtypeStruct((n_seq, L, qkv_w), F32),
                   jax.ShapeDtypeStruct((n_seq, L, LANES), F32),
                   jax.ShapeDtypeStruct((n_seq, L, v_w), F32),
                   jax.ShapeDtypeStruct((n_seq, L, c_b), BF16),
                   jax.ShapeDtypeStruct((n_seq, kq - 1, qkv_w), F32),
                   jax.ShapeDtypeStruct((n_seq, kd - 1, c_b), F32)],
        scratch_shapes=[pltpu.VMEM((ns * nsub, QBUF_OFF + tl, qkv_w), F32),
                        pltpu.VMEM((ns * nsub, DBUF_OFF + tl, c_b), F32),
                        pltpu.VMEM((SUBLANES - 1, DBUF_OFF + tl, c_b), F32)],
        compiler_params=pltpu.CompilerParams(dimension_semantics=("arbitrary", "arbitrary"),
                                             vmem_limit_bytes=VMEM_LIMIT),
        name="even_pre",
    )(x, gain, w_all, conv_w, alog_pad, dtb_pad, dw_w, dw_b, ln_g, ln_b, qbuf, dbuf)


def _unit_lower_inverses(a_list, row, col):
    x = row ^ col
    eye = (row == col).astype(F32)
    a8 = [jnp.where(x < 8, a, 0.0) for a in a_list]
    a8_2 = [_mm(t, t) for t in a8]
    p = [eye - t for t in a8]
    a8_4 = [_mm(t, t) for t in a8_2]
    p = [t + _mm(t, sq) for t, sq in zip(p, a8_2)]
    d = [t + _mm(t, sq) for t, sq in zip(p, a8_4)]
    s = 8
    while s < CHUNK:
        off = [jnp.where((x >= s) & (x < 2 * s), a, 0.0) for a in a_list]
        od = [_mm(o, t) for o, t in zip(off, d)]
        d = [t - _mm(t, u) for t, u in zip(d, od)]
        s *= 2
    return d


def _delta_kernel(qkv_ref, gcol_ref, z_ref, onorm_ref, sin_ref, o_ref, sout_ref, *, tb, n_heads, chained):
    dk = dv = LANES
    qk_w = n_heads * dk
    n_pairs = tb // PAIR
    if chained:
        @pl.when(pl.program_id(1) == 0)
        def _():
            sout_ref[...] = sin_ref[...]

    row = lax.broadcasted_iota(jnp.int32, (PAIR, PAIR), 0)
    col = lax.broadcasted_iota(jnp.int32, (PAIR, PAIR), 1)
    same = (row ^ col) < CHUNK
    causal = same & (col <= row)
    strict = same & (col < row)
    in_chunk = [(col >= c * CHUNK) & (col < (c + 1) * CHUNK) for c in range(2)]

    tiles = [(p, h) for p in range(n_pairs) for h in range(n_heads)]
    q, k, kb, vb, gc, a, qk = {}, {}, {}, {}, {}, {}, {}
    for p, h in tiles:
        rows = slice(p * PAIR, (p + 1) * PAIR)
        q[p, h] = qkv_ref[0, rows, h * dk:(h + 1) * dk] * (dk ** -0.5)
        k[p, h] = qkv_ref[0, rows, qk_w + h * dk:qk_w + (h + 1) * dk]
        gc[p, h] = jnp.broadcast_to(gcol_ref[0, rows, h:h + 1], (PAIR, PAIR))
        bt = jnp.broadcast_to(gcol_ref[0, rows, n_heads + h:n_heads + h + 1], (PAIR, PAIR))
        kb[p, h] = k[p, h] * bt
        vb[p, h] = qkv_ref[0, rows, 2 * qk_w + h * dv:2 * qk_w + (h + 1) * dv] * bt
    for t in tiles:
        diff = gc[t] - gc[t].T
        decay = jnp.where(causal, jnp.exp(jnp.where(causal, diff, 0.0)), 0.0)
        a[t] = jnp.where(strict, _mm_nt(kb[t], k[t]) * decay, 0.0)
        qk[t] = jnp.where(causal, _mm_nt(q[t], k[t]) * decay, 0.0)
    inv = dict(zip(tiles, _unit_lower_inverses([a[t] for t in tiles], row, col)))
    uw, kg_t, gl = {}, {}, {}
    for t in tiles:
        eg = jnp.exp(gc[t])
        uw[t] = _mm(inv[t], jnp.concatenate([vb[t], kb[t] * eg], axis=1))
        gl_rows = [jnp.broadcast_to(gc[t][(c + 1) * CHUNK - 1:(c + 1) * CHUNK, :], (PAIR, PAIR)) for c in range(2)]
        gl[t] = [jnp.exp(g) for g in gl_rows]
        gcl = jnp.where(row < CHUNK, gl_rows[0], gl_rows[1])
        kg_t[t] = (k[t] * jnp.exp(gcl - gc[t])).T
        q[t] = q[t] * eg
    qk_uw = {t: _mm(qk[t], uw[t]) for t in tiles}
    kg_uw = {(t, c): _mm(jnp.where(in_chunk[c], kg_t[t], 0.0), uw[t])
             for t in tiles for c in range(2)}

    state = [sout_ref[0, h] for h in range(n_heads)] if chained else None
    for p in range(n_pairs):
        rows = slice(p * PAIR, (p + 1) * PAIR)
        o_parts = {h: [] for h in range(n_heads)}
        for c in range(2):
            r = slice(c * CHUNK, (c + 1) * CHUNK)
            for h in range(n_heads):
                t = (p, h)
                s_old = state[h] if chained else sin_ref[2 * p + c, h]
                q_eff = q[t][r] - qk_uw[t][r, dv:]
                xs = _mm(jnp.concatenate([kg_uw[t, c][:, dv:], q_eff], axis=0), s_old)
                o_parts[h].append(qk_uw[t][r, :dv] + xs[dk:])
                s_new = s_old * gl[t][c] + (kg_uw[t, c][:, :dv] - xs[:dk])
                if chained:
                    state[h] = s_new
                else:
                    sout_ref[2 * p + c, h] = s_new
        for h in range(n_heads):
            o = jnp.concatenate(o_parts[h], axis=0)
            zz = z_ref[0, rows, h * dv:(h + 1) * dv].astype(F32)
            y = (o * _rms_scale(o) * onorm_ref[...]) * _silu(zz)
            o_ref[0, rows, h * dv:(h + 1) * dv] = y.astype(o_ref.dtype)
    if chained:
        for h in range(n_heads):
            sout_ref[0, h] = state[h]


def _delta(qkv, gcol, z, onorm, s0, *, n_heads):
    n_seq, L, qkv_w = qkv.shape
    v_w = z.shape[-1]
    assert qkv_w == 3 * n_heads * LANES and v_w == n_heads * LANES
    if L % PAIR == 0:
        chained, n_grp, Lg, per_grp = True, n_seq, L, 1
    else:
        assert L == CHUNK and n_seq % 2 == 0
        per_grp = max(d for d in range(2, DELTA_ROWS // CHUNK + 1, 2) if n_seq % d == 0)
        chained, n_grp, Lg = False, n_seq // per_grp, per_grp * CHUNK
        qkv, gcol, z = (t.reshape(n_grp, Lg, t.shape[-1]) for t in (qkv, gcol, z))
    tb = min(DELTA_ROWS, Lg)
    assert Lg % tb == 0
    kern = functools.partial(_delta_kernel, tb=tb, n_heads=n_heads, chained=chained)
    blk = lambda w: pl.BlockSpec((1, tb, w), lambda s, j: (s, j, 0))
    st = pl.BlockSpec((per_grp, n_heads, LANES, LANES), lambda s, j: (s, 0, 0, 0))
    o, s_new = pl.pallas_call(
        kern,
        grid=(n_grp, Lg // tb),
        in_specs=[blk(qkv_w), blk(LANES), blk(v_w), _const_spec((1, LANES)), st],
        out_specs=[blk(v_w), st],
        out_shape=[jax.ShapeDtypeStruct((n_grp, Lg, v_w), BF16),
                   jax.ShapeDtypeStruct(s0.shape, F32)],
        compiler_params=pltpu.CompilerParams(dimension_semantics=("arbitrary", "arbitrary"),
                                             vmem_limit_bytes=VMEM_LIMIT),
        name="delta_rule",
    )(qkv, gcol, z, onorm, s0)
    return o.reshape(n_seq, L, v_w), s_new


def _out_mlp_kernel(*refs, n_a, steps, tf):
    n_s = len(steps)
    per = 1 + n_a
    w_ref, g_ref, wup_ref, wdn_ref = refs[n_s * per:n_s * per + 4]
    o_refs = refs[n_s * per + 4:]

    def tile(x_ref, a_refs, o_ref):
        a = a_refs[0][...] if n_a == 1 else jnp.concatenate([a_ref[...] for a_ref in a_refs], axis=-1)
        x1 = x_ref[...] + jnp.dot(a, w_ref[...], preferred_element_type=F32)
        h = (x1 * _rms_scale(x1) * g_ref[...]).astype(BF16)
        acc = x1
        for f in range(wup_ref.shape[-1] // tf):
            r = jnp.maximum(jnp.dot(h, wup_ref[0, :, f * tf:(f + 1) * tf], preferred_element_type=F32), 0.0)
            acc = acc + jnp.dot((r * r).astype(BF16), wdn_ref[0, f * tf:(f + 1) * tf, :],
                                preferred_element_type=F32)
        o_ref[...] = acc

    i = pl.program_id(0)
    first = 0
    for s in range(n_s):
        run = functools.partial(tile, refs[s * per], refs[s * per + 1:(s + 1) * per], o_refs[s])
        if n_s == 1:
            run()
        else:
            pl.when((i >= first) & (i < first + steps[s]))(run)
        first += steps[s]


def _out_mlp(streams, w_out, gain, w_up, w_down, layer):
    D = streams[0][0].shape[-1]
    flat = [(x.reshape(-1, D), [a.reshape(-1, a.shape[-1]) for a in a_list]) for x, a_list in streams]
    tm = min([MLP_ROWS] + [x.shape[0] for x, _ in flat])
    assert all(x.shape[0] % tm == 0 for x, _ in flat)
    steps = [x.shape[0] // tm for x, _ in flat]
    n_a = len(flat[0][1])
    F = w_up.shape[-1]
    tf = min(MLP_FF, F)
    assert F % tf == 0
    kern = functools.partial(_out_mlp_kernel, n_a=n_a, steps=tuple(steps), tf=tf)
    layer_spec = lambda shape: pl.BlockSpec((1,) + shape[1:], lambda i: (layer, 0, 0),
                                            pipeline_mode=pl.Buffered(1))

    def rows(w, first, n):
        return pl.BlockSpec((tm, w), lambda i: (jnp.minimum(jnp.maximum(i - first, 0), n - 1), 0))

    in_specs, args, out_specs, first = [], [], [], 0
    for (x, a_list), n in zip(flat, steps):
        in_specs += [rows(D, first, n)] + [rows(a.shape[1], first, n) for a in a_list]
        args += [x] + a_list
        out_specs.append(rows(D, first, n))
        first += n
    outs = pl.pallas_call(
        kern,
        grid=(sum(steps),),
        in_specs=in_specs + [_const_spec(w_out.shape), _const_spec((1, D)), layer_spec(w_up.shape),
                             layer_spec(w_down.shape)],
        out_specs=out_specs,
        out_shape=[jax.ShapeDtypeStruct(x.shape, F32) for x, _ in flat],
        compiler_params=pltpu.CompilerParams(dimension_semantics=("arbitrary",), vmem_limit_bytes=VMEM_LIMIT),
        name="out_mlp",
    )(*args, w_out, gain, w_up, w_down)
    return [o.reshape(x.shape) for o, (x, _) in zip(outs, streams)]


def _odd_pre_kernel(x_ref, g_ref, w_ref, qn_ref, kn_ref, q_ref, k_ref, v_ref, *mxu_copies, n_heads):
    dh = LANES
    hd = n_heads * dh
    ns, tl, d_model = x_ref.shape
    x = x_ref[...].reshape(ns * tl, d_model)
    h = (x * _rms_scale(x) * g_ref[...]).astype(BF16)
    q = jnp.dot(h, w_ref[:, 0:hd], preferred_element_type=F32)
    k = jnp.dot(h, w_ref[:, hd:2 * hd], preferred_element_type=F32)
    v = jnp.dot(h, w_ref[:, 2 * hd:3 * hd], preferred_element_type=F32)
    per_seq = lambda t: t.reshape(ns, tl, dh)
    for hh in range(n_heads):
        cols = slice(hh * dh, (hh + 1) * dh)
        qh = q[:, cols]
        kh = k[:, cols]
        q_ref[:, hh] = per_seq((qh * _rms_scale(qh) * qn_ref[...] * (dh ** -0.5 * LOG2E)).astype(q_ref.dtype))
        kh = kh * _rms_scale(kh) * kn_ref[...]
        k_ref[:, hh] = per_seq(kh)
        v_ref[:, hh] = per_seq(v[:, cols])
        if mxu_copies:
            mxu_copies[0][:, hh] = per_seq(kh.astype(BF16))
            mxu_copies[1][:, hh] = per_seq(v[:, cols].astype(BF16))


def _odd_pre(x, gain, w_qkv, qn, kn, *, n_heads, mxu_copies):
    B, L, D = x.shape
    tl = min(PROJ_ROWS, L)
    assert L % tl == 0 and w_qkv.shape[1] == 3 * n_heads * LANES
    kern = functools.partial(_odd_pre_kernel, n_heads=n_heads)
    ns = max(d for d in range(1, max(1, PROJ_ROWS // tl) + 1) if B % d == 0)
    head_major = pl.BlockSpec((ns, n_heads, tl, LANES), lambda b, j: (b, 0, j, 0))
    hm_shape = (B, n_heads, L, LANES)
    n_copies = 2 if mxu_copies else 0
    return pl.pallas_call(
        kern,
        grid=(B // ns, L // tl),
        in_specs=[pl.BlockSpec((ns, tl, D), lambda b, j: (b, j, 0)), _const_spec((1, D)),
                  _const_spec(w_qkv.shape), _const_spec((1, LANES)), _const_spec((1, LANES))],
        out_specs=[head_major] * (3 + n_copies),
        out_shape=([jax.ShapeDtypeStruct(hm_shape, BF16), jax.ShapeDtypeStruct(hm_shape, F32),
                    jax.ShapeDtypeStruct(hm_shape, F32)] + [jax.ShapeDtypeStruct(hm_shape, BF16)] * n_copies),
        compiler_params=pltpu.CompilerParams(dimension_semantics=("arbitrary", "arbitrary"),
                                             vmem_limit_bytes=VMEM_LIMIT),
        name="odd_pre",
    )(x, gain, w_qkv, qn, kn)


def _suffix_ones(tk):
    j = lax.broadcasted_iota(jnp.int32, (2 * tk, tk), 0) & (tk - 1)
    s = lax.broadcasted_iota(jnp.int32, (2 * tk, tk), 1)
    return (j > s).astype(BF16)


def _sb_logits(q, k_blk, u, below=None):
    zs = [_mm_nt(a, b) for a, b in zip(q, k_blk)] if isinstance(q, (list, tuple)) else [_mm_nt(q, k_blk)]
    z = jnp.concatenate(zs, axis=0) if len(zs) > 1 else zs[0]
    t = jnp.maximum(z, 0.0) + jnp.log(1.0 + jnp.exp2(-jnp.abs(z))) * LOG2E
    if below is not None:
        t = jnp.where(below, t, 0.0)
    t_hi = t.astype(BF16)
    t_lo = (t - t_hi.astype(F32)).astype(BF16)
    later = jnp.dot(jnp.concatenate([t_hi, t_lo], axis=1), u, preferred_element_type=F32)
    return z, t, later


def _sb_weights(z, t, later, off, v_blk, acc):
    wts = jnp.exp2(z - t - (later + off))
    own = later[:, 0:1] + t[:, 0:1]
    if not isinstance(v_blk, (list, tuple)):
        return own, acc + _mm(wts, v_blk)
    rows = z.shape[0] // len(v_blk)
    return own, [a + _mm(wts[n * rows:(n + 1) * rows], v) for n, (a, v) in enumerate(zip(acc, v_blk))]


def _sb_block(q, k_blk, v_blk, u, off, acc, below=None):
    return _sb_weights(*_sb_logits(q, k_blk, u, below), off, v_blk, acc)


def _sb_kernel(*refs, tq, tkp, n_past, n_q, hb, qb, n_prev):
    lazy_past = bool(n_past) and n_q == 1
    if lazy_past:
        q_ref, kn_ref, vn_ref, kl_ref, vl_ref, kp_ref, vp_ref, o_ref, kbuf, vbuf, sem = refs
    elif n_past:
        q_ref, kn_ref, vn_ref, kp_ref, vp_ref, o_ref = refs
    else:
        q_ref, kn_ref, vn_ref, o_ref = refs
    u_self = _suffix_ones(tq)
    u_past = _suffix_ones(tkp) if n_past else None
    row = lax.broadcasted_iota(jnp.int32, (tq, tq), 0)
    col = lax.broadcasted_iota(jnp.int32, (tq, tq), 1)
    below = col < row
    not_below = jnp.where(below, 0.0, MASKED)

    walks = []
    if n_q == 1:
        heads = range(hb)
        qs = [q_ref[0, hh] for hh in heads]
        below_all = jnp.concatenate([below] * hb, axis=0)
        spent, accs = _sb_block(qs, [kn_ref[0, hh] for hh in heads], [vn_ref[0, hh] for hh in heads], u_self,
                                jnp.concatenate([not_below] * hb, axis=0),
                                [jnp.zeros((tq, LANES), F32)] * hb, below_all)
        if lazy_past:
            own, accs = _sb_block(qs, [kl_ref[0, hh] for hh in heads], [vl_ref[0, hh] for hh in heads], u_past,
                                  spent, accs)
            spent = spent + own
        walks = [(hh, 0, qs[hh], spent[hh * tq:(hh + 1) * tq], accs[hh], -1, n_past - 2) for hh in heads]
    chains = [(hh, w) for hh in range(hb) for w in range(qb) if n_q > 1]
    logits = {}
    for c, (hh, w) in enumerate(chains):
        i = pl.program_id(2) * qb + w
        q = q_ref[0, hh, w * tq:(w + 1) * tq, :]
        for d in range(n_prev + 1):
            blk = pl.multiple_of(jnp.maximum(i - d, 0) * tq, tq)
            logits[c, d] = _sb_logits(q, kn_ref[0, hh, pl.ds(blk, tq), :], u_self, below if d == 0 else None)
    state = {}
    for d in range(n_prev + 1):
        for c, (hh, w) in enumerate(chains):
            i = pl.program_id(2) * qb + w
            blk = pl.multiple_of(jnp.maximum(i - d, 0) * tq, tq)
            v_blk = vn_ref[0, hh, pl.ds(blk, tq), :]
            if d == 0:
                state[c] = _sb_weights(*logits[c, 0], not_below, v_blk, jnp.zeros((tq, LANES), F32))
            else:
                spent, acc = state[c]
                has_prev = jnp.full((tq, 1), i, jnp.int32) >= d
                own, acc = _sb_weights(*logits[c, d], jnp.where(has_prev, spent, MASKED), v_blk, acc)
                state[c] = (spent + jnp.where(has_prev, own, 0.0), acc)
    for c, (hh, w) in enumerate(chains):
        i = pl.program_id(2) * qb + w
        q = q_ref[0, hh, w * tq:(w + 1) * tq, :]
        walks.append((hh, w, q, *state[c], i - n_prev - 1, n_past - 1))

    def walk(q, k_ref, v_ref, hh, tk, u, first, spent, acc, from_hbm=False):
        def cond(st):
            return (st[0] >= 0) & (jnp.min(st[1]) < -LOG2_W_FLOOR)

        def body(st):
            j, spent, acc = st
            s0 = pl.multiple_of(j * tk, tk)
            if from_hbm:
                head = pl.program_id(1) * hb + hh
                copies = [pltpu.make_async_copy(src.at[pl.program_id(0), head, pl.ds(s0, tk), :], dst, sem.at[n])
                          for n, (src, dst) in enumerate(((k_ref, kbuf), (v_ref, vbuf)))]
                for cp in copies:
                    cp.start()
                for cp in copies:
                    cp.wait()
                k_blk, v_blk = kbuf[...], vbuf[...]
            else:
                k_blk, v_blk = k_ref[0, hh, pl.ds(s0, tk), :], v_ref[0, hh, pl.ds(s0, tk), :]
            own, acc = _sb_block(q, k_blk, v_blk, u, spent, acc)
            return j - 1, spent + own, acc

        _, spent, acc = lax.while_loop(cond, body, (first, spent, acc))
        return spent, acc

    for hh, w, _, _, acc, _, _ in walks:
        o_ref[0, w * tq:(w + 1) * tq, hh * LANES:(hh + 1) * LANES] = acc.astype(o_ref.dtype)
    if n_q == 1 and n_past <= 1:
        return

    least = functools.reduce(jnp.minimum, [wk[3] for wk in walks])

    @pl.when(jnp.min(least) < -LOG2_W_FLOOR)
    def _():
        for hh, w, q, spent, acc, first_self, first_past in walks:
            if n_q > 1:
                spent, acc = walk(q, kn_ref, vn_ref, hh, tq, u_self, first_self, spent, acc)
            if n_past:
                spent, acc = walk(q, kp_ref, vp_ref, hh, tkp, u_past, jnp.int32(first_past), spent, acc,
                                  from_hbm=lazy_past)
            o_ref[0, w * tq:(w + 1) * tq, hh * LANES:(hh + 1) * LANES] = acc.astype(o_ref.dtype)


def _stick_breaking(q, k_new, v_new, k_past, v_past):
    B, H, L, dh = q.shape
    assert dh == LANES
    tq = min(SB_TQ, L)
    assert L % tq == 0
    n_q = L // tq
    assert tq & (tq - 1) == 0
    if n_q == 1:
        hb, qb = H, 1
    else:
        kv_bytes = 2 * 2 * L * LANES * k_new.dtype.itemsize
        hb = max(d for d in (4, 2, 1) if H % d == 0 and d * kv_bytes <= SB_KV_VMEM)
        qb = max(d for d in (SB_CHAINS // hb, 2, 1) if n_q % d == 0)
    n_prev = 0 if n_q == 1 else SB_PREV
    assert H % hb == 0 and n_q % qb == 0
    args = [q, k_new, v_new]
    full = lambda n: pl.BlockSpec((1, hb, n, LANES), lambda b, h, i: (b, h, 0, 0))
    in_specs = [pl.BlockSpec((1, hb, tq * qb, LANES), lambda b, h, i: (b, h, i, 0)), full(L), full(L)]
    n_past, tkp = 0, 0
    scratch = []
    if k_past is not None:
        P = k_past.shape[2]
        tkp = min(PROJ_ROWS, P)
        assert P % tkp == 0 and tkp & (tkp - 1) == 0
        n_past = P // tkp
        if n_q == 1:
            last = pl.BlockSpec((1, hb, tkp, LANES), lambda b, h, i: (b, h, n_past - 1, 0))
            hbm = pl.BlockSpec(memory_space=pl.ANY)
            args += [k_past, v_past, k_past, v_past]
            in_specs += [last, last, hbm, hbm]
            scratch = [pltpu.VMEM((tkp, LANES), k_past.dtype), pltpu.VMEM((tkp, LANES), v_past.dtype),
                       pltpu.SemaphoreType.DMA((2,))]
        else:
            args += [k_past, v_past]
            in_specs += [full(P), full(P)]
    kern = functools.partial(_sb_kernel, tq=tq, tkp=tkp, n_past=n_past, n_q=n_q, hb=hb, qb=qb, n_prev=n_prev)
    return pl.pallas_call(
        kern,
        grid=(B, H // hb, n_q // qb),
        in_specs=in_specs,
        out_specs=pl.BlockSpec((1, tq * qb, hb * LANES), lambda b, h, i: (b, i, h)),
        scratch_shapes=scratch,
        out_shape=jax.ShapeDtypeStruct((B, L, H * LANES), BF16),
        compiler_params=pltpu.CompilerParams(dimension_semantics=("arbitrary", "arbitrary", "arbitrary"),
                                             vmem_limit_bytes=VMEM_LIMIT),
        name="stick_breaking",
    )(*args)


def _stack(parts):
    return parts[0][None] if len(parts) == 1 else jnp.stack(parts)


def _pad_lanes(v):
    return jnp.zeros((1, LANES), F32).at[0, :v.shape[0]].set(v.astype(F32))


def _even_mixers(x, s0, qbuf, dbuf, wts, gain_mix):
    (w_all, conv_w, alog_pad, dtb_pad, onorm, dw_w, dw_b, ln_g, ln_b, n_heads, qkv_w, v_w, c_b) = wts
    qkv, gcol, z, c, qlast, dlast = _even_pre(x, gain_mix, w_all, conv_w, alog_pad, dtb_pad, dw_w, dw_b, ln_g, ln_b,
                                              qbuf, dbuf, n_heads=n_heads, qkv_w=qkv_w, v_w=v_w, c_b=c_b)
    o, s_new = _delta(qkv, gcol, z, onorm, s0, n_heads=n_heads)
    return [o, c], s_new, qlast, dlast


def _odd_mixer(x, k_past, v_past, wts, gain_mix):
    w_qkv, qn, kn, n_heads = wts
    mxu_copies = x.shape[1] > SB_TQ
    q, k, v, *kv_mxu = _odd_pre(x, gain_mix, w_qkv, qn, kn, n_heads=n_heads, mxu_copies=mxu_copies)
    o = _stick_breaking(q, *(kv_mxu or (k, v)), k_past, v_past)
    return [o], k, v


def kernel(x_prompt, x_sample, state_delta, state_qkv_conv, state_dw_conv, cache_k, cache_v, norm_mix, norm_mlp, w_in_e, conv_qkv_e, a_log_e, dt_bias_e, onorm_e, dw_w_e, dw_b_e, ln_g_e, ln_b_e, w_out_e, w_qkv_o, qn_o, kn_o, w_out_o, w_up, w_down):
    depth = norm_mix.shape[0]
    bp = x_prompt.shape[0]
    xp, xs = x_prompt, x_sample
    pd, pq, pw, pk, pv = [], [], [], [], []
    sd, sq, sw, sk, sv = [], [], [], [], []
    row = lambda v: v.astype(F32).reshape(1, -1)
    wu, wd = w_up.astype(BF16), w_down.astype(BF16)
    for i in range(depth):
        gm, gl = row(norm_mix[i]), row(norm_mlp[i])
        if i % 2 == 0:
            e = i // 2
            n_heads = a_log_e.shape[1]
            qkv_w = conv_qkv_e.shape[2]
            c_b = dw_w_e.shape[2]
            v_w = n_heads * onorm_e.shape[1]
            w_in = w_in_e[e]
            ab0 = qkv_w + v_w
            ab = jnp.zeros((w_in.shape[0], LANES), w_in.dtype).at[:, :2 * n_heads].set(w_in[:, ab0:ab0 + 2 * n_heads])
            w_all = jnp.concatenate([w_in[:, :ab0], w_in[:, ab0 + 2 * n_heads:], ab], axis=1).astype(BF16)
            w_out = w_out_e[e].astype(BF16)
            wts = (w_all, conv_qkv_e[e].astype(F32), _pad_lanes(a_log_e[e]), _pad_lanes(dt_bias_e[e]),
                   row(onorm_e[e]), dw_w_e[e].astype(F32), row(dw_b_e[e]), row(ln_g_e[e]), row(ln_b_e[e]),
                   n_heads, qkv_w, v_w, c_b)
            kq, kd = conv_qkv_e.shape[1], dw_w_e.shape[1]
            s0 = jnp.zeros((bp,) + state_delta.shape[2:], F32)
            qb0 = jnp.zeros((bp, kq - 1, qkv_w), F32)
            db0 = jnp.zeros((bp, kd - 1, c_b), F32)
            ap, d1, q1, c1 = _even_mixers(xp, s0, qb0, db0, wts, gm)
            as_, d2, q2, c2 = _even_mixers(xs, state_delta[e].astype(F32), state_qkv_conv[e].astype(F32),
                                           state_dw_conv[e].astype(F32), wts, gm)
            pd.append(d1); pq.append(q1); pw.append(c1)
            sd.append(d2); sq.append(q2); sw.append(c2)
        else:
            o = i // 2
            n_heads = cache_k.shape[2]
            w_out = w_out_o[o].astype(BF16)
            wts = (w_qkv_o[o].astype(BF16), row(qn_o[o]), row(kn_o[o]), n_heads)
            ap, k1, v1 = _odd_mixer(xp, None, None, wts, gm)
            as_, k2, v2 = _odd_mixer(xs, cache_k[o].astype(F32), cache_v[o].astype(F32), wts, gm)
            pk.append(k1); pv.append(v1)
            sk.append(k2); sv.append(v2)
        xp, xs = _out_mlp([(xp, ap), (xs, as_)], w_out, gl, wu, wd, i)
    return (xp, xs,
            *(_stack(t) for t in (pd, pq, pw, pk, pv, sd, sq, sw, sk, sv)))
```
